```python
import jax
import jax.numpy as jnp
from jax import lax
import numpy as np

D_MODEL = 1024
BATCH = 32
SEQ = 256
DEPTH = 2
DEC_BATCH = 2
DEC_SEQ = 2048
PAST_LEN = 512

GRID_W = 64
N_BRANCH = 4
BRANCH_W = 256
MLA_HEADS = 4
MLA_Q_LORA = 256
MLA_KV_LORA = 128
MLA_NOPE = 64
MLA_ROPE = 32
MLA_V = 64
MLA_SCALE = (MLA_NOPE + MLA_ROPE) ** -0.5
FNET_GROUPS = 4
FNET_CH = BRANCH_W // FNET_GROUPS
GLA_HEADS = 4
GLA_DK = 32
GLA_DV = 64
GLA_GATE_RANK = 16
GLA_TAU = 16.0
GLA_CHUNK = 64
SWA_HEADS = 4
SWA_KV_HEADS = 2
SWA_GROUP = SWA_HEADS // SWA_KV_HEADS
SWA_HEAD_DIM = 64
SWA_WINDOW = 128
SWA_SCALE = SWA_HEAD_DIM ** -0.5
ATTN_BLOCK = 128
PEER_HEADS = 8
PEER_N_KEYS = 128
PEER_N_EXPERTS = PEER_N_KEYS * PEER_N_KEYS
PEER_KEY_DIM = 256
PEER_HALF = PEER_KEY_DIM // 2
PEER_TOPK = 16
PEER_TOKEN_BLOCK = 128

ROPE_THETA = 10000.0
NORM_EPS = 1e-6
DEEPNORM_ALPHA = (2.0 * DEPTH) ** 0.25
DEEPNORM_BETA = (8.0 * DEPTH) ** -0.25

IN_SPLITS = (
    ('mla_q', MLA_Q_LORA),
    ('mla_kv', MLA_KV_LORA + MLA_ROPE),
    ('fnet', BRANCH_W),
    ('gla_q', GLA_HEADS * GLA_DK),
    ('gla_k', GLA_HEADS * GLA_DK),
    ('gla_v', GLA_HEADS * GLA_DV),
    ('gla_g', BRANCH_W),
    ('gla_af', GLA_GATE_RANK),
    ('gla_ab', GLA_GATE_RANK),
    ('swa_q', SWA_HEADS * SWA_HEAD_DIM),
    ('swa_k', SWA_KV_HEADS * SWA_HEAD_DIM),
    ('swa_v', SWA_KV_HEADS * SWA_HEAD_DIM),
    ('gates', N_BRANCH * D_MODEL),
)
IN_NAMES = tuple(n for n, _ in IN_SPLITS)
IN_OFFSETS = tuple(int(o) for o in np.cumsum([w for _, w in IN_SPLITS])[:-1])
IN_WIDTH = int(sum(w for _, w in IN_SPLITS))

kernel_name = 'hybrid_diffusion_mla_fnet_gla_swa_peer_step'


def layer_norm(x, g=None, b=None):
    xf = x.astype(jnp.float32)
    mu = jnp.mean(xf, -1, keepdims=True)
    var = jnp.mean(jnp.square(xf - mu), -1, keepdims=True)
    y = (xf - mu) * lax.rsqrt(var + NORM_EPS)
    if g is not None:
        y = y * g.astype(jnp.float32) + b.astype(jnp.float32)
    return y.astype(x.dtype)


def rms_norm(x, g):
    xf = x.astype(jnp.float32)
    y = xf * lax.rsqrt(jnp.mean(xf * xf, -1, keepdims=True) + NORM_EPS) * g.astype(jnp.float32)
    return y.astype(x.dtype)


def axial_rope(x):
    n = x.shape[-2]
    half = x.shape[-1] // 2
    t = jnp.arange(n)
    rows = (t // GRID_W).astype(jnp.float32)
    cols = (t % GRID_W).astype(jnp.float32)
    freqs = ROPE_THETA ** (-jnp.arange(0, half, 2, dtype=jnp.float32) / half)

    def rot(xa, pos):
        ang = pos[:, None] * freqs[None, :]
        cos, sin = jnp.cos(ang), jnp.sin(ang)
        x1, x2 = xa[..., :half // 2], xa[..., half // 2:]
        return jnp.concatenate([x1 * cos - x2 * sin, x2 * cos + x1 * sin], -1)

    xf = x.astype(jnp.float32)
    return jnp.concatenate([rot(xf[..., :half], rows), rot(xf[..., half:], cols)], -1).astype(x.dtype)


def softmax_with_sink(s, sink):
    if sink is None:
        return jax.nn.softmax(s, -1)
    s_all = jnp.concatenate([s, jnp.broadcast_to(sink, s[..., :1].shape)], -1)
    return jax.nn.softmax(s_all, -1)[..., :-1]


def dense_attention(q, k, v, scale, sink=None):
    b, hk, g, sq, dk = q.shape
    nb = sq // ATTN_BLOCK
    qb = jnp.moveaxis(q.reshape(b, hk, g, nb, ATTN_BLOCK, dk), 3, 0)
    sink_b = None if sink is None else sink.astype(jnp.float32)[None, :, :, None, None]

    def one_block(qi):
        s = jnp.einsum('bhgqd,bhkd->bhgqk', qi, k, preferred_element_type=jnp.float32) * scale
        p = softmax_with_sink(s, sink_b)
        return jnp.einsum('bhgqk,bhkd->bhgqd', p.astype(v.dtype), v)

    o = lax.map(one_block, qb)
    return jnp.moveaxis(o, 0, 3).reshape(b, hk, g, sq, v.shape[-1])


def windowed_attention(q, k, v, k_ctx, v_ctx, sink):
    b, hk, g, s, dh = q.shape
    blk = SWA_WINDOW
    nb = s // blk
    qb = q.reshape(b, hk, g, nb, blk, dh)

    def band(x):
        xp = jnp.pad(x, ((0, 0), (0, 0), (blk, blk), (0, 0))).reshape(b, hk, nb + 2, blk, x.shape[-1])
        return jnp.concatenate([xp[:, :, :-2], xp[:, :, 1:-1], xp[:, :, 2:]], axis=3)

    kb, vb = band(k), band(v)
    qpos = jnp.arange(nb)[:, None] * blk + jnp.arange(blk)[None, :]
    kpos = (jnp.arange(nb)[:, None] - 1) * blk + jnp.arange(3 * blk)[None, :]
    rel = kpos[:, None, :] - qpos[:, :, None]
    valid = (jnp.abs(rel) <= SWA_WINDOW) & (kpos[:, None, :] >= 0) & (kpos[:, None, :] < s)
    s_band = jnp.einsum('bhgnqd,bhnkd->bhgnqk', qb, kb, preferred_element_type=jnp.float32) * SWA_SCALE
    s_band = jnp.where(valid, s_band, -jnp.inf)
    s_ctx = jnp.einsum('bhgnqd,bhld->bhgnql', qb, k_ctx, preferred_element_type=jnp.float32) * SWA_SCALE
    p = softmax_with_sink(jnp.concatenate([s_band, s_ctx], -1),
                          sink.astype(jnp.float32)[None, :, :, None, None, None])
    p_band, p_ctx = p[..., :3 * blk], p[..., 3 * blk:]
    o = (jnp.einsum('bhgnqk,bhnkd->bhgnqd', p_band.astype(v.dtype), vb)
         + jnp.einsum('bhgnql,bhld->bhgnqd', p_ctx.astype(v.dtype), v_ctx))
    return o.reshape(b, hk, g, s, dh)


def gla_scan(q, k, v, log_a, s0):
    b, s, h, dk = q.shape
    dv = v.shape[-1]
    n = s // GLA_CHUNK
    f32 = jnp.float32
    qc, kc, vc, ac = [t.astype(f32).reshape(b, n, GLA_CHUNK, h, t.shape[-1]) for t in (q, k, v, log_a)]
    cum = jnp.cumsum(ac, axis=2)
    causal = jnp.tril(jnp.ones((GLA_CHUNK, GLA_CHUNK), dtype=bool))
    decay = jnp.exp(jnp.minimum(cum[:, :, :, None] - cum[:, :, None, :], 0.0))
    att = jnp.sum(qc[:, :, :, None] * kc[:, :, None, :] * decay, -1)
    att = jnp.where(causal[:, :, None], att, 0.0)
    o_intra = jnp.einsum('bntsh,bnshv->bnthv', att, vc)
    last = cum[:, :, -1:]
    u = jnp.einsum('bnshk,bnshv->bnhkv', kc * jnp.exp(last - cum), vc)
    g = jnp.exp(last[:, :, 0])

    def step(state, xs):
        g_n, u_n = xs
        return g_n[..., None] * state + u_n, state

    s_final, s_in = lax.scan(step, s0.astype(f32), (jnp.moveaxis(g, 1, 0), jnp.moveaxis(u, 1, 0)))
    s_in = jnp.moveaxis(s_in, 0, 1)
    o_inter = jnp.einsum('bnthk,bnhkv->bnthv', qc * jnp.exp(cum), s_in)
    return (o_intra + o_inter).reshape(b, s, h, dv).astype(v.dtype), s_final.astype(v.dtype)


def gla_bidirectional(parts, lp, s0_fwd, s0_bwd):
    b, s, _ = parts['gla_q'].shape
    q = parts['gla_q'].reshape(b, s, GLA_HEADS, GLA_DK) * (GLA_DK ** -0.5)
    k = parts['gla_k'].reshape(b, s, GLA_HEADS, GLA_DK)
    v = parts['gla_v'].reshape(b, s, GLA_HEADS, GLA_DV)

    def log_decay(a_low, w2, b2):
        z = (a_low @ w2 + b2).astype(jnp.float32)
        return (jax.nn.log_sigmoid(z) / GLA_TAU).reshape(b, s, GLA_HEADS, GLA_DK)

    la_f = log_decay(parts['gla_af'], lp['w_gla_a_fwd'], lp['b_gla_a_fwd'])
    la_b = log_decay(parts['gla_ab'], lp['w_gla_a_bwd'], lp['b_gla_a_bwd'])
    o_f, s_f = gla_scan(q, k, v, la_f, s0_fwd)
    flip = lambda t: jnp.flip(t, axis=1)
    o_b, s_b = gla_scan(flip(q), flip(k), flip(v), flip(la_b), s0_bwd)
    o = rms_norm(o_f + flip(o_b), lp['gla_norm']).reshape(b, s, BRANCH_W)
    return o * jax.nn.silu(parts['gla_g']), s_f, s_b


def fourier_mix(f):
    b, s, _ = f.shape
    fg = f.astype(jnp.float32).reshape(b, s, FNET_GROUPS, FNET_CH)
    return jnp.fft.fft2(fg, axes=(1, 3), norm='ortho').real.reshape(b, s, BRANCH_W).astype(f.dtype)


def mla_project(parts, lp, positional):
    b, s, _ = parts['mla_q'].shape
    q = (rms_norm(parts['mla_q'], lp['mla_q_norm']) @ lp['w_uq'])
    q = q.reshape(b, s, MLA_HEADS, MLA_NOPE + MLA_ROPE).transpose(0, 2, 1, 3)
    q_nope, q_rope = q[..., :MLA_NOPE], q[..., MLA_NOPE:]
    ckv = rms_norm(parts['mla_kv'][..., :MLA_KV_LORA], lp['mla_kv_norm'])
    k_rope = parts['mla_kv'][..., MLA_KV_LORA:]
    if positional:
        q_rope = axial_rope(q_rope)
        k_rope = axial_rope(k_rope)
    return jnp.concatenate([q_nope, q_rope], -1), ckv, k_rope


def mla_expand(ckv, k_rope, w_ukv):
    b, s, _ = ckv.shape
    kv = (ckv @ w_ukv).reshape(b, s, MLA_HEADS, MLA_NOPE + MLA_V).transpose(0, 2, 1, 3)
    k_nope, v = kv[..., :MLA_NOPE], kv[..., MLA_NOPE:]
    k = jnp.concatenate([k_nope, jnp.broadcast_to(k_rope[:, None], (b, MLA_HEADS, s, MLA_ROPE))], -1)
    return k, v


def mla_attend(q, k, v):
    o = dense_attention(q[:, :, None], k, v, MLA_SCALE)[:, :, 0]
    b, h, s, dv = o.shape
    return o.transpose(0, 2, 1, 3).reshape(b, s, h * dv)


def swa_project(parts, positional):
    b, s, _ = parts['swa_q'].shape
    q = parts['swa_q'].reshape(b, s, SWA_KV_HEADS, SWA_GROUP, SWA_HEAD_DIM).transpose(0, 2, 3, 1, 4)
    k = parts['swa_k'].reshape(b, s, SWA_KV_HEADS, SWA_HEAD_DIM).transpose(0, 2, 1, 3)
    v = parts['swa_v'].reshape(b, s, SWA_KV_HEADS, SWA_HEAD_DIM).transpose(0, 2, 1, 3)
    if positional:
        q = axial_rope(q)
        k = axial_rope(k)
    return q, k, v


def swa_merge_heads(o):
    b, hk, g, s, dh = o.shape
    return o.transpose(0, 3, 1, 2, 4).reshape(b, s, hk * g * dh)


def merge_branches(branches, gates, lp):
    b, s = branches.shape[:2]
    proj = jnp.einsum('bsnc,ncd->bsnd', branches, lp['w_branch'])
    g = jax.nn.sigmoid(gates.reshape(b, s, N_BRANCH, D_MODEL))
    return jnp.sum(g * proj, axis=2) @ lp['w_out']


def context_mixer(u, lp):
    b, s, _ = u.shape
    parts = dict(zip(IN_NAMES, jnp.split(u @ lp['w_in'], IN_OFFSETS, axis=-1)))
    q_a, ckv, k_rope = mla_project(parts, lp, positional=False)
    k_a, v_a = mla_expand(ckv, k_rope, lp['w_ukv'])
    o_a = mla_attend(q_a, k_a, v_a)
    o_b = fourier_mix(parts['fnet'])
    zero = jnp.zeros((b, GLA_HEADS, GLA_DK, GLA_DV), jnp.float32)
    o_c, s_f, s_b = gla_bidirectional(parts, lp, zero, zero)
    q_d, k_d, v_d = swa_project(parts, positional=False)
    o_d = swa_merge_heads(dense_attention(q_d, k_d, v_d, SWA_SCALE,
                                          lp['swa_sink'].reshape(SWA_KV_HEADS, SWA_GROUP)))
    mix = merge_branches(jnp.stack([o_a, o_b, o_c, o_d], axis=2), parts['gates'], lp)
    return mix, (ckv, k_rope, k_d, v_d, jnp.stack([s_f, s_b], axis=1))


def latent_mixer(u, lp, cache):
    ckv_c, krope_c, k_ctx, v_ctx, gla_state = cache
    parts = dict(zip(IN_NAMES, jnp.split(u @ lp['w_in'], IN_OFFSETS, axis=-1)))
    q_a, ckv, k_rope = mla_project(parts, lp, positional=True)
    k_lat, v_lat = mla_expand(ckv, k_rope, lp['w_ukv'])
    k_c, v_c = mla_expand(ckv_c, krope_c, lp['w_ukv'])
    o_a = mla_attend(q_a, jnp.concatenate([k_c, k_lat], axis=2), jnp.concatenate([v_c, v_lat], axis=2))
    o_b = fourier_mix(parts['fnet'])
    o_c, _, _ = gla_bidirectional(parts, lp, gla_state[:, 0], gla_state[:, 1])
    q_d, k_d, v_d = swa_project(parts, positional=True)
    o_d = swa_merge_heads(windowed_attention(q_d, k_d, v_d, k_ctx, v_ctx,
                                             lp['swa_sink'].reshape(SWA_KV_HEADS, SWA_GROUP)))
    return merge_branches(jnp.stack([o_a, o_b, o_c, o_d], axis=2), parts['gates'], lp)


def peer(u, lp):
    b, s, d = u.shape
    t = b * s
    xt = u.reshape(t, d)
    q = (xt @ lp['w_peer_q']).reshape(t, PEER_HEADS, 2, PEER_HALF)
    sc = jnp.einsum('thpc,hpkc->thpk', q, lp['peer_keys'], preferred_element_type=jnp.float32)
    v1, i1 = lax.top_k(sc[:, :, 0], PEER_TOPK)
    v2, i2 = lax.top_k(sc[:, :, 1], PEER_TOPK)
    cand = (v1[..., :, None] + v2[..., None, :]).reshape(t, PEER_HEADS, PEER_TOPK * PEER_TOPK)
    cidx = (i1[..., :, None] * PEER_N_KEYS + i2[..., None, :]).reshape(t, PEER_HEADS, PEER_TOPK * PEER_TOPK)
    top, pos = lax.top_k(cand, PEER_TOPK)
    idx = jnp.take_along_axis(cidx, pos, axis=-1)
    w = jax.nn.softmax(top, axis=-1)
    nb = t // PEER_TOKEN_BLOCK
    hk = PEER_HEADS * PEER_TOPK
    tab_u, tab_v = lp['peer_u'], lp['peer_v']

    def block(args):
        xb, ib, wb = args
        act = jax.nn.gelu(jnp.einsum('td,tkd->tk', xb, tab_u[ib]), approximate=False)
        return jnp.einsum('tk,tkd->td', (wb * act).astype(xb.dtype), tab_v[ib])

    out = lax.map(block, (xt.reshape(nb, PEER_TOKEN_BLOCK, d),
                          idx.reshape(nb, PEER_TOKEN_BLOCK, hk),
                          w.reshape(nb, PEER_TOKEN_BLOCK, hk)))
    return out.reshape(b, s, d).astype(u.dtype)


def trunk_layer(x, cond, lp, cache):
    m = jax.nn.silu(cond) @ lp['w_ada'] + lp['b_ada']
    sh1, sc1, g1, sh2, sc2, g2 = jnp.split(m[:, None, :], 6, axis=-1)
    u = layer_norm(x) * (1 + sc1) + sh1
    if cache is None:
        mix, new_cache = context_mixer(u, lp)
    else:
        mix, new_cache = latent_mixer(u, lp, cache), None
    x = layer_norm(DEEPNORM_ALPHA * x + g1 * mix, lp['ln1_g'], lp['ln1_b'])
    u = layer_norm(x) * (1 + sc2) + sh2
    x = layer_norm(DEEPNORM_ALPHA * x + g2 * peer(u, lp), lp['ln2_g'], lp['ln2_b'])
    return x, new_cache


def setup_inputs(seed: int = 0) -> dict:
    key = jax.random.key(seed)
    ks = iter(jax.random.split(key, 48))
    nrm = lambda shape, scale: jax.random.normal(next(ks), shape, jnp.float32) * scale
    L, D = DEPTH, D_MODEL
    return {
        'x_prompt': nrm((BATCH, SEQ, D), 1.0),
        'x_sample': nrm((DEC_BATCH, DEC_SEQ, D), 1.0),
        'c': nrm((DEC_BATCH, D), 1.0),
        'cache_mla_ckv': nrm((DEC_BATCH, L, PAST_LEN, MLA_KV_LORA), 1.0),
        'cache_mla_krope': nrm((DEC_BATCH, L, PAST_LEN, MLA_ROPE), 1.0),
        'cache_swa_k': nrm((DEC_BATCH, L, SWA_KV_HEADS, PAST_LEN, SWA_HEAD_DIM), 1.0),
        'cache_swa_v': nrm((DEC_BATCH, L, SWA_KV_HEADS, PAST_LEN, SWA_HEAD_DIM), 1.0),
        'state_gla': nrm((DEC_BATCH, L, 2, GLA_HEADS, GLA_DK, GLA_DV), 1.0),
        'c_ctx': nrm((D,), 1.0),
        'w_ada': nrm((L, D, 6 * D), 0.5 * D ** -0.5),
        'b_ada': nrm((L, 6 * D), 0.01),
        'w_in': nrm((L, D, IN_WIDTH), D ** -0.5),
        'mla_q_norm': 1.0 + nrm((L, MLA_Q_LORA), 0.02),
        'w_uq': nrm((L, MLA_Q_LORA, MLA_HEADS * (MLA_NOPE + MLA_ROPE)), MLA_Q_LORA ** -0.5),
        'mla_kv_norm': 1.0 + nrm((L, MLA_KV_LORA), 0.02),
        'w_ukv': nrm((L, MLA_KV_LORA, MLA_HEADS * (MLA_NOPE + MLA_V)), MLA_KV_LORA ** -0.5),
        'w_gla_a_fwd': nrm((L, GLA_GATE_RANK, GLA_HEADS * GLA_DK), GLA_GATE_RANK ** -0.5),
        'b_gla_a_fwd': nrm((L, GLA_HEADS * GLA_DK), 0.01),
        'w_gla_a_bwd': nrm((L, GLA_GATE_RANK, GLA_HEADS * GLA_DK), GLA_GATE_RANK ** -0.5),
        'b_gla_a_bwd': nrm((L, GLA_HEADS * GLA_DK), 0.01),
        'gla_norm': 1.0 + nrm((L, GLA_DV), 0.02),
        'swa_sink': nrm((L, SWA_HEADS), 0.1),
        'w_branch': nrm((L, N_BRANCH, BRANCH_W, D), BRANCH_W ** -0.5),
        'w_out': nrm((L, D, D), DEEPNORM_BETA * D ** -0.5),
        'ln1_g': 1.0 + nrm((L, D), 0.02),
        'ln1_b': nrm((L, D), 0.02),
        'ln2_g': 1.0 + nrm((L, D), 0.02),
        'ln2_b': nrm((L, D), 0.02),
        'w_peer_q': nrm((L, D, PEER_HEADS * PEER_KEY_DIM), D ** -0.5),
        'peer_keys': nrm((L, PEER_HEADS, 2, PEER_N_KEYS, PEER_HALF), PEER_HALF ** -0.5),
        'peer_u': nrm((L, PEER_N_EXPERTS, D), D ** -0.5),
        'peer_v': nrm((L, PEER_N_EXPERTS, D), DEEPNORM_BETA * PEER_HEADS ** -0.5),
    }


def reference(x_prompt, x_sample, c, cache_mla_ckv, cache_mla_krope, cache_swa_k, cache_swa_v, state_gla,
              c_ctx, w_ada, b_ada, w_in, mla_q_norm, w_uq, mla_kv_norm, w_ukv,
              w_gla_a_fwd, b_gla_a_fwd, w_gla_a_bwd, b_gla_a_bwd, gla_norm, swa_sink,
              w_branch, w_out, ln1_g, ln1_b, ln2_g, ln2_b, w_peer_q, peer_keys, peer_u, peer_v):
    def layer_params(l):
        return {
            'w_ada': w_ada[l], 'b_ada': b_ada[l], 'w_in': w_in[l],
            'mla_q_norm': mla_q_norm[l], 'w_uq': w_uq[l], 'mla_kv_norm': mla_kv_norm[l], 'w_ukv': w_ukv[l],
            'w_gla_a_fwd': w_gla_a_fwd[l], 'b_gla_a_fwd': b_gla_a_fwd[l],
            'w_gla_a_bwd': w_gla_a_bwd[l], 'b_gla_a_bwd': b_gla_a_bwd[l], 'gla_norm': gla_norm[l],
            'swa_sink': swa_sink[l], 'w_branch': w_branch[l], 'w_out': w_out[l],
            'ln1_g': ln1_g[l], 'ln1_b': ln1_b[l], 'ln2_g': ln2_g[l], 'ln2_b': ln2_b[l],
            'w_peer_q': w_peer_q[l], 'peer_keys': peer_keys[l], 'peer_u': peer_u[l], 'peer_v': peer_v[l],
        }

    h = x_prompt
    ctx_states = []
    for l in range(DEPTH):
        h, st = trunk_layer(h, c_ctx[None, :], layer_params(l), None)
        ctx_states.append(st)
    y_prompt = h
    new_mla_ckv = jnp.stack([st[0] for st in ctx_states], axis=1)
    new_mla_krope = jnp.stack([st[1] for st in ctx_states], axis=1)
    new_swa_k = jnp.stack([st[2] for st in ctx_states], axis=1)
    new_swa_v = jnp.stack([st[3] for st in ctx_states], axis=1)
    new_gla_state = jnp.stack([st[4] for st in ctx_states], axis=1)

    z = x_sample
    for l in range(DEPTH):
        cache = (cache_mla_ckv[:, l], cache_mla_krope[:, l], cache_swa_k[:, l], cache_swa_v[:, l], state_gla[:, l])
        z, _ = trunk_layer(z, c, layer_params(l), cache)
    y_sample = z
    return (y_prompt, y_sample, new_mla_ckv, new_mla_krope, new_swa_k, new_swa_v, new_gla_state)
```

```python
import jax
import jax.numpy as jnp
from jax import lax
import numpy as np
from jax.experimental import pallas as pl
from jax.experimental.pallas import tpu as pltpu

D_MODEL = 1024
BATCH = 32
SEQ = 256
DEPTH = 2
DEC_BATCH = 2
DEC_SEQ = 2048
PAST_LEN = 512

GRID_W = 64
N_BRANCH = 4
BRANCH_W = 256
MLA_HEADS = 4
MLA_Q_LORA = 256
MLA_KV_LORA = 128
MLA_NOPE = 64
MLA_ROPE = 32
MLA_V = 64
MLA_SCALE = (MLA_NOPE + MLA_ROPE) ** -0.5
FNET_GROUPS = 4
FNET_CH = BRANCH_W // FNET_GROUPS
GLA_HEADS = 4
GLA_DK = 32
GLA_DV = 64
GLA_GATE_RANK = 16
GLA_TAU = 16.0
GLA_CHUNK = 64
SWA_HEADS = 4
SWA_KV_HEADS = 2
SWA_GROUP = SWA_HEADS // SWA_KV_HEADS
SWA_HEAD_DIM = 64
SWA_WINDOW = 128
SWA_SCALE = SWA_HEAD_DIM ** -0.5
ATTN_BLOCK = 128
PEER_HEADS = 8
PEER_N_KEYS = 128
PEER_N_EXPERTS = PEER_N_KEYS * PEER_N_KEYS
PEER_KEY_DIM = 256
PEER_HALF = PEER_KEY_DIM // 2
PEER_TOPK = 16
PEER_TOKEN_TILE = 512
PEER_EXPERT_BLOCK = 8 * PEER_N_KEYS

ROPE_THETA = 10000.0
NORM_EPS = 1e-6
DEEPNORM_ALPHA = (2.0 * DEPTH) ** 0.25
DEEPNORM_BETA = (8.0 * DEPTH) ** -0.25

IN_SPLITS = (
    ('mla_q', MLA_Q_LORA),
    ('mla_kv', MLA_KV_LORA + MLA_ROPE),
    ('fnet', BRANCH_W),
    ('gla_q', GLA_HEADS * GLA_DK),
    ('gla_k', GLA_HEADS * GLA_DK),
    ('gla_v', GLA_HEADS * GLA_DV),
    ('gla_g', BRANCH_W),
    ('gla_af', GLA_GATE_RANK),
    ('gla_ab', GLA_GATE_RANK),
    ('swa_q', SWA_HEADS * SWA_HEAD_DIM),
    ('swa_k', SWA_KV_HEADS * SWA_HEAD_DIM),
    ('swa_v', SWA_KV_HEADS * SWA_HEAD_DIM),
    ('gates', N_BRANCH * D_MODEL),
)
IN_NAMES = tuple(n for n, _ in IN_SPLITS)
IN_OFFSETS = tuple(int(o) for o in np.cumsum([w for _, w in IN_SPLITS])[:-1])
IN_WIDTH = int(sum(w for _, w in IN_SPLITS))

LANES = 128
NEG_INF = float('-inf')
VMEM_LIMIT = 48 * 1024 * 1024


def layer_norm(x, g=None, b=None):
    xf = x.astype(jnp.float32)
    mu = jnp.mean(xf, -1, keepdims=True)
    var = jnp.mean(jnp.square(xf - mu), -1, keepdims=True)
    y = (xf - mu) * lax.rsqrt(var + NORM_EPS)
    if g is not None:
        y = y * g.astype(jnp.float32) + b.astype(jnp.float32)
    return y.astype(x.dtype)


def rms_norm(x, g):
    xf = x.astype(jnp.float32)
    y = xf * lax.rsqrt(jnp.mean(xf * xf, -1, keepdims=True) + NORM_EPS) * g.astype(jnp.float32)
    return y.astype(x.dtype)


def axial_rope(x):
    n = x.shape[-2]
    half = x.shape[-1] // 2
    t = jnp.arange(n)
    rows = (t // GRID_W).astype(jnp.float32)
    cols = (t % GRID_W).astype(jnp.float32)
    freqs = ROPE_THETA ** (-jnp.arange(0, half, 2, dtype=jnp.float32) / half)

    def rot(xa, pos):
        ang = pos[:, None] * freqs[None, :]
        cos, sin = jnp.cos(ang), jnp.sin(ang)
        x1, x2 = xa[..., :half // 2], xa[..., half // 2:]
        return jnp.concatenate([x1 * cos - x2 * sin, x2 * cos + x1 * sin], -1)

    xf = x.astype(jnp.float32)
    return jnp.concatenate([rot(xf[..., :half], rows), rot(xf[..., half:], cols)], -1).astype(x.dtype)


def softmax_with_sink(s, sink):
    if sink is None:
        return jax.nn.softmax(s, -1)
    s_all = jnp.concatenate([s, jnp.broadcast_to(sink, s[..., :1].shape)], -1)
    return jax.nn.softmax(s_all, -1)[..., :-1]


def dense_attention(q, k, v, scale, sink=None):
    b, hk, g, sq, dk = q.shape
    nb = sq // ATTN_BLOCK
    qb = jnp.moveaxis(q.reshape(b, hk, g, nb, ATTN_BLOCK, dk), 3, 0)
    sink_b = None if sink is None else sink.astype(jnp.float32)[None, :, :, None, None]

    def one_block(qi):
        s = jnp.einsum('bhgqd,bhkd->bhgqk', qi, k, preferred_element_type=jnp.float32) * scale
        p = softmax_with_sink(s, sink_b)
        return jnp.einsum('bhgqk,bhkd->bhgqd', p.astype(v.dtype), v)

    o = lax.map(one_block, qb)
    return jnp.moveaxis(o, 0, 3).reshape(b, hk, g, sq, v.shape[-1])


def windowed_attention(q, k, v, k_ctx, v_ctx, sink):
    b, hk, g, s, dh = q.shape
    blk = SWA_WINDOW
    nb = s // blk
    qb = q.reshape(b, hk, g, nb, blk, dh)

    def band(x):
        xp = jnp.pad(x, ((0, 0), (0, 0), (blk, blk), (0, 0))).reshape(b, hk, nb + 2, blk, x.shape[-1])
        return jnp.concatenate([xp[:, :, :-2], xp[:, :, 1:-1], xp[:, :, 2:]], axis=3)

    kb, vb = band(k), band(v)
    qpos = jnp.arange(nb)[:, None] * blk + jnp.arange(blk)[None, :]
    kpos = (jnp.arange(nb)[:, None] - 1) * blk + jnp.arange(3 * blk)[None, :]
    rel = kpos[:, None, :] - qpos[:, :, None]
    valid = (jnp.abs(rel) <= SWA_WINDOW) & (kpos[:, None, :] >= 0) & (kpos[:, None, :] < s)
    s_band = jnp.einsum('bhgnqd,bhnkd->bhgnqk', qb, kb, preferred_element_type=jnp.float32) * SWA_SCALE
    s_band = jnp.where(valid, s_band, -jnp.inf)
    s_ctx = jnp.einsum('bhgnqd,bhld->bhgnql', qb, k_ctx, preferred_element_type=jnp.float32) * SWA_SCALE
    p = softmax_with_sink(jnp.concatenate([s_band, s_ctx], -1),
                          sink.astype(jnp.float32)[None, :, :, None, None, None])
    p_band, p_ctx = p[..., :3 * blk], p[..., 3 * blk:]
    o = (jnp.einsum('bhgnqk,bhnkd->bhgnqd', p_band.astype(v.dtype), vb)
         + jnp.einsum('bhgnql,bhld->bhgnqd', p_ctx.astype(v.dtype), v_ctx))
    return o.reshape(b, hk, g, s, dh)


def gla_scan(q, k, v, log_a, s0):
    b, s, h, dk = q.shape
    dv = v.shape[-1]
    n = s // GLA_CHUNK
    f32 = jnp.float32
    qc, kc, vc, ac = [t.astype(f32).reshape(b, n, GLA_CHUNK, h, t.shape[-1]) for t in (q, k, v, log_a)]
    cum = jnp.cumsum(ac, axis=2)
    causal = jnp.tril(jnp.ones((GLA_CHUNK, GLA_CHUNK), dtype=bool))
    decay = jnp.exp(jnp.minimum(cum[:, :, :, None] - cum[:, :, None, :], 0.0))
    att = jnp.sum(qc[:, :, :, None] * kc[:, :, None, :] * decay, -1)
    att = jnp.where(causal[:, :, None], att, 0.0)
    o_intra = jnp.einsum('bntsh,bnshv->bnthv', att, vc)
    last = cum[:, :, -1:]
    u = jnp.einsum('bnshk,bnshv->bnhkv', kc * jnp.exp(last - cum), vc)
    g = jnp.exp(last[:, :, 0])

    def step(state, xs):
        g_n, u_n = xs
        return g_n[..., None] * state + u_n, state

    s_final, s_in = lax.scan(step, s0.astype(f32), (jnp.moveaxis(g, 1, 0), jnp.moveaxis(u, 1, 0)))
    s_in = jnp.moveaxis(s_in, 0, 1)
    o_inter = jnp.einsum('bnthk,bnhkv->bnthv', qc * jnp.exp(cum), s_in)
    return (o_intra + o_inter).reshape(b, s, h, dv).astype(v.dtype), s_final.astype(v.dtype)


def gla_bidirectional(parts, lp, s0_fwd, s0_bwd):
    b, s, _ = parts['gla_q'].shape
    q = parts['gla_q'].reshape(b, s, GLA_HEADS, GLA_DK) * (GLA_DK ** -0.5)
    k = parts['gla_k'].reshape(b, s, GLA_HEADS, GLA_DK)
    v = parts['gla_v'].reshape(b, s, GLA_HEADS, GLA_DV)

    def log_decay(a_low, w2, b2):
        z = (a_low @ w2 + b2).astype(jnp.float32)
        return (jax.nn.log_sigmoid(z) / GLA_TAU).reshape(b, s, GLA_HEADS, GLA_DK)

    la_f = log_decay(parts['gla_af'], lp['w_gla_a_fwd'], lp['b_gla_a_fwd'])
    la_b = log_decay(parts['gla_ab'], lp['w_gla_a_bwd'], lp['b_gla_a_bwd'])
    o_f, s_f = gla_scan(q, k, v, la_f, s0_fwd)
    flip = lambda t: jnp.flip(t, axis=1)
    o_b, s_b = gla_scan(flip(q), flip(k), flip(v), flip(la_b), s0_bwd)
    o = rms_norm(o_f + flip(o_b), lp['gla_norm']).reshape(b, s, BRANCH_W)
    return o * jax.nn.silu(parts['gla_g']), s_f, s_b


def fourier_mix(f):
    b, s, _ = f.shape
    fg = f.astype(jnp.float32).reshape(b, s, FNET_GROUPS, FNET_CH)
    return jnp.fft.fft2(fg, axes=(1, 3), norm='ortho').real.reshape(b, s, BRANCH_W).astype(f.dtype)


def mla_project(parts, lp, positional):
    b, s, _ = parts['mla_q'].shape
    q = (rms_norm(parts['mla_q'], lp['mla_q_norm']) @ lp['w_uq'])
    q = q.reshape(b, s, MLA_HEADS, MLA_NOPE + MLA_ROPE).transpose(0, 2, 1, 3)
    q_nope, q_rope = q[..., :MLA_NOPE], q[..., MLA_NOPE:]
    ckv = rms_norm(parts['mla_kv'][..., :MLA_KV_LORA], lp['mla_kv_norm'])
    k_rope = parts['mla_kv'][..., MLA_KV_LORA:]
    if positional:
        q_rope = axial_rope(q_rope)
        k_rope = axial_rope(k_rope)
    return jnp.concatenate([q_nope, q_rope], -1), ckv, k_rope


def mla_expand(ckv, k_rope, w_ukv):
    b, s, _ = ckv.shape
    kv = (ckv @ w_ukv).reshape(b, s, MLA_HEADS, MLA_NOPE + MLA_V).transpose(0, 2, 1, 3)
    k_nope, v = kv[..., :MLA_NOPE], kv[..., MLA_NOPE:]
    k = jnp.concatenate([k_nope, jnp.broadcast_to(k_rope[:, None], (b, MLA_HEADS, s, MLA_ROPE))], -1)
    return k, v


def mla_attend(q, k, v):
    o = dense_attention(q[:, :, None], k, v, MLA_SCALE)[:, :, 0]
    b, h, s, dv = o.shape
    return o.transpose(0, 2, 1, 3).reshape(b, s, h * dv)


def swa_project(parts, positional):
    b, s, _ = parts['swa_q'].shape
    q = parts['swa_q'].reshape(b, s, SWA_KV_HEADS, SWA_GROUP, SWA_HEAD_DIM).transpose(0, 2, 3, 1, 4)
    k = parts['swa_k'].reshape(b, s, SWA_KV_HEADS, SWA_HEAD_DIM).transpose(0, 2, 1, 3)
    v = parts['swa_v'].reshape(b, s, SWA_KV_HEADS, SWA_HEAD_DIM).transpose(0, 2, 1, 3)
    if positional:
        q = axial_rope(q)
        k = axial_rope(k)
    return q, k, v


def swa_merge_heads(o):
    b, hk, g, s, dh = o.shape
    return o.transpose(0, 3, 1, 2, 4).reshape(b, s, hk * g * dh)


def merge_branches(branches, gates, lp):
    b, s = branches.shape[:2]
    proj = jnp.einsum('bsnc,ncd->bsnd', branches, lp['w_branch'])
    g = jax.nn.sigmoid(gates.reshape(b, s, N_BRANCH, D_MODEL))
    return jnp.sum(g * proj, axis=2) @ lp['w_out']


def context_mixer(u, lp):
    b, s, _ = u.shape
    parts = dict(zip(IN_NAMES, jnp.split(u @ lp['w_in'], IN_OFFSETS, axis=-1)))
    q_a, ckv, k_rope = mla_project(parts, lp, positional=False)
    k_a, v_a = mla_expand(ckv, k_rope, lp['w_ukv'])
    o_a = mla_attend(q_a, k_a, v_a)
    o_b = fourier_mix(parts['fnet'])
    zero = jnp.zeros((b, GLA_HEADS, GLA_DK, GLA_DV), jnp.float32)
    o_c, s_f, s_b = gla_bidirectional(parts, lp, zero, zero)
    q_d, k_d, v_d = swa_project(parts, positional=False)
    o_d = swa_merge_heads(dense_attention(q_d, k_d, v_d, SWA_SCALE,
                                          lp['swa_sink'].reshape(SWA_KV_HEADS, SWA_GROUP)))
    mix = merge_branches(jnp.stack([o_a, o_b, o_c, o_d], axis=2), parts['gates'], lp)
    return mix, (ckv, k_rope, k_d, v_d, jnp.stack([s_f, s_b], axis=1))


def latent_mixer(u, lp, cache):
    ckv_c, krope_c, k_ctx, v_ctx, gla_state = cache
    parts = dict(zip(IN_NAMES, jnp.split(u @ lp['w_in'], IN_OFFSETS, axis=-1)))
    q_a, ckv, k_rope = mla_project(parts, lp, positional=True)
    k_lat, v_lat = mla_expand(ckv, k_rope, lp['w_ukv'])
    k_c, v_c = mla_expand(ckv_c, krope_c, lp['w_ukv'])
    o_a = mla_attend(q_a, jnp.concatenate([k_c, k_lat], axis=2), jnp.concatenate([v_c, v_lat], axis=2))
    o_b = fourier_mix(parts['fnet'])
    o_c, _, _ = gla_bidirectional(parts, lp, gla_state[:, 0], gla_state[:, 1])
    q_d, k_d, v_d = swa_project(parts, positional=True)
    o_d = swa_merge_heads(windowed_attention(q_d, k_d, v_d, k_ctx, v_ctx,
                                             lp['swa_sink'].reshape(SWA_KV_HEADS, SWA_GROUP)))
    return merge_branches(jnp.stack([o_a, o_b, o_c, o_d], axis=2), parts['gates'], lp)


def _ln(x):
    mu = jnp.mean(x, -1, keepdims=True)
    xc = x - mu
    var = jnp.mean(xc * xc, -1, keepdims=True)
    return xc * lax.rsqrt(var + NORM_EPS)


def _extract_top(work, n):
    iota = lax.broadcasted_iota(jnp.int32, work.shape, 0)
    rows = work.shape[0]
    vals = []
    for _ in range(n):
        mx = jnp.max(work, axis=0, keepdims=True)
        vals.append(mx)
        first = jnp.min(jnp.where(work == mx, iota, rows), axis=0, keepdims=True)
        work = jnp.where(iota == first, NEG_INF, work)
    return jnp.concatenate(vals, axis=0)


def _peer_route_kernel(x_ref, sh_ref, sc_ref, wq_ref, keys_ref, ut_ref, s1_ref, s2_ref, st_ref, sc_scr):
    tt = x_ref.shape[0]
    u = _ln(x_ref[...]) * (1.0 + sc_ref[...]) + sh_ref[...]
    ut = u.T.astype(jnp.bfloat16)
    ut_ref[...] = ut
    qt = jnp.dot(wq_ref[...], ut, preferred_element_type=jnp.float32)
    for hp in range(2 * PEER_HEADS):
        q_hp = qt[hp * PEER_HALF:(hp + 1) * PEER_HALF, :].astype(jnp.bfloat16)
        sc_scr[hp] = jnp.dot(keys_ref[hp], q_hp, preferred_element_type=jnp.float32)

    def per_head(h, carry):
        for tc in range(tt // LANES):
            ls = slice(tc * LANES, (tc + 1) * LANES)
            s1 = sc_scr[2 * h, :, ls]
            s2 = sc_scr[2 * h + 1, :, ls]
            m = jnp.max(s1, axis=0, keepdims=True) + jnp.max(s2, axis=0, keepdims=True)
            s1 = s1 - m
            a = _extract_top(s1, PEER_TOPK)
            b = _extract_top(s2, PEER_TOPK)
            cands = [a[0:1] + b, a[1:2] + b[0:8]]
            cands += [a[i:i + 1] + b[0:8] for i in range(2, 8)]
            cands += [a[8:16] + b[0:1]]
            cand = jnp.concatenate(cands, axis=0)
            tau = _extract_top(cand, PEER_TOPK)[PEER_TOPK - 1:PEER_TOPK]
            z = jnp.sum(jnp.where(cand >= tau, jnp.exp(cand), 0.0), axis=0, keepdims=True)
            s1_ref[h, :, :, ls] = s1.reshape(PEER_N_KEYS // 8, 8, LANES)
            s2_ref[h, :, ls] = s2
            st_ref[h, :, ls] = jnp.concatenate([tau, 1.0 / z, jnp.zeros((6, LANES), jnp.float32)], axis=0)
        return carry

    lax.fori_loop(0, PEER_HEADS, per_head, 0)


def _peer_dense_kernel(ut_ref, s1_ref, s2_ref, st_ref, u_ref, vt_ref, x_ref, g_ref, lng_ref, lnb_ref,
                       y_ref, acc_ref, act_ref, gt_ref):
    j = pl.program_id(1)
    eb, tt = act_ref.shape
    n_i1 = eb // PEER_N_KEYS
    assert n_i1 == 8

    @pl.when(j == 0)
    def _():
        acc_ref[...] = jnp.zeros_like(acc_ref)

    act_ref[...] = jnp.dot(u_ref[...], ut_ref[...], preferred_element_type=jnp.float32)

    def per_chunk(tc, carry):
        ls = pl.ds(pl.multiple_of(tc * LANES, LANES), LANES)
        for il in range(n_i1):
            w = jnp.zeros((PEER_N_KEYS, LANES), jnp.float32)
            for h in range(PEER_HEADS):
                s1_rows = s1_ref[h, j, :, ls]
                cand = s2_ref[h, :, ls] + s1_rows[il:il + 1]
                val = jnp.exp(cand) * st_ref[h, 1:2, ls]
                w = w + jnp.where(cand >= st_ref[h, 0:1, ls], val, 0.0)
            rs = slice(il * PEER_N_KEYS, (il + 1) * PEER_N_KEYS)
            a = act_ref[rs, ls]
            gelu = 0.5 * a * (1.0 + lax.erf(a * np.float32(np.sqrt(0.5))))
            gt_ref[rs, ls] = (w * gelu).astype(jnp.bfloat16)
        return carry

    lax.fori_loop(0, tt // LANES, per_chunk, 0)
    acc_ref[...] += jnp.dot(vt_ref[...], gt_ref[...], preferred_element_type=jnp.float32)

    @pl.when(j == pl.num_programs(1) - 1)
    def _():
        r = DEEPNORM_ALPHA * x_ref[...] + g_ref[...] * acc_ref[...].T
        y_ref[...] = _ln(r) * lng_ref[...] + lnb_ref[...]


def peer_layer(x, mod, tile_cond, wq_t, keys, u_tab, vt_tab, ln_g, ln_b, *, tt, eb):
    t, d = x.shape
    nt = t // tt
    nh = PEER_HEADS
    row = lambda k: pl.BlockSpec((None, None, 1, d), lambda i, *_: (tile_cond(i), k, 0, 0))
    ut, s1, s2, st = pl.pallas_call(
        _peer_route_kernel,
        grid=(nt,),
        in_specs=[pl.BlockSpec((tt, d), lambda i: (i, 0)), row(0), row(1),
                  pl.BlockSpec(wq_t.shape, lambda i: (0, 0)),
                  pl.BlockSpec(keys.shape, lambda i: (0, 0, 0))],
        out_specs=[pl.BlockSpec((d, tt), lambda i: (0, i)),
                   pl.BlockSpec((nh, PEER_N_KEYS // 8, 8, tt), lambda i: (0, 0, 0, i)),
                   pl.BlockSpec((nh, PEER_N_KEYS, tt), lambda i: (0, 0, i)),
                   pl.BlockSpec((nh, 8, tt), lambda i: (0, 0, i))],
        out_shape=[jax.ShapeDtypeStruct((d, t), jnp.bfloat16),
                   jax.ShapeDtypeStruct((nh, PEER_N_KEYS // 8, 8, t), jnp.float32),
                   jax.ShapeDtypeStruct((nh, PEER_N_KEYS, t), jnp.float32),
                   jax.ShapeDtypeStruct((nh, 8, t), jnp.float32)],
        scratch_shapes=[pltpu.VMEM((2 * nh, PEER_N_KEYS, tt), jnp.float32)],
        compiler_params=pltpu.CompilerParams(dimension_semantics=("arbitrary",), vmem_limit_bytes=VMEM_LIMIT),
        name="peer_route",
    )(x, mod, mod, wq_t, keys)

    ne = PEER_N_EXPERTS // eb
    y = pl.pallas_call(
        _peer_dense_kernel,
        grid=(nt, ne),
        in_specs=[pl.BlockSpec((d, tt), lambda i, j: (0, i)),
                  pl.BlockSpec((nh, PEER_N_KEYS // 8, 8, tt), lambda i, j: (0, 0, 0, i)),
                  pl.BlockSpec((nh, PEER_N_KEYS, tt), lambda i, j: (0, 0, i)),
                  pl.BlockSpec((nh, 8, tt), lambda i, j: (0, 0, i)),
                  pl.BlockSpec((eb, d), lambda i, j: (j, 0)),
                  pl.BlockSpec((d, eb), lambda i, j: (0, j)),
                  pl.BlockSpec((tt, d), lambda i, j: (i, 0)),
                  row(2),
                  pl.BlockSpec((1, d), lambda i, j: (0, 0)),
                  pl.BlockSpec((1, d), lambda i, j: (0, 0))],
        out_specs=pl.BlockSpec((tt, d), lambda i, j: (i, 0)),
        out_shape=jax.ShapeDtypeStruct((t, d), jnp.float32),
        scratch_shapes=[pltpu.VMEM((d, tt), jnp.float32), pltpu.VMEM((eb, tt), jnp.float32),
                        pltpu.VMEM((eb, tt), jnp.bfloat16)],
        compiler_params=pltpu.CompilerParams(dimension_semantics=("arbitrary", "arbitrary"),
                                             vmem_limit_bytes=VMEM_LIMIT),
        name="peer_dense",
    )(ut, s1, s2, st, u_tab, vt_tab, x, mod, ln_g, ln_b)
    return y


def _mixer_half(x, mod, lp, cache):
    sh1, sc1, g1 = mod[:, None, 0], mod[:, None, 1], mod[:, None, 2]
    u = layer_norm(x) * (1 + sc1) + sh1
    if cache is None:
        mix, new_cache = context_mixer(u, lp)
    else:
        mix, new_cache = latent_mixer(u, lp, cache), None
    return layer_norm(DEEPNORM_ALPHA * x + g1 * mix, lp['ln1_g'], lp['ln1_b']), new_cache


def kernel(x_prompt, x_sample, c, cache_mla_ckv, cache_mla_krope, cache_swa_k, cache_swa_v, state_gla,
           c_ctx, w_ada, b_ada, w_in, mla_q_norm, w_uq, mla_kv_norm, w_ukv,
           w_gla_a_fwd, b_gla_a_fwd, w_gla_a_bwd, b_gla_a_bwd, gla_norm, swa_sink,
           w_branch, w_out, ln1_g, ln1_b, ln2_g, ln2_b, w_peer_q, peer_keys, peer_u, peer_v):
    def layer_params(l):
        return {
            'w_in': w_in[l],
            'mla_q_norm': mla_q_norm[l], 'w_uq': w_uq[l], 'mla_kv_norm': mla_kv_norm[l], 'w_ukv': w_ukv[l],
            'w_gla_a_fwd': w_gla_a_fwd[l], 'b_gla_a_fwd': b_gla_a_fwd[l],
            'w_gla_a_bwd': w_gla_a_bwd[l], 'b_gla_a_bwd': b_gla_a_bwd[l], 'gla_norm': gla_norm[l],
            'swa_sink': swa_sink[l], 'w_branch': w_branch[l], 'w_out': w_out[l],
            'ln1_g': ln1_g[l], 'ln1_b': ln1_b[l],
        }

    n_ctx = BATCH * SEQ
    ctx_tiles = n_ctx // PEER_TOKEN_TILE
    lat_tiles = DEC_SEQ // PEER_TOKEN_TILE
    tile_cond = lambda i: jnp.where(i < ctx_tiles, 0, 1 + (i - ctx_tiles) // lat_tiles)
    conds = jnp.concatenate([c_ctx[None, :], c], axis=0)

    h, z = x_prompt, x_sample
    ctx_states = []
    for l in range(DEPTH):
        lp = layer_params(l)
        mod = (jax.nn.silu(conds) @ w_ada[l] + b_ada[l]).reshape(1 + DEC_BATCH, 6, D_MODEL)
        h1, st = _mixer_half(h, mod[:1], lp, None)
        ctx_states.append(st)
        cache = (cache_mla_ckv[:, l], cache_mla_krope[:, l], cache_swa_k[:, l], cache_swa_v[:, l], state_gla[:, l])
        z1, _ = _mixer_half(z, mod[1:], lp, cache)
        x1 = jnp.concatenate([h1.reshape(n_ctx, D_MODEL), z1.reshape(DEC_BATCH * DEC_SEQ, D_MODEL)], axis=0)
        y = peer_layer(
            x1, mod[:, 3:6, None, :], tile_cond,
            w_peer_q[l].T.astype(jnp.bfloat16),
            peer_keys[l].reshape(2 * PEER_HEADS, PEER_N_KEYS, PEER_HALF).astype(jnp.bfloat16),
            peer_u[l].astype(jnp.bfloat16), peer_v[l].T.astype(jnp.bfloat16),
            ln2_g[l][None, :], ln2_b[l][None, :], tt=PEER_TOKEN_TILE, eb=PEER_EXPERT_BLOCK)
        h = y[:n_ctx].reshape(BATCH, SEQ, D_MODEL)
        z = y[n_ctx:].reshape(DEC_BATCH, DEC_SEQ, D_MODEL)

    new_mla_ckv = jnp.stack([st[0] for st in ctx_states], axis=1)
    new_mla_krope = jnp.stack([st[1] for st in ctx_states], axis=1)
    new_swa_k = jnp.stack([st[2] for st in ctx_states], axis=1)
    new_swa_v = jnp.stack([st[3] for st in ctx_states], axis=1)
    new_gla_state = jnp.stack([st[4] for st in ctx_states], axis=1)
    return (h, z, new_mla_ckv, new_mla_krope, new_swa_k, new_swa_v, new_gla_state)
```

```python
import jax
import jax.numpy as jnp
from jax import lax
import numpy as np
from jax.experimental import pallas as pl
from jax.experimental.pallas import tpu as pltpu

D_MODEL = 1024
BATCH = 32
SEQ = 256
DEPTH = 2
DEC_BATCH = 2
DEC_SEQ = 2048
PAST_LEN = 512

GRID_W = 64
N_BRANCH = 4
BRANCH_W = 256
MLA_HEADS = 4
MLA_Q_LORA = 256
MLA_KV_LORA = 128
MLA_NOPE = 64
MLA_ROPE = 32
MLA_V = 64
MLA_SCALE = (MLA_NOPE + MLA_ROPE) ** -0.5
FNET_GROUPS = 4
FNET_CH = BRANCH_W // FNET_GROUPS
GLA_HEADS = 4
GLA_DK = 32
GLA_DV = 64
GLA_GATE_RANK = 16
GLA_TAU = 16.0
GLA_CHUNK = 64
SWA_HEADS = 4
SWA_KV_HEADS = 2
SWA_GROUP = SWA_HEADS // SWA_KV_HEADS
SWA_HEAD_DIM = 64
SWA_WINDOW = 128
SWA_SCALE = SWA_HEAD_DIM ** -0.5
ATTN_BLOCK = 128
PEER_HEADS = 8
PEER_N_KEYS = 128
PEER_N_EXPERTS = PEER_N_KEYS * PEER_N_KEYS
PEER_KEY_DIM = 256
PEER_HALF = PEER_KEY_DIM // 2
PEER_TOPK = 16
PEER_TOKEN_TILE = 512
PEER_EXPERT_BLOCK = 8 * PEER_N_KEYS

ROPE_THETA = 10000.0
NORM_EPS = 1e-6
DEEPNORM_ALPHA = (2.0 * DEPTH) ** 0.25
DEEPNORM_BETA = (8.0 * DEPTH) ** -0.25

IN_SPLITS = (
    ('mla_q', MLA_Q_LORA),
    ('mla_kv', MLA_KV_LORA + MLA_ROPE),
    ('fnet', BRANCH_W),
    ('gla_q', GLA_HEADS * GLA_DK),
    ('gla_k', GLA_HEADS * GLA_DK),
    ('gla_v', GLA_HEADS * GLA_DV),
    ('gla_g', BRANCH_W),
    ('gla_af', GLA_GATE_RANK),
    ('gla_ab', GLA_GATE_RANK),
    ('swa_q', SWA_HEADS * SWA_HEAD_DIM),
    ('swa_k', SWA_KV_HEADS * SWA_HEAD_DIM),
    ('swa_v', SWA_KV_HEADS * SWA_HEAD_DIM),
    ('gates', N_BRANCH * D_MODEL),
)
IN_NAMES = tuple(n for n, _ in IN_SPLITS)
IN_OFFSETS = tuple(int(o) for o in np.cumsum([w for _, w in IN_SPLITS])[:-1])
IN_WIDTH = int(sum(w for _, w in IN_SPLITS))
GATES_OFFSET = IN_OFFSETS[-1]
PARTS_WIDTH = -(-GATES_OFFSET // 128) * 128

LANES = 128
NEG_INF = float('-inf')
VMEM_LIMIT = 48 * 1024 * 1024


def layer_norm(x, g=None, b=None):
    xf = x.astype(jnp.float32)
    mu = jnp.mean(xf, -1, keepdims=True)
    var = jnp.mean(jnp.square(xf - mu), -1, keepdims=True)
    y = (xf - mu) * lax.rsqrt(var + NORM_EPS)
    if g is not None:
        y = y * g.astype(jnp.float32) + b.astype(jnp.float32)
    return y.astype(x.dtype)


def rms_norm(x, g):
    xf = x.astype(jnp.float32)
    y = xf * lax.rsqrt(jnp.mean(xf * xf, -1, keepdims=True) + NORM_EPS) * g.astype(jnp.float32)
    return y.astype(x.dtype)


def axial_rope(x):
    n = x.shape[-2]
    half = x.shape[-1] // 2
    t = jnp.arange(n)
    rows = (t // GRID_W).astype(jnp.float32)
    cols = (t % GRID_W).astype(jnp.float32)
    freqs = ROPE_THETA ** (-jnp.arange(0, half, 2, dtype=jnp.float32) / half)

    def rot(xa, pos):
        ang = pos[:, None] * freqs[None, :]
        cos, sin = jnp.cos(ang), jnp.sin(ang)
        x1, x2 = xa[..., :half // 2], xa[..., half // 2:]
        return jnp.concatenate([x1 * cos - x2 * sin, x2 * cos + x1 * sin], -1)

    xf = x.astype(jnp.float32)
    return jnp.concatenate([rot(xf[..., :half], rows), rot(xf[..., half:], cols)], -1).astype(x.dtype)


def softmax_with_sink(s, sink):
    if sink is None:
        return jax.nn.softmax(s, -1)
    s_all = jnp.concatenate([s, jnp.broadcast_to(sink, s[..., :1].shape)], -1)
    return jax.nn.softmax(s_all, -1)[..., :-1]


def dense_attention(q, k, v, scale, sink=None):
    b, hk, g, sq, dk = q.shape
    nb = sq // ATTN_BLOCK
    qb = jnp.moveaxis(q.reshape(b, hk, g, nb, ATTN_BLOCK, dk), 3, 0)
    sink_b = None if sink is None else sink.astype(jnp.float32)[None, :, :, None, None]

    def one_block(qi):
        s = jnp.einsum('bhgqd,bhkd->bhgqk', qi, k, preferred_element_type=jnp.float32) * scale
        p = softmax_with_sink(s, sink_b)
        return jnp.einsum('bhgqk,bhkd->bhgqd', p.astype(v.dtype), v)

    o = lax.map(one_block, qb)
    return jnp.moveaxis(o, 0, 3).reshape(b, hk, g, sq, v.shape[-1])


def windowed_attention(q, k, v, k_ctx, v_ctx, sink):
    b, hk, g, s, dh = q.shape
    blk = SWA_WINDOW
    nb = s // blk
    qb = q.reshape(b, hk, g, nb, blk, dh)

    def band(x):
        xp = jnp.pad(x, ((0, 0), (0, 0), (blk, blk), (0, 0))).reshape(b, hk, nb + 2, blk, x.shape[-1])
        return jnp.concatenate([xp[:, :, :-2], xp[:, :, 1:-1], xp[:, :, 2:]], axis=3)

    kb, vb = band(k), band(v)
    qpos = jnp.arange(nb)[:, None] * blk + jnp.arange(blk)[None, :]
    kpos = (jnp.arange(nb)[:, None] - 1) * blk + jnp.arange(3 * blk)[None, :]
    rel = kpos[:, None, :] - qpos[:, :, None]
    valid = (jnp.abs(rel) <= SWA_WINDOW) & (kpos[:, None, :] >= 0) & (kpos[:, None, :] < s)
    s_band = jnp.einsum('bhgnqd,bhnkd->bhgnqk', qb, kb, preferred_element_type=jnp.float32) * SWA_SCALE
    s_band = jnp.where(valid, s_band, -jnp.inf)
    s_ctx = jnp.einsum('bhgnqd,bhld->bhgnql', qb, k_ctx, preferred_element_type=jnp.float32) * SWA_SCALE
    p = softmax_with_sink(jnp.concatenate([s_band, s_ctx], -1),
                          sink.astype(jnp.float32)[None, :, :, None, None, None])
    p_band, p_ctx = p[..., :3 * blk], p[..., 3 * blk:]
    o = (jnp.einsum('bhgnqk,bhnkd->bhgnqd', p_band.astype(v.dtype), vb)
         + jnp.einsum('bhgnql,bhld->bhgnqd', p_ctx.astype(v.dtype), v_ctx))
    return o.reshape(b, hk, g, s, dh)


def gla_scan(q, k, v, log_a, s0):
    b, s, h, dk = q.shape
    dv = v.shape[-1]
    n = s // GLA_CHUNK
    f32 = jnp.float32
    qc, kc, vc, ac = [t.astype(f32).reshape(b, n, GLA_CHUNK, h, t.shape[-1]) for t in (q, k, v, log_a)]
    cum = jnp.cumsum(ac, axis=2)
    causal = jnp.tril(jnp.ones((GLA_CHUNK, GLA_CHUNK), dtype=bool))
    decay = jnp.exp(jnp.minimum(cum[:, :, :, None] - cum[:, :, None, :], 0.0))
    att = jnp.sum(qc[:, :, :, None] * kc[:, :, None, :] * decay, -1)
    att = jnp.where(causal[:, :, None], att, 0.0)
    o_intra = jnp.einsum('bntsh,bnshv->bnthv', att, vc)
    last = cum[:, :, -1:]
    u = jnp.einsum('bnshk,bnshv->bnhkv', kc * jnp.exp(last - cum), vc)
    g = jnp.exp(last[:, :, 0])

    def step(state, xs):
        g_n, u_n = xs
        return g_n[..., None] * state + u_n, state

    s_final, s_in = lax.scan(step, s0.astype(f32), (jnp.moveaxis(g, 1, 0), jnp.moveaxis(u, 1, 0)))
    s_in = jnp.moveaxis(s_in, 0, 1)
    o_inter = jnp.einsum('bnthk,bnhkv->bnthv', qc * jnp.exp(cum), s_in)
    return (o_intra + o_inter).reshape(b, s, h, dv).astype(v.dtype), s_final.astype(v.dtype)


def gla_bidirectional(parts, lp, s0_fwd, s0_bwd):
    b, s, _ = parts['gla_q'].shape
    q = parts['gla_q'].reshape(b, s, GLA_HEADS, GLA_DK) * (GLA_DK ** -0.5)
    k = parts['gla_k'].reshape(b, s, GLA_HEADS, GLA_DK)
    v = parts['gla_v'].reshape(b, s, GLA_HEADS, GLA_DV)

    def log_decay(a_low, w2, b2):
        z = (a_low @ w2 + b2).astype(jnp.float32)
        return (jax.nn.log_sigmoid(z) / GLA_TAU).reshape(b, s, GLA_HEADS, GLA_DK)

    la_f = log_decay(parts['gla_af'], lp['w_gla_a_fwd'], lp['b_gla_a_fwd'])
    la_b = log_decay(parts['gla_ab'], lp['w_gla_a_bwd'], lp['b_gla_a_bwd'])
    o_f, s_f = gla_scan(q, k, v, la_f, s0_fwd)
    flip = lambda t: jnp.flip(t, axis=1)
    o_b, s_b = gla_scan(flip(q), flip(k), flip(v), flip(la_b), s0_bwd)
    o = rms_norm(o_f + flip(o_b), lp['gla_norm']).reshape(b, s, BRANCH_W)
    return o * jax.nn.silu(parts['gla_g']), s_f, s_b


def fourier_mix(f):
    b, s, _ = f.shape
    fg = f.astype(jnp.float32).reshape(b, s, FNET_GROUPS, FNET_CH)
    return jnp.fft.fft2(fg, axes=(1, 3), norm='ortho').real.reshape(b, s, BRANCH_W).astype(f.dtype)


def mla_project(parts, lp, positional):
    b, s, _ = parts['mla_q'].shape
    q = (rms_norm(parts['mla_q'], lp['mla_q_norm']) @ lp['w_uq'])
    q = q.reshape(b, s, MLA_HEADS, MLA_NOPE + MLA_ROPE).transpose(0, 2, 1, 3)
    q_nope, q_rope = q[..., :MLA_NOPE], q[..., MLA_NOPE:]
    ckv = rms_norm(parts['mla_kv'][..., :MLA_KV_LORA], lp['mla_kv_norm'])
    k_rope = parts['mla_kv'][..., MLA_KV_LORA:]
    if positional:
        q_rope = axial_rope(q_rope)
        k_rope = axial_rope(k_rope)
    return jnp.concatenate([q_nope, q_rope], -1), ckv, k_rope


def mla_expand(ckv, k_rope, w_ukv):
    b, s, _ = ckv.shape
    kv = (ckv @ w_ukv).reshape(b, s, MLA_HEADS, MLA_NOPE + MLA_V).transpose(0, 2, 1, 3)
    k_nope, v = kv[..., :MLA_NOPE], kv[..., MLA_NOPE:]
    k = jnp.concatenate([k_nope, jnp.broadcast_to(k_rope[:, None], (b, MLA_HEADS, s, MLA_ROPE))], -1)
    return k, v


def mla_attend(q, k, v):
    o = dense_attention(q[:, :, None], k, v, MLA_SCALE)[:, :, 0]
    b, h, s, dv = o.shape
    return o.transpose(0, 2, 1, 3).reshape(b, s, h * dv)


def swa_project(parts, positional):
    b, s, _ = parts['swa_q'].shape
    q = parts['swa_q'].reshape(b, s, SWA_KV_HEADS, SWA_GROUP, SWA_HEAD_DIM).transpose(0, 2, 3, 1, 4)
    k = parts['swa_k'].reshape(b, s, SWA_KV_HEADS, SWA_HEAD_DIM).transpose(0, 2, 1, 3)
    v = parts['swa_v'].reshape(b, s, SWA_KV_HEADS, SWA_HEAD_DIM).transpose(0, 2, 1, 3)
    if positional:
        q = axial_rope(q)
        k = axial_rope(k)
    return q, k, v


def swa_merge_heads(o):
    b, hk, g, s, dh = o.shape
    return o.transpose(0, 3, 1, 2, 4).reshape(b, s, hk * g * dh)


def context_branches(parts, lp):
    b = parts['mla_q'].shape[0]
    q_a, ckv, k_rope = mla_project(parts, lp, positional=False)
    k_a, v_a = mla_expand(ckv, k_rope, lp['w_ukv'])
    o_a = mla_attend(q_a, k_a, v_a)
    o_b = fourier_mix(parts['fnet'])
    zero = jnp.zeros((b, GLA_HEADS, GLA_DK, GLA_DV), jnp.float32)
    o_c, s_f, s_b = gla_bidirectional(parts, lp, zero, zero)
    q_d, k_d, v_d = swa_project(parts, positional=False)
    o_d = swa_merge_heads(dense_attention(q_d, k_d, v_d, SWA_SCALE,
                                          lp['swa_sink'].reshape(SWA_KV_HEADS, SWA_GROUP)))
    return jnp.concatenate([o_a, o_b, o_c, o_d], axis=-1), (ckv, k_rope, k_d, v_d, jnp.stack([s_f, s_b], axis=1))


def latent_branches(parts, lp, cache):
    ckv_c, krope_c, k_ctx, v_ctx, gla_state = cache
    q_a, ckv, k_rope = mla_project(parts, lp, positional=True)
    k_lat, v_lat = mla_expand(ckv, k_rope, lp['w_ukv'])
    k_c, v_c = mla_expand(ckv_c, krope_c, lp['w_ukv'])
    o_a = mla_attend(q_a, jnp.concatenate([k_c, k_lat], axis=2), jnp.concatenate([v_c, v_lat], axis=2))
    o_b = fourier_mix(parts['fnet'])
    o_c, _, _ = gla_bidirectional(parts, lp, gla_state[:, 0], gla_state[:, 1])
    q_d, k_d, v_d = swa_project(parts, positional=True)
    o_d = swa_merge_heads(windowed_attention(q_d, k_d, v_d, k_ctx, v_ctx,
                                             lp['swa_sink'].reshape(SWA_KV_HEADS, SWA_GROUP)))
    return jnp.concatenate([o_a, o_b, o_c, o_d], axis=-1)


def _ln(x):
    mu = jnp.mean(x, -1, keepdims=True)
    xc = x - mu
    var = jnp.mean(xc * xc, -1, keepdims=True)
    return xc * lax.rsqrt(var + NORM_EPS)


def _in_proj_kernel(x_ref, sh_ref, sc_ref, w_ref, o_ref):
    u = _ln(x_ref[...]) * (1.0 + sc_ref[...]) + sh_ref[...]
    o_ref[...] = jnp.dot(u.astype(jnp.bfloat16), w_ref[...], preferred_element_type=jnp.float32)


def in_proj(x, mod, tile_cond, w, *, tt):
    t, d = x.shape
    n = w.shape[1]
    row = lambda k: pl.BlockSpec((None, None, 1, d), lambda i: (tile_cond(i, tt), k, 0, 0))
    return pl.pallas_call(
        _in_proj_kernel,
        grid=(t // tt,),
        in_specs=[pl.BlockSpec((tt, d), lambda i: (i, 0)), row(0), row(1), pl.BlockSpec((d, n), lambda i: (0, 0))],
        out_specs=pl.BlockSpec((tt, n), lambda i: (i, 0)),
        out_shape=jax.ShapeDtypeStruct((t, n), jnp.float32),
        compiler_params=pltpu.CompilerParams(dimension_semantics=("arbitrary",), vmem_limit_bytes=VMEM_LIMIT),
        name="in_proj",
    )(x, mod, mod, w)


def _merge_kernel(x_ref, sh_ref, sc_ref, g_ref, br_ref, wg_ref, wb_ref, wo_ref, lng_ref, lnb_ref, y_ref):
    x = x_ref[...]
    u = (_ln(x) * (1.0 + sc_ref[...]) + sh_ref[...]).astype(jnp.bfloat16)
    acc = jnp.zeros(x.shape, jnp.float32)
    for b in range(N_BRANCH):
        gate = jnp.dot(u, wg_ref[:, b * D_MODEL:(b + 1) * D_MODEL], preferred_element_type=jnp.float32)
        proj = jnp.dot(br_ref[:, b * BRANCH_W:(b + 1) * BRANCH_W].astype(jnp.bfloat16), wb_ref[b],
                       preferred_element_type=jnp.float32)
        acc = acc + jax.nn.sigmoid(gate) * proj
    mix = jnp.dot(acc.astype(jnp.bfloat16), wo_ref[...], preferred_element_type=jnp.float32)
    y_ref[...] = _ln(DEEPNORM_ALPHA * x + g_ref[...] * mix) * lng_ref[...] + lnb_ref[...]


def merge(x, mod, tile_cond, branches, w_gates, w_branch, w_out, ln_g, ln_b, *, tt):
    t, d = x.shape
    row = lambda k: pl.BlockSpec((None, None, 1, d), lambda i: (tile_cond(i, tt), k, 0, 0))
    full = lambda a: pl.BlockSpec(a.shape, lambda i: (0,) * a.ndim)
    return pl.pallas_call(
        _merge_kernel,
        grid=(t // tt,),
        in_specs=[pl.BlockSpec((tt, d), lambda i: (i, 0)), row(0), row(1), row(2),
                  pl.BlockSpec((tt, N_BRANCH * BRANCH_W), lambda i: (i, 0)),
                  full(w_gates), full(w_branch), full(w_out), full(ln_g), full(ln_b)],
        out_specs=pl.BlockSpec((tt, d), lambda i: (i, 0)),
        out_shape=jax.ShapeDtypeStruct((t, d), jnp.float32),
        compiler_params=pltpu.CompilerParams(dimension_semantics=("arbitrary",), vmem_limit_bytes=VMEM_LIMIT),
        name="merge",
    )(x, mod, mod, mod, branches, w_gates, w_branch, w_out, ln_g, ln_b)


def _extract_top(work, n):
    iota = lax.broadcasted_iota(jnp.int32, work.shape, 0)
    rows = work.shape[0]
    vals = []
    for _ in range(n):
        mx = jnp.max(work, axis=0, keepdims=True)
        vals.append(mx)
        first = jnp.min(jnp.where(work == mx, iota, rows), axis=0, keepdims=True)
        work = jnp.where(iota == first, NEG_INF, work)
    return jnp.concatenate(vals, axis=0)


def _peer_route_kernel(x_ref, sh_ref, sc_ref, wq_ref, keys_ref, ut_ref, s1_ref, s2_ref, st_ref, sc_scr):
    tt = x_ref.shape[0]
    u = _ln(x_ref[...]) * (1.0 + sc_ref[...]) + sh_ref[...]
    ut = u.T.astype(jnp.bfloat16)
    ut_ref[...] = ut
    qt = jnp.dot(wq_ref[...], ut, preferred_element_type=jnp.float32)
    for hp in range(2 * PEER_HEADS):
        q_hp = qt[hp * PEER_HALF:(hp + 1) * PEER_HALF, :].astype(jnp.bfloat16)
        sc_scr[hp] = jnp.dot(keys_ref[hp], q_hp, preferred_element_type=jnp.float32)

    def per_head(h, carry):
        for tc in range(tt // LANES):
            ls = slice(tc * LANES, (tc + 1) * LANES)
            s1 = sc_scr[2 * h, :, ls]
            s2 = sc_scr[2 * h + 1, :, ls]
            m = jnp.max(s1, axis=0, keepdims=True) + jnp.max(s2, axis=0, keepdims=True)
            s1 = s1 - m
            a = _extract_top(s1, PEER_TOPK)
            b = _extract_top(s2, PEER_TOPK)
            cands = [a[0:1] + b, a[1:2] + b[0:8]]
            cands += [a[i:i + 1] + b[0:8] for i in range(2, 8)]
            cands += [a[8:16] + b[0:1]]
            cand = jnp.concatenate(cands, axis=0)
            tau = _extract_top(cand, PEER_TOPK)[PEER_TOPK - 1:PEER_TOPK]
            z = jnp.sum(jnp.where(cand >= tau, jnp.exp(cand), 0.0), axis=0, keepdims=True)
            s1_ref[h, :, :, ls] = s1.reshape(PEER_N_KEYS // 8, 8, LANES)
            s2_ref[h, :, ls] = s2
            st_ref[h, :, ls] = jnp.concatenate([tau, 1.0 / z, jnp.zeros((6, LANES), jnp.float32)], axis=0)
        return carry

    lax.fori_loop(0, PEER_HEADS, per_head, 0)


def _peer_dense_kernel(ut_ref, s1_ref, s2_ref, st_ref, u_ref, vt_ref, x_ref, g_ref, lng_ref, lnb_ref,
                       y_ref, acc_ref, act_ref, gt_ref):
    j = pl.program_id(1)
    eb, tt = act_ref.shape
    n_i1 = eb // PEER_N_KEYS
    assert n_i1 == 8

    @pl.when(j == 0)
    def _():
        acc_ref[...] = jnp.zeros_like(acc_ref)

    act_ref[...] = jnp.dot(u_ref[...], ut_ref[...], preferred_element_type=jnp.float32)

    def per_chunk(tc, carry):
        ls = pl.ds(pl.multiple_of(tc * LANES, LANES), LANES)
        for il in range(n_i1):
            w = jnp.zeros((PEER_N_KEYS, LANES), jnp.float32)
            for h in range(PEER_HEADS):
                s1_rows = s1_ref[h, j, :, ls]
                cand = s2_ref[h, :, ls] + s1_rows[il:il + 1]
                val = jnp.exp(cand) * st_ref[h, 1:2, ls]
                w = w + jnp.where(cand >= st_ref[h, 0:1, ls], val, 0.0)
            rs = slice(il * PEER_N_KEYS, (il + 1) * PEER_N_KEYS)
            a = act_ref[rs, ls]
            gelu = 0.5 * a * (1.0 + lax.erf(a * np.float32(np.sqrt(0.5))))
            gt_ref[rs, ls] = (w * gelu).astype(jnp.bfloat16)
        return carry

    lax.fori_loop(0, tt // LANES, per_chunk, 0)
    acc_ref[...] += jnp.dot(vt_ref[...], gt_ref[...], preferred_element_type=jnp.float32)

    @pl.when(j == pl.num_programs(1) - 1)
    def _():
        r = DEEPNORM_ALPHA * x_ref[...] + g_ref[...] * acc_ref[...].T
        y_ref[...] = _ln(r) * lng_ref[...] + lnb_ref[...]


def peer_layer(x, mod, tile_cond, wq_t, keys, u_tab, vt_tab, ln_g, ln_b, *, tt, eb):
    t, d = x.shape
    nt = t // tt
    nh = PEER_HEADS
    row = lambda k: pl.BlockSpec((None, None, 1, d), lambda i, *_: (tile_cond(i, tt), 3 + k, 0, 0))
    ut, s1, s2, st = pl.pallas_call(
        _peer_route_kernel,
        grid=(nt,),
        in_specs=[pl.BlockSpec((tt, d), lambda i: (i, 0)), row(0), row(1),
                  pl.BlockSpec(wq_t.shape, lambda i: (0, 0)),
                  pl.BlockSpec(keys.shape, lambda i: (0, 0, 0))],
        out_specs=[pl.BlockSpec((d, tt), lambda i: (0, i)),
                   pl.BlockSpec((nh, PEER_N_KEYS // 8, 8, tt), lambda i: (0, 0, 0, i)),
                   pl.BlockSpec((nh, PEER_N_KEYS, tt), lambda i: (0, 0, i)),
                   pl.BlockSpec((nh, 8, tt), lambda i: (0, 0, i))],
        out_shape=[jax.ShapeDtypeStruct((d, t), jnp.bfloat16),
                   jax.ShapeDtypeStruct((nh, PEER_N_KEYS // 8, 8, t), jnp.float32),
                   jax.ShapeDtypeStruct((nh, PEER_N_KEYS, t), jnp.float32),
                   jax.ShapeDtypeStruct((nh, 8, t), jnp.float32)],
        scratch_shapes=[pltpu.VMEM((2 * nh, PEER_N_KEYS, tt), jnp.float32)],
        compiler_params=pltpu.CompilerParams(dimension_semantics=("arbitrary",), vmem_limit_bytes=VMEM_LIMIT),
        name="peer_route",
    )(x, mod, mod, wq_t, keys)

    ne = PEER_N_EXPERTS // eb
    y = pl.pallas_call(
        _peer_dense_kernel,
        grid=(nt, ne),
        in_specs=[pl.BlockSpec((d, tt), lambda i, j: (0, i)),
                  pl.BlockSpec((nh, PEER_N_KEYS // 8, 8, tt), lambda i, j: (0, 0, 0, i)),
                  pl.BlockSpec((nh, PEER_N_KEYS, tt), lambda i, j: (0, 0, i)),
                  pl.BlockSpec((nh, 8, tt), lambda i, j: (0, 0, i)),
                  pl.BlockSpec((eb, d), lambda i, j: (j, 0)),
                  pl.BlockSpec((d, eb), lambda i, j: (0, j)),
                  pl.BlockSpec((tt, d), lambda i, j: (i, 0)),
                  row(2),
                  pl.BlockSpec((1, d), lambda i, j: (0, 0)),
                  pl.BlockSpec((1, d), lambda i, j: (0, 0))],
        out_specs=pl.BlockSpec((tt, d), lambda i, j: (i, 0)),
        out_shape=jax.ShapeDtypeStruct((t, d), jnp.float32),
        scratch_shapes=[pltpu.VMEM((d, tt), jnp.float32), pltpu.VMEM((eb, tt), jnp.float32),
                        pltpu.VMEM((eb, tt), jnp.bfloat16)],
        compiler_params=pltpu.CompilerParams(dimension_semantics=("arbitrary", "arbitrary"),
                                             vmem_limit_bytes=VMEM_LIMIT),
        name="peer_dense",
    )(ut, s1, s2, st, u_tab, vt_tab, x, mod, ln_g, ln_b)
    return y


def _tile_cond(i, tt):
    ctx_tiles = BATCH * SEQ // tt
    return jnp.where(i < ctx_tiles, 0, 1 + (i - ctx_tiles) // (DEC_SEQ // tt))


def kernel(x_prompt, x_sample, c, cache_mla_ckv, cache_mla_krope, cache_swa_k, cache_swa_v, state_gla,
           c_ctx, w_ada, b_ada, w_in, mla_q_norm, w_uq, mla_kv_norm, w_ukv,
           w_gla_a_fwd, b_gla_a_fwd, w_gla_a_bwd, b_gla_a_bwd, gla_norm, swa_sink,
           w_branch, w_out, ln1_g, ln1_b, ln2_g, ln2_b, w_peer_q, peer_keys, peer_u, peer_v):
    def layer_params(l):
        return {
            'mla_q_norm': mla_q_norm[l], 'w_uq': w_uq[l], 'mla_kv_norm': mla_kv_norm[l], 'w_ukv': w_ukv[l],
            'w_gla_a_fwd': w_gla_a_fwd[l], 'b_gla_a_fwd': b_gla_a_fwd[l],
            'w_gla_a_bwd': w_gla_a_bwd[l], 'b_gla_a_bwd': b_gla_a_bwd[l], 'gla_norm': gla_norm[l],
            'swa_sink': swa_sink[l],
        }

    bf16 = jnp.bfloat16
    n_ctx, n_lat = BATCH * SEQ, DEC_BATCH * DEC_SEQ
    conds = jnp.concatenate([c_ctx[None, :], c], axis=0)
    x = jnp.concatenate([x_prompt.reshape(n_ctx, D_MODEL), x_sample.reshape(n_lat, D_MODEL)], axis=0)
    ctx_states = []
    for l in range(DEPTH):
        lp = layer_params(l)
        mod = (jax.nn.silu(conds) @ w_ada[l] + b_ada[l]).reshape(1 + DEC_BATCH, 6, 1, D_MODEL)
        w_parts = jnp.pad(w_in[l][:, :GATES_OFFSET], ((0, 0), (0, PARTS_WIDTH - GATES_OFFSET))).astype(bf16)
        hs = in_proj(x, mod, _tile_cond, w_parts, tt=512)
        split = lambda a: dict(zip(IN_NAMES[:-1], jnp.split(a[..., :GATES_OFFSET], IN_OFFSETS[:-1], axis=-1)))
        br_ctx, st = context_branches(split(hs[:n_ctx].reshape(BATCH, SEQ, PARTS_WIDTH)), lp)
        ctx_states.append(st)
        cache = (cache_mla_ckv[:, l], cache_mla_krope[:, l], cache_swa_k[:, l], cache_swa_v[:, l], state_gla[:, l])
        br_lat = latent_branches(split(hs[n_ctx:].reshape(DEC_BATCH, DEC_SEQ, PARTS_WIDTH)), lp, cache)
        branches = jnp.concatenate([br_ctx.reshape(n_ctx, -1), br_lat.reshape(n_lat, -1)], axis=0)
        x1 = merge(x, mod, _tile_cond, branches, w_in[l][:, GATES_OFFSET:].astype(bf16), w_branch[l].astype(bf16),
                   w_out[l].astype(bf16), ln1_g[l][None, :], ln1_b[l][None, :], tt=256)
        x = peer_layer(
            x1, mod, _tile_cond,
            w_peer_q[l].T.astype(bf16),
            peer_keys[l].reshape(2 * PEER_HEADS, PEER_N_KEYS, PEER_HALF).astype(bf16),
            peer_u[l].astype(bf16), peer_v[l].T.astype(bf16),
            ln2_g[l][None, :], ln2_b[l][None, :], tt=PEER_TOKEN_TILE, eb=PEER_EXPERT_BLOCK)

    h = x[:n_ctx].reshape(BATCH, SEQ, D_MODEL)
    z = x[n_ctx:].reshape(DEC_BATCH, DEC_SEQ, D_MODEL)
    new_mla_ckv = jnp.stack([st[0] for st in ctx_states], axis=1)
    new_mla_krope = jnp.stack([st[1] for st in ctx_states], axis=1)
    new_swa_k = jnp.stack([st[2] for st in ctx_states], axis=1)
    new_swa_v = jnp.stack([st[3] for st in ctx_states], axis=1)
    new_gla_state = jnp.stack([st[4] for st in ctx_states], axis=1)
    return (h, z, new_mla_ckv, new_mla_krope, new_swa_k, new_swa_v, new_gla_state)
```

```python
import jax
import jax.numpy as jnp
from jax import lax
import numpy as np
from jax.experimental import pallas as pl
from jax.experimental.pallas import tpu as pltpu

D_MODEL = 1024
BATCH = 32
SEQ = 256
DEPTH = 2
DEC_BATCH = 2
DEC_SEQ = 2048
PAST_LEN = 512

GRID_W = 64
N_BRANCH = 4
BRANCH_W = 256
MLA_HEADS = 4
MLA_Q_LORA = 256
MLA_KV_LORA = 128
MLA_NOPE = 64
MLA_ROPE = 32
MLA_V = 64
MLA_SCALE = (MLA_NOPE + MLA_ROPE) ** -0.5
FNET_GROUPS = 4
FNET_CH = BRANCH_W // FNET_GROUPS
GLA_HEADS = 4
GLA_DK = 32
GLA_DV = 64
GLA_GATE_RANK = 16
GLA_TAU = 16.0
GLA_CHUNK = 64
SWA_HEADS = 4
SWA_KV_HEADS = 2
SWA_GROUP = SWA_HEADS // SWA_KV_HEADS
SWA_HEAD_DIM = 64
SWA_WINDOW = 128
SWA_SCALE = SWA_HEAD_DIM ** -0.5
ATTN_BLOCK = 128
PEER_HEADS = 8
PEER_N_KEYS = 128
PEER_N_EXPERTS = PEER_N_KEYS * PEER_N_KEYS
PEER_KEY_DIM = 256
PEER_HALF = PEER_KEY_DIM // 2
PEER_TOPK = 16
PEER_ROUTE_TILE = 512
PEER_TOKEN_TILE = 1024
PEER_EXPERT_BLOCK = 8 * PEER_N_KEYS
DENSE_CHUNK = 256

ROPE_THETA = 10000.0
NORM_EPS = 1e-6
DEEPNORM_ALPHA = (2.0 * DEPTH) ** 0.25
DEEPNORM_BETA = (8.0 * DEPTH) ** -0.25

IN_SPLITS = (
    ('mla_q', MLA_Q_LORA),
    ('mla_kv', MLA_KV_LORA + MLA_ROPE),
    ('fnet', BRANCH_W),
    ('gla_q', GLA_HEADS * GLA_DK),
    ('gla_k', GLA_HEADS * GLA_DK),
    ('gla_v', GLA_HEADS * GLA_DV),
    ('gla_g', BRANCH_W),
    ('gla_af', GLA_GATE_RANK),
    ('gla_ab', GLA_GATE_RANK),
    ('swa_q', SWA_HEADS * SWA_HEAD_DIM),
    ('swa_k', SWA_KV_HEADS * SWA_HEAD_DIM),
    ('swa_v', SWA_KV_HEADS * SWA_HEAD_DIM),
    ('gates', N_BRANCH * D_MODEL),
)
IN_NAMES = tuple(n for n, _ in IN_SPLITS)
IN_OFFSETS = tuple(int(o) for o in np.cumsum([w for _, w in IN_SPLITS])[:-1])
IN_WIDTH = int(sum(w for _, w in IN_SPLITS))
GATES_OFFSET = IN_OFFSETS[-1]
PARTS_WIDTH = -(-GATES_OFFSET // 128) * 128

LANES = 128
SUBLANES = 8
NEG_INF = float('-inf')
VMEM_LIMIT = 56 * 1024 * 1024


def layer_norm(x, g=None, b=None):
    xf = x.astype(jnp.float32)
    mu = jnp.mean(xf, -1, keepdims=True)
    var = jnp.mean(jnp.square(xf - mu), -1, keepdims=True)
    y = (xf - mu) * lax.rsqrt(var + NORM_EPS)
    if g is not None:
        y = y * g.astype(jnp.float32) + b.astype(jnp.float32)
    return y.astype(x.dtype)


def rms_norm(x, g):
    xf = x.astype(jnp.float32)
    y = xf * lax.rsqrt(jnp.mean(xf * xf, -1, keepdims=True) + NORM_EPS) * g.astype(jnp.float32)
    return y.astype(x.dtype)


def axial_rope(x):
    n = x.shape[-2]
    half = x.shape[-1] // 2
    t = jnp.arange(n)
    rows = (t // GRID_W).astype(jnp.float32)
    cols = (t % GRID_W).astype(jnp.float32)
    freqs = ROPE_THETA ** (-jnp.arange(0, half, 2, dtype=jnp.float32) / half)

    def rot(xa, pos):
        ang = pos[:, None] * freqs[None, :]
        cos, sin = jnp.cos(ang), jnp.sin(ang)
        x1, x2 = xa[..., :half // 2], xa[..., half // 2:]
        return jnp.concatenate([x1 * cos - x2 * sin, x2 * cos + x1 * sin], -1)

    xf = x.astype(jnp.float32)
    return jnp.concatenate([rot(xf[..., :half], rows), rot(xf[..., half:], cols)], -1).astype(x.dtype)


def softmax_with_sink(s, sink):
    if sink is None:
        return jax.nn.softmax(s, -1)
    s_all = jnp.concatenate([s, jnp.broadcast_to(sink, s[..., :1].shape)], -1)
    return jax.nn.softmax(s_all, -1)[..., :-1]


def dense_attention(q, k, v, scale, sink=None):
    b, hk, g, sq, dk = q.shape
    nb = sq // ATTN_BLOCK
    qb = jnp.moveaxis(q.reshape(b, hk, g, nb, ATTN_BLOCK, dk), 3, 0)
    sink_b = None if sink is None else sink.astype(jnp.float32)[None, :, :, None, None]

    def one_block(qi):
        s = jnp.einsum('bhgqd,bhkd->bhgqk', qi, k, preferred_element_type=jnp.float32) * scale
        p = softmax_with_sink(s, sink_b)
        return jnp.einsum('bhgqk,bhkd->bhgqd', p.astype(v.dtype), v)

    o = lax.map(one_block, qb)
    return jnp.moveaxis(o, 0, 3).reshape(b, hk, g, sq, v.shape[-1])


def windowed_attention(q, k, v, k_ctx, v_ctx, sink):
    b, hk, g, s, dh = q.shape
    blk = SWA_WINDOW
    nb = s // blk
    qb = q.reshape(b, hk, g, nb, blk, dh)

    def band(x):
        xp = jnp.pad(x, ((0, 0), (0, 0), (blk, blk), (0, 0))).reshape(b, hk, nb + 2, blk, x.shape[-1])
        return jnp.concatenate([xp[:, :, :-2], xp[:, :, 1:-1], xp[:, :, 2:]], axis=3)

    kb, vb = band(k), band(v)
    qpos = jnp.arange(nb)[:, None] * blk + jnp.arange(blk)[None, :]
    kpos = (jnp.arange(nb)[:, None] - 1) * blk + jnp.arange(3 * blk)[None, :]
    rel = kpos[:, None, :] - qpos[:, :, None]
    valid = (jnp.abs(rel) <= SWA_WINDOW) & (kpos[:, None, :] >= 0) & (kpos[:, None, :] < s)
    s_band = jnp.einsum('bhgnqd,bhnkd->bhgnqk', qb, kb, preferred_element_type=jnp.float32) * SWA_SCALE
    s_band = jnp.where(valid, s_band, -jnp.inf)
    s_ctx = jnp.einsum('bhgnqd,bhld->bhgnql', qb, k_ctx, preferred_element_type=jnp.float32) * SWA_SCALE
    p = softmax_with_sink(jnp.concatenate([s_band, s_ctx], -1),
                          sink.astype(jnp.float32)[None, :, :, None, None, None])
    p_band, p_ctx = p[..., :3 * blk], p[..., 3 * blk:]
    o = (jnp.einsum('bhgnqk,bhnkd->bhgnqd', p_band.astype(v.dtype), vb)
         + jnp.einsum('bhgnql,bhld->bhgnqd', p_ctx.astype(v.dtype), v_ctx))
    return o.reshape(b, hk, g, s, dh)


def gla_scan(q, k, v, log_a, s0):
    b, s, h, dk = q.shape
    dv = v.shape[-1]
    n = s // GLA_CHUNK
    f32 = jnp.float32
    qc, kc, vc, ac = [t.astype(f32).reshape(b, n, GLA_CHUNK, h, t.shape[-1]) for t in (q, k, v, log_a)]
    cum = jnp.cumsum(ac, axis=2)
    causal = jnp.tril(jnp.ones((GLA_CHUNK, GLA_CHUNK), dtype=bool))
    decay = jnp.exp(jnp.minimum(cum[:, :, :, None] - cum[:, :, None, :], 0.0))
    att = jnp.sum(qc[:, :, :, None] * kc[:, :, None, :] * decay, -1)
    att = jnp.where(causal[:, :, None], att, 0.0)
    o_intra = jnp.einsum('bntsh,bnshv->bnthv', att, vc)
    last = cum[:, :, -1:]
    u = jnp.einsum('bnshk,bnshv->bnhkv', kc * jnp.exp(last - cum), vc)
    g = jnp.exp(last[:, :, 0])

    def step(state, xs):
        g_n, u_n = xs
        return g_n[..., None] * state + u_n, state

    s_final, s_in = lax.scan(step, s0.astype(f32), (jnp.moveaxis(g, 1, 0), jnp.moveaxis(u, 1, 0)))
    s_in = jnp.moveaxis(s_in, 0, 1)
    o_inter = jnp.einsum('bnthk,bnhkv->bnthv', qc * jnp.exp(cum), s_in)
    return (o_intra + o_inter).reshape(b, s, h, dv).astype(v.dtype), s_final.astype(v.dtype)


def gla_bidirectional(parts, lp, s0_fwd, s0_bwd):
    b, s, _ = parts['gla_q'].shape
    q = parts['gla_q'].reshape(b, s, GLA_HEADS, GLA_DK) * (GLA_DK ** -0.5)
    k = parts['gla_k'].reshape(b, s, GLA_HEADS, GLA_DK)
    v = parts['gla_v'].reshape(b, s, GLA_HEADS, GLA_DV)

    def log_decay(a_low, w2, b2):
        z = (a_low @ w2 + b2).astype(jnp.float32)
        return (jax.nn.log_sigmoid(z) / GLA_TAU).reshape(b, s, GLA_HEADS, GLA_DK)

    la_f = log_decay(parts['gla_af'], lp['w_gla_a_fwd'], lp['b_gla_a_fwd'])
    la_b = log_decay(parts['gla_ab'], lp['w_gla_a_bwd'], lp['b_gla_a_bwd'])
    o_f, s_f = gla_scan(q, k, v, la_f, s0_fwd)
    flip = lambda t: jnp.flip(t, axis=1)
    o_b, s_b = gla_scan(flip(q), flip(k), flip(v), flip(la_b), s0_bwd)
    o = rms_norm(o_f + flip(o_b), lp['gla_norm']).reshape(b, s, BRANCH_W)
    return o * jax.nn.silu(parts['gla_g']), s_f, s_b


def fourier_mix(f):
    b, s, _ = f.shape
    fg = f.astype(jnp.float32).reshape(b, s, FNET_GROUPS, FNET_CH)
    return jnp.fft.fft2(fg, axes=(1, 3), norm='ortho').real.reshape(b, s, BRANCH_W).astype(f.dtype)


def mla_project(parts, lp, positional):
    b, s, _ = parts['mla_q'].shape
    q = (rms_norm(parts['mla_q'], lp['mla_q_norm']) @ lp['w_uq'])
    q = q.reshape(b, s, MLA_HEADS, MLA_NOPE + MLA_ROPE).transpose(0, 2, 1, 3)
    q_nope, q_rope = q[..., :MLA_NOPE], q[..., MLA_NOPE:]
    ckv = rms_norm(parts['mla_kv'][..., :MLA_KV_LORA], lp['mla_kv_norm'])
    k_rope = parts['mla_kv'][..., MLA_KV_LORA:]
    if positional:
        q_rope = axial_rope(q_rope)
        k_rope = axial_rope(k_rope)
    return jnp.concatenate([q_nope, q_rope], -1), ckv, k_rope


def mla_expand(ckv, k_rope, w_ukv):
    b, s, _ = ckv.shape
    kv = (ckv @ w_ukv).reshape(b, s, MLA_HEADS, MLA_NOPE + MLA_V).transpose(0, 2, 1, 3)
    k_nope, v = kv[..., :MLA_NOPE], kv[..., MLA_NOPE:]
    k = jnp.concatenate([k_nope, jnp.broadcast_to(k_rope[:, None], (b, MLA_HEADS, s, MLA_ROPE))], -1)
    return k, v


def mla_attend(q, k, v):
    o = dense_attention(q[:, :, None], k, v, MLA_SCALE)[:, :, 0]
    b, h, s, dv = o.shape
    return o.transpose(0, 2, 1, 3).reshape(b, s, h * dv)


def swa_project(parts, positional):
    b, s, _ = parts['swa_q'].shape
    q = parts['swa_q'].reshape(b, s, SWA_KV_HEADS, SWA_GROUP, SWA_HEAD_DIM).transpose(0, 2, 3, 1, 4)
    k = parts['swa_k'].reshape(b, s, SWA_KV_HEADS, SWA_HEAD_DIM).transpose(0, 2, 1, 3)
    v = parts['swa_v'].reshape(b, s, SWA_KV_HEADS, SWA_HEAD_DIM).transpose(0, 2, 1, 3)
    if positional:
        q = axial_rope(q)
        k = axial_rope(k)
    return q, k, v


def swa_merge_heads(o):
    b, hk, g, s, dh = o.shape
    return o.transpose(0, 3, 1, 2, 4).reshape(b, s, hk * g * dh)


def context_branches(parts, lp):
    b = parts['mla_q'].shape[0]
    q_a, ckv, k_rope = mla_project(parts, lp, positional=False)
    k_a, v_a = mla_expand(ckv, k_rope, lp['w_ukv'])
    o_a = mla_attend(q_a, k_a, v_a)
    o_b = fourier_mix(parts['fnet'])
    zero = jnp.zeros((b, GLA_HEADS, GLA_DK, GLA_DV), jnp.float32)
    o_c, s_f, s_b = gla_bidirectional(parts, lp, zero, zero)
    q_d, k_d, v_d = swa_project(parts, positional=False)
    o_d = swa_merge_heads(dense_attention(q_d, k_d, v_d, SWA_SCALE,
                                          lp['swa_sink'].reshape(SWA_KV_HEADS, SWA_GROUP)))
    return jnp.concatenate([o_a, o_b, o_c, o_d], axis=-1), (ckv, k_rope, k_d, v_d, jnp.stack([s_f, s_b], axis=1))


def latent_branches(parts, lp, cache):
    ckv_c, krope_c, k_ctx, v_ctx, gla_state = cache
    q_a, ckv, k_rope = mla_project(parts, lp, positional=True)
    k_lat, v_lat = mla_expand(ckv, k_rope, lp['w_ukv'])
    k_c, v_c = mla_expand(ckv_c, krope_c, lp['w_ukv'])
    o_a = mla_attend(q_a, jnp.concatenate([k_c, k_lat], axis=2), jnp.concatenate([v_c, v_lat], axis=2))
    o_b = fourier_mix(parts['fnet'])
    o_c, _, _ = gla_bidirectional(parts, lp, gla_state[:, 0], gla_state[:, 1])
    q_d, k_d, v_d = swa_project(parts, positional=True)
    o_d = swa_merge_heads(windowed_attention(q_d, k_d, v_d, k_ctx, v_ctx,
                                             lp['swa_sink'].reshape(SWA_KV_HEADS, SWA_GROUP)))
    return jnp.concatenate([o_a, o_b, o_c, o_d], axis=-1)


def _ln(x):
    mu = jnp.mean(x, -1, keepdims=True)
    xc = x - mu
    var = jnp.mean(xc * xc, -1, keepdims=True)
    return xc * lax.rsqrt(var + NORM_EPS)


def _in_proj_kernel(x_ref, sh_ref, sc_ref, w_ref, o_ref):
    u = _ln(x_ref[...]) * (1.0 + sc_ref[...]) + sh_ref[...]
    o_ref[...] = jnp.dot(u.astype(jnp.bfloat16), w_ref[...], preferred_element_type=jnp.float32)


def in_proj(x, mod, tile_cond, w, *, tt):
    t, d = x.shape
    n = w.shape[1]
    row = lambda k: pl.BlockSpec((None, None, 1, d), lambda i: (tile_cond(i, tt), k, 0, 0))
    return pl.pallas_call(
        _in_proj_kernel,
        grid=(t // tt,),
        in_specs=[pl.BlockSpec((tt, d), lambda i: (i, 0)), row(0), row(1), pl.BlockSpec((d, n), lambda i: (0, 0))],
        out_specs=pl.BlockSpec((tt, n), lambda i: (i, 0)),
        out_shape=jax.ShapeDtypeStruct((t, n), jnp.float32),
        compiler_params=pltpu.CompilerParams(dimension_semantics=("arbitrary",), vmem_limit_bytes=VMEM_LIMIT),
        name="in_proj",
    )(x, mod, mod, w)


def _merge_kernel(x_ref, sh_ref, sc_ref, g_ref, br_ref, wg_ref, wb_ref, wo_ref, lng_ref, lnb_ref, y_ref):
    x = x_ref[...]
    u = (_ln(x) * (1.0 + sc_ref[...]) + sh_ref[...]).astype(jnp.bfloat16)
    acc = jnp.zeros(x.shape, jnp.float32)
    for b in range(N_BRANCH):
        gate = jnp.dot(u, wg_ref[:, b * D_MODEL:(b + 1) * D_MODEL], preferred_element_type=jnp.float32)
        proj = jnp.dot(br_ref[:, b * BRANCH_W:(b + 1) * BRANCH_W].astype(jnp.bfloat16), wb_ref[b],
                       preferred_element_type=jnp.float32)
        acc = acc + jax.nn.sigmoid(gate) * proj
    mix = jnp.dot(acc.astype(jnp.bfloat16), wo_ref[...], preferred_element_type=jnp.float32)
    y_ref[...] = _ln(DEEPNORM_ALPHA * x + g_ref[...] * mix) * lng_ref[...] + lnb_ref[...]


def merge(x, mod, tile_cond, branches, w_gates, w_branch, w_out, ln_g, ln_b, *, tt):
    t, d = x.shape
    row = lambda k: pl.BlockSpec((None, None, 1, d), lambda i: (tile_cond(i, tt), k, 0, 0))
    full = lambda a: pl.BlockSpec(a.shape, lambda i: (0,) * a.ndim)
    return pl.pallas_call(
        _merge_kernel,
        grid=(t // tt,),
        in_specs=[pl.BlockSpec((tt, d), lambda i: (i, 0)), row(0), row(1), row(2),
                  pl.BlockSpec((tt, N_BRANCH * BRANCH_W), lambda i: (i, 0)),
                  full(w_gates), full(w_branch), full(w_out), full(ln_g), full(ln_b)],
        out_specs=pl.BlockSpec((tt, d), lambda i: (i, 0)),
        out_shape=jax.ShapeDtypeStruct((t, d), jnp.float32),
        compiler_params=pltpu.CompilerParams(dimension_semantics=("arbitrary",), vmem_limit_bytes=VMEM_LIMIT),
        name="merge",
    )(x, mod, mod, mod, branches, w_gates, w_branch, w_out, ln_g, ln_b)


def _oddeven_merge_sort_pairs(n):
    pairs = []
    p = 1
    while p < n:
        k = p
        while k >= 1:
            for j in range(k % p, n - k, 2 * k):
                for i in range(min(k, n - j - k)):
                    if (i + j) // (p * 2) == (i + j + k) // (p * 2):
                        pairs.append((i + j, i + j + k))
            k //= 2
        p *= 2
    return pairs


def _bitonic_merge_pairs(n):
    pairs = []
    k = n // 2
    while k >= 1:
        pairs += [(i, i + k) for i in range(n) if (i & k) == 0]
        k //= 2
    return pairs


def _compare_exchange(v, pairs):
    for i, j in pairs:
        v[i], v[j] = jnp.maximum(v[i], v[j]), jnp.minimum(v[i], v[j])


def _merge_top(v, shifts):
    nv = len(v)
    dropped = None
    for r in shifts:
        other = [pltpu.roll(v[nv - 1 - i], SUBLANES - r, 0) for i in range(nv)]
        lo = [jnp.minimum(v[i], other[i]) for i in range(nv)]
        v = [jnp.maximum(v[i], other[i]) for i in range(nv)]
        _compare_exchange(v, _bitonic_merge_pairs(nv))
        while len(lo) > 1:
            lo = [jnp.maximum(lo[2 * i], lo[2 * i + 1]) for i in range(len(lo) // 2)]
        d = lo[0]
        if dropped is not None:
            d = jnp.maximum(d, jnp.maximum(dropped, pltpu.roll(dropped, SUBLANES - r, 0)))
        dropped = d
    return v, dropped


def _sorted_top(p):
    nv = PEER_N_KEYS // SUBLANES
    v = [p[SUBLANES * i:SUBLANES * (i + 1)] for i in range(nv)]
    _compare_exchange(v, _oddeven_merge_sort_pairs(nv))
    return _merge_top(v, (4, 2, 1))


def _rank16_17(c):
    v = list(c)
    _compare_exchange(v, _oddeven_merge_sort_pairs(len(v)))
    v = v + [pltpu.roll(t, SUBLANES - 4, 0) for t in reversed(v)]
    _compare_exchange(v, _bitonic_merge_pairs(len(v)))
    v, dropped = _merge_top(v, (2, 1))
    return v[PEER_TOPK - 1][0:1], dropped[0:1]


_INNER_PAIRS = [(i, j) for i in range(1, 8) for j in range(1, 8) if (i + 1) * (j + 1) <= PEER_TOPK + 1]


def _peer_route_kernel(x_ref, sh_ref, sc_ref, wq_ref, keys_ref, ut_ref, th_ref, e1_ref, p2_ref, e2_ref,
                       sc_scr, a_scr, b_scr, a1_scr, ai_scr, bi_scr):
    tt = x_ref.shape[0]
    u = _ln(x_ref[...]) * (1.0 + sc_ref[...]) + sh_ref[...]
    ut = u.T.astype(jnp.bfloat16)
    ut_ref[...] = ut
    qt = jnp.dot(wq_ref[...], ut, preferred_element_type=jnp.float32)
    for hp in range(2 * PEER_HEADS):
        q_hp = qt[hp * PEER_HALF:(hp + 1) * PEER_HALF, :].astype(jnp.bfloat16)
        sc_scr[hp] = jnp.dot(keys_ref[hp], q_hp, preferred_element_type=jnp.float32)

    for scr in (a_scr, b_scr, a1_scr, ai_scr, bi_scr):
        scr[...] = jnp.full(scr.shape, NEG_INF, jnp.float32)

    def per_head(h, carry):
        for tc in range(tt // LANES):
            ls = slice(tc * LANES, (tc + 1) * LANES)
            s1 = sc_scr[2 * h, :, ls]
            s2 = sc_scr[2 * h + 1, :, ls]
            p1 = s1 - jnp.max(s1, axis=0, keepdims=True)
            p2 = s2 - jnp.max(s2, axis=0, keepdims=True)
            a, a16 = _sorted_top(p1)
            b, b16 = _sorted_top(p2)
            for i in range(PEER_TOPK + 1):
                ai = a[i][0:1] if i < PEER_TOPK else a16[0:1]
                bi = b[i][0:1] if i < PEER_TOPK else b16[0:1]
                a_scr[i:i + 1, :] = ai
                b_scr[i:i + 1, :] = bi
                if i >= 1:
                    a1_scr[i - 1:i, :] = ai
                for r, (pi, pj) in enumerate(_INNER_PAIRS):
                    if pi == i:
                        ai_scr[r:r + 1, :] = ai
                    if pj == i:
                        bi_scr[r:r + 1, :] = bi
            bb = b_scr[...]
            cand_tiles = [a_scr[0:1, :] + bb[SUBLANES * k:SUBLANES * (k + 1)] for k in range(3)]
            cand_tiles += [a1_scr[SUBLANES * k:SUBLANES * (k + 1), :] + bb[0:1] for k in range(2)]
            cand_tiles += [ai_scr[SUBLANES * k:SUBLANES * (k + 1), :] + bi_scr[SUBLANES * k:SUBLANES * (k + 1), :]
                           for k in range(3)]
            c16, c17 = _rank16_17(cand_tiles)
            tau = 0.5 * (c16 + c17)
            cand = jnp.concatenate(cand_tiles, axis=0)
            z = jnp.sum(jnp.where(cand >= tau, jnp.exp(cand), 0.0), axis=0, keepdims=True)
            th_ref[h, :, :, ls] = (tau - p1).reshape(PEER_N_KEYS // SUBLANES, SUBLANES, LANES)
            e1_ref[h, :, :, ls] = (jnp.exp(p1) / z).reshape(PEER_N_KEYS // SUBLANES, SUBLANES, LANES)
            p2_ref[h, :, ls] = p2
            e2_ref[h, :, ls] = jnp.exp(p2)
        return carry

    lax.fori_loop(0, PEER_HEADS, per_head, 0)


def _peer_dense_kernel(ut_ref, th_ref, e1_ref, p2_ref, e2_ref, u_ref, vt_ref, o_ref, act_ref, gt_ref):
    j = pl.program_id(1)
    eb, tt = act_ref.shape
    n_i1 = eb // PEER_N_KEYS
    assert n_i1 == SUBLANES
    nch = tt // DENSE_CHUNK

    @pl.when(j == 0)
    def _():
        o_ref[...] = jnp.zeros_like(o_ref)

    def mm1(c):
        cs = slice(c * DENSE_CHUNK, (c + 1) * DENSE_CHUNK)
        act_ref[:, cs] = jnp.dot(u_ref[...], ut_ref[:, cs], preferred_element_type=jnp.float32)

    def mm2(c):
        cs = slice(c * DENSE_CHUNK, (c + 1) * DENSE_CHUNK)
        o_ref[:, cs] += jnp.dot(vt_ref[...], gt_ref[:, cs], preferred_element_type=jnp.float32)

    def weights(c):
        for half in range(DENSE_CHUNK // LANES):
            ls = slice(c * DENSE_CHUNK + half * LANES, c * DENSE_CHUNK + (half + 1) * LANES)
            for il in range(n_i1):
                w = None
                for h in range(PEER_HEADS):
                    th_rows = th_ref[h, j, :, ls]
                    e1_rows = e1_ref[h, j, :, ls]
                    wh = jnp.where(p2_ref[h, :, ls] >= th_rows[il:il + 1], e2_ref[h, :, ls] * e1_rows[il:il + 1], 0.0)
                    w = wh if w is None else w + wh
                rs = slice(il * PEER_N_KEYS, (il + 1) * PEER_N_KEYS)
                a = act_ref[rs, ls]
                gelu = 0.5 * a * (1.0 + lax.erf(a * np.float32(np.sqrt(0.5))))
                gt_ref[rs, ls] = (w * gelu).astype(jnp.bfloat16)

    mm1(0)
    for c in range(nch):
        if c + 1 < nch:
            mm1(c + 1)
        weights(c)
        if c >= 1:
            mm2(c - 1)
    mm2(nch - 1)


def _residual_ln_kernel(x_ref, g_ref, pt_ref, lng_ref, lnb_ref, y_ref):
    r = DEEPNORM_ALPHA * x_ref[...] + g_ref[...] * pt_ref[...].T
    y_ref[...] = _ln(r) * lng_ref[...] + lnb_ref[...]


def peer_layer(x, mod, tile_cond, wq_t, keys, u_tab, vt_tab, ln_g, ln_b, *, tt_route, tt, eb):
    t, d = x.shape
    nh = PEER_HEADS
    row = lambda k, tsz: pl.BlockSpec((None, None, 1, d), lambda i, *_: (tile_cond(i, tsz), 3 + k, 0, 0))
    rows_shape = (nh, PEER_N_KEYS // SUBLANES, SUBLANES, t)
    tile_shape = (nh, PEER_N_KEYS, t)
    ut, th, e1, p2, e2 = pl.pallas_call(
        _peer_route_kernel,
        grid=(t // tt_route,),
        in_specs=[pl.BlockSpec((tt_route, d), lambda i: (i, 0)), row(0, tt_route), row(1, tt_route),
                  pl.BlockSpec(wq_t.shape, lambda i: (0, 0)),
                  pl.BlockSpec(keys.shape, lambda i: (0, 0, 0))],
        out_specs=[pl.BlockSpec((d, tt_route), lambda i: (0, i)),
                   pl.BlockSpec(rows_shape[:3] + (tt_route,), lambda i: (0, 0, 0, i)),
                   pl.BlockSpec(rows_shape[:3] + (tt_route,), lambda i: (0, 0, 0, i)),
                   pl.BlockSpec(tile_shape[:2] + (tt_route,), lambda i: (0, 0, i)),
                   pl.BlockSpec(tile_shape[:2] + (tt_route,), lambda i: (0, 0, i))],
        out_shape=[jax.ShapeDtypeStruct((d, t), jnp.bfloat16),
                   jax.ShapeDtypeStruct(rows_shape, jnp.float32), jax.ShapeDtypeStruct(rows_shape, jnp.float32),
                   jax.ShapeDtypeStruct(tile_shape, jnp.float32), jax.ShapeDtypeStruct(tile_shape, jnp.float32)],
        scratch_shapes=[pltpu.VMEM((2 * nh, PEER_N_KEYS, tt_route), jnp.float32),
                        pltpu.VMEM((3 * SUBLANES, LANES), jnp.float32), pltpu.VMEM((3 * SUBLANES, LANES), jnp.float32),
                        pltpu.VMEM((2 * SUBLANES, LANES), jnp.float32), pltpu.VMEM((3 * SUBLANES, LANES), jnp.float32),
                        pltpu.VMEM((3 * SUBLANES, LANES), jnp.float32)],
        compiler_params=pltpu.CompilerParams(dimension_semantics=("arbitrary",), vmem_limit_bytes=VMEM_LIMIT),
        name="peer_route",
    )(x, mod, mod, wq_t, keys)

    ne = PEER_N_EXPERTS // eb
    once = dict(pipeline_mode=pl.Buffered(1))
    pt = pl.pallas_call(
        _peer_dense_kernel,
        grid=(t // tt, ne),
        in_specs=[pl.BlockSpec((d, tt), lambda i, j: (0, i)),
                  pl.BlockSpec(rows_shape[:3] + (tt,), lambda i, j: (0, 0, 0, i), **once),
                  pl.BlockSpec(rows_shape[:3] + (tt,), lambda i, j: (0, 0, 0, i), **once),
                  pl.BlockSpec(tile_shape[:2] + (tt,), lambda i, j: (0, 0, i), **once),
                  pl.BlockSpec(tile_shape[:2] + (tt,), lambda i, j: (0, 0, i), **once),
                  pl.BlockSpec((eb, d), lambda i, j: (j, 0)),
                  pl.BlockSpec((d, eb), lambda i, j: (0, j))],
        out_specs=pl.BlockSpec((d, tt), lambda i, j: (0, i)),
        out_shape=jax.ShapeDtypeStruct((d, t), jnp.float32),
        scratch_shapes=[pltpu.VMEM((eb, tt), jnp.float32), pltpu.VMEM((eb, tt), jnp.bfloat16)],
        compiler_params=pltpu.CompilerParams(dimension_semantics=("arbitrary", "arbitrary"),
                                             vmem_limit_bytes=VMEM_LIMIT),
        name="peer_dense",
    )(ut, th, e1, p2, e2, u_tab, vt_tab)

    tl = PEER_ROUTE_TILE
    return pl.pallas_call(
        _residual_ln_kernel,
        grid=(t // tl,),
        in_specs=[pl.BlockSpec((tl, d), lambda i: (i, 0)), row(2, tl), pl.BlockSpec((d, tl), lambda i: (0, i)),
                  pl.BlockSpec((1, d), lambda i: (0, 0)), pl.BlockSpec((1, d), lambda i: (0, 0))],
        out_specs=pl.BlockSpec((tl, d), lambda i: (i, 0)),
        out_shape=jax.ShapeDtypeStruct((t, d), jnp.float32),
        compiler_params=pltpu.CompilerParams(dimension_semantics=("arbitrary",), vmem_limit_bytes=VMEM_LIMIT),
        name="peer_residual_ln",
    )(x, mod, pt, ln_g, ln_b)


def _tile_cond(i, tt):
    ctx_tiles = BATCH * SEQ // tt
    return jnp.where(i < ctx_tiles, 0, 1 + (i - ctx_tiles) // (DEC_SEQ // tt))


def kernel(x_prompt, x_sample, c, cache_mla_ckv, cache_mla_krope, cache_swa_k, cache_swa_v, state_gla,
           c_ctx, w_ada, b_ada, w_in, mla_q_norm, w_uq, mla_kv_norm, w_ukv,
           w_gla_a_fwd, b_gla_a_fwd, w_gla_a_bwd, b_gla_a_bwd, gla_norm, swa_sink,
           w_branch, w_out, ln1_g, ln1_b, ln2_g, ln2_b, w_peer_q, peer_keys, peer_u, peer_v):
    def layer_params(l):
        return {
            'mla_q_norm': mla_q_norm[l], 'w_uq': w_uq[l], 'mla_kv_norm': mla_kv_norm[l], 'w_ukv': w_ukv[l],
            'w_gla_a_fwd': w_gla_a_fwd[l], 'b_gla_a_fwd': b_gla_a_fwd[l],
            'w_gla_a_bwd': w_gla_a_bwd[l], 'b_gla_a_bwd': b_gla_a_bwd[l], 'gla_norm': gla_norm[l],
            'swa_sink': swa_sink[l],
        }

    bf16 = jnp.bfloat16
    n_ctx, n_lat = BATCH * SEQ, DEC_BATCH * DEC_SEQ
    conds = jnp.concatenate([c_ctx[None, :], c], axis=0)
    x = jnp.concatenate([x_prompt.reshape(n_ctx, D_MODEL), x_sample.reshape(n_lat, D_MODEL)], axis=0)
    ctx_states = []
    for l in range(DEPTH):
        lp = layer_params(l)
        mod = (jax.nn.silu(conds) @ w_ada[l] + b_ada[l]).reshape(1 + DEC_BATCH, 6, 1, D_MODEL)
        w_parts = jnp.pad(w_in[l][:, :GATES_OFFSET], ((0, 0), (0, PARTS_WIDTH - GATES_OFFSET))).astype(bf16)
        hs = in_proj(x, mod, _tile_cond, w_parts, tt=512)
        split = lambda a: dict(zip(IN_NAMES[:-1], jnp.split(a[..., :GATES_OFFSET], IN_OFFSETS[:-1], axis=-1)))
        br_ctx, st = context_branches(split(hs[:n_ctx].reshape(BATCH, SEQ, PARTS_WIDTH)), lp)
        ctx_states.append(st)
        cache = (cache_mla_ckv[:, l], cache_mla_krope[:, l], cache_swa_k[:, l], cache_swa_v[:, l], state_gla[:, l])
        br_lat = latent_branches(split(hs[n_ctx:].reshape(DEC_BATCH, DEC_SEQ, PARTS_WIDTH)), lp, cache)
        branches = jnp.concatenate([br_ctx.reshape(n_ctx, -1), br_lat.reshape(n_lat, -1)], axis=0)
        x1 = merge(x, mod, _tile_cond, branches, w_in[l][:, GATES_OFFSET:].astype(bf16), w_branch[l].astype(bf16),
                   w_out[l].astype(bf16), ln1_g[l][None, :], ln1_b[l][None, :], tt=256)
        x = peer_layer(
            x1, mod, _tile_cond,
            w_peer_q[l].T.astype(bf16),
            peer_keys[l].reshape(2 * PEER_HEADS, PEER_N_KEYS, PEER_HALF).astype(bf16),
            peer_u[l].astype(bf16), peer_v[l].T.astype(bf16),
            ln2_g[l][None, :], ln2_b[l][None, :],
            tt_route=PEER_ROUTE_TILE, tt=PEER_TOKEN_TILE, eb=PEER_EXPERT_BLOCK)

    h = x[:n_ctx].reshape(BATCH, SEQ, D_MODEL)
    z = x[n_ctx:].reshape(DEC_BATCH, DEC_SEQ, D_MODEL)
    new_mla_ckv = jnp.stack([st[0] for st in ctx_states], axis=1)
    new_mla_krope = jnp.stack([st[1] for st in ctx_states], axis=1)
    new_swa_k = jnp.stack([st[2] for st in ctx_states], axis=1)
    new_swa_v = jnp.stack([st[3] for st in ctx_states], axis=1)
    new_gla_state = jnp.stack([st[4] for st in ctx_states], axis=1)
    return (h, z, new_mla_ckv, new_mla_krope, new_swa_k, new_swa_v, new_gla_state)
```

```python
import functools

import jax
import jax.numpy as jnp
from jax import lax
import numpy as np
from jax.experimental import pallas as pl
from jax.experimental.pallas import tpu as pltpu

D_MODEL = 1024
BATCH = 32
SEQ = 256
DEPTH = 2
DEC_BATCH = 2
DEC_SEQ = 2048
PAST_LEN = 512

GRID_W = 64
N_BRANCH = 4
BRANCH_W = 256
MLA_HEADS = 4
MLA_Q_LORA = 256
MLA_KV_LORA = 128
MLA_NOPE = 64
MLA_ROPE = 32
MLA_V = 64
MLA_SCALE = (MLA_NOPE + MLA_ROPE) ** -0.5
FNET_GROUPS = 4
FNET_CH = BRANCH_W // FNET_GROUPS
GLA_HEADS = 4
GLA_DK = 32
GLA_DV = 64
GLA_GATE_RANK = 16
GLA_TAU = 16.0
GLA_CHUNK = 64
SWA_HEADS = 4
SWA_KV_HEADS = 2
SWA_GROUP = SWA_HEADS // SWA_KV_HEADS
SWA_HEAD_DIM = 64
SWA_WINDOW = 128
SWA_SCALE = SWA_HEAD_DIM ** -0.5
ATTN_BLOCK = 128
PEER_HEADS = 8
PEER_N_KEYS = 128
PEER_N_EXPERTS = PEER_N_KEYS * PEER_N_KEYS
PEER_KEY_DIM = 256
PEER_HALF = PEER_KEY_DIM // 2
PEER_TOPK = 16
PEER_ROUTE_TILE = 512
PEER_TOKEN_TILE = 1024
PEER_EXPERT_BLOCK = 8 * PEER_N_KEYS
DENSE_CHUNK = 256
GLA_SEG = 128
GLA_STEPS = 8
HK = GLA_HEADS * GLA_DK
HV = GLA_HEADS * GLA_DV

ROPE_THETA = 10000.0
NORM_EPS = 1e-6
DEEPNORM_ALPHA = (2.0 * DEPTH) ** 0.25
DEEPNORM_BETA = (8.0 * DEPTH) ** -0.25

IN_SPLITS = (
    ('mla_q', MLA_Q_LORA),
    ('mla_kv', MLA_KV_LORA + MLA_ROPE),
    ('fnet', BRANCH_W),
    ('gla_q', GLA_HEADS * GLA_DK),
    ('gla_k', GLA_HEADS * GLA_DK),
    ('gla_v', GLA_HEADS * GLA_DV),
    ('gla_g', BRANCH_W),
    ('gla_af', GLA_GATE_RANK),
    ('gla_ab', GLA_GATE_RANK),
    ('swa_q', SWA_HEADS * SWA_HEAD_DIM),
    ('swa_k', SWA_KV_HEADS * SWA_HEAD_DIM),
    ('swa_v', SWA_KV_HEADS * SWA_HEAD_DIM),
    ('gates', N_BRANCH * D_MODEL),
)
IN_NAMES = tuple(n for n, _ in IN_SPLITS)
IN_OFFSETS = tuple(int(o) for o in np.cumsum([w for _, w in IN_SPLITS])[:-1])
IN_WIDTH = int(sum(w for _, w in IN_SPLITS))
GATES_OFFSET = IN_OFFSETS[-1]
HS_ORDER = ('mla_q', 'fnet', 'gla_q', 'gla_k', 'gla_v', 'gla_g', 'swa_q', 'swa_k', 'swa_v', 'mla_kv', 'gla_af', 'gla_ab')
_widths = dict(IN_SPLITS)
HS_OFFSET = {n: int(o) for n, o in zip(HS_ORDER, np.cumsum([0] + [_widths[n] for n in HS_ORDER])[:-1])}
PARTS_WIDTH = -(-GATES_OFFSET // 128) * 128

LANES = 128
SUBLANES = 8
NEG_INF = float('-inf')
VMEM_LIMIT = 56 * 1024 * 1024


def layer_norm(x, g=None, b=None):
    xf = x.astype(jnp.float32)
    mu = jnp.mean(xf, -1, keepdims=True)
    var = jnp.mean(jnp.square(xf - mu), -1, keepdims=True)
    y = (xf - mu) * lax.rsqrt(var + NORM_EPS)
    if g is not None:
        y = y * g.astype(jnp.float32) + b.astype(jnp.float32)
    return y.astype(x.dtype)


def rms_norm(x, g):
    xf = x.astype(jnp.float32)
    y = xf * lax.rsqrt(jnp.mean(xf * xf, -1, keepdims=True) + NORM_EPS) * g.astype(jnp.float32)
    return y.astype(x.dtype)


def axial_rope(x):
    n = x.shape[-2]
    half = x.shape[-1] // 2
    t = jnp.arange(n)
    rows = (t // GRID_W).astype(jnp.float32)
    cols = (t % GRID_W).astype(jnp.float32)
    freqs = ROPE_THETA ** (-jnp.arange(0, half, 2, dtype=jnp.float32) / half)

    def rot(xa, pos):
        ang = pos[:, None] * freqs[None, :]
        cos, sin = jnp.cos(ang), jnp.sin(ang)
        x1, x2 = xa[..., :half // 2], xa[..., half // 2:]
        return jnp.concatenate([x1 * cos - x2 * sin, x2 * cos + x1 * sin], -1)

    xf = x.astype(jnp.float32)
    return jnp.concatenate([rot(xf[..., :half], rows), rot(xf[..., half:], cols)], -1).astype(x.dtype)


def softmax_with_sink(s, sink):
    if sink is None:
        return jax.nn.softmax(s, -1)
    s_all = jnp.concatenate([s, jnp.broadcast_to(sink, s[..., :1].shape)], -1)
    return jax.nn.softmax(s_all, -1)[..., :-1]


def dense_attention(q, k, v, scale, sink=None):
    b, hk, g, sq, dk = q.shape
    nb = sq // ATTN_BLOCK
    qb = jnp.moveaxis(q.reshape(b, hk, g, nb, ATTN_BLOCK, dk), 3, 0)
    sink_b = None if sink is None else sink.astype(jnp.float32)[None, :, :, None, None]

    def one_block(qi):
        s = jnp.einsum('bhgqd,bhkd->bhgqk', qi, k, preferred_element_type=jnp.float32) * scale
        p = softmax_with_sink(s, sink_b)
        return jnp.einsum('bhgqk,bhkd->bhgqd', p.astype(v.dtype), v)

    o = lax.map(one_block, qb)
    return jnp.moveaxis(o, 0, 3).reshape(b, hk, g, sq, v.shape[-1])


def windowed_attention(q, k, v, k_ctx, v_ctx, sink):
    b, hk, g, s, dh = q.shape
    blk = SWA_WINDOW
    nb = s // blk
    qb = q.reshape(b, hk, g, nb, blk, dh)

    def band(x):
        xp = jnp.pad(x, ((0, 0), (0, 0), (blk, blk), (0, 0))).reshape(b, hk, nb + 2, blk, x.shape[-1])
        return jnp.concatenate([xp[:, :, :-2], xp[:, :, 1:-1], xp[:, :, 2:]], axis=3)

    kb, vb = band(k), band(v)
    qpos = jnp.arange(nb)[:, None] * blk + jnp.arange(blk)[None, :]
    kpos = (jnp.arange(nb)[:, None] - 1) * blk + jnp.arange(3 * blk)[None, :]
    rel = kpos[:, None, :] - qpos[:, :, None]
    valid = (jnp.abs(rel) <= SWA_WINDOW) & (kpos[:, None, :] >= 0) & (kpos[:, None, :] < s)
    s_band = jnp.einsum('bhgnqd,bhnkd->bhgnqk', qb, kb, preferred_element_type=jnp.float32) * SWA_SCALE
    s_band = jnp.where(valid, s_band, -jnp.inf)
    s_ctx = jnp.einsum('bhgnqd,bhld->bhgnql', qb, k_ctx, preferred_element_type=jnp.float32) * SWA_SCALE
    p = softmax_with_sink(jnp.concatenate([s_band, s_ctx], -1),
                          sink.astype(jnp.float32)[None, :, :, None, None, None])
    p_band, p_ctx = p[..., :3 * blk], p[..., 3 * blk:]
    o = (jnp.einsum('bhgnqk,bhnkd->bhgnqd', p_band.astype(v.dtype), vb)
         + jnp.einsum('bhgnql,bhld->bhgnqd', p_ctx.astype(v.dtype), v_ctx))
    return o.reshape(b, hk, g, s, dh)


def fourier_mix(f):
    b, s, _ = f.shape
    fg = f.astype(jnp.float32).reshape(b, s, FNET_GROUPS, FNET_CH)
    return jnp.fft.fft2(fg, axes=(1, 3), norm='ortho').real.reshape(b, s, BRANCH_W).astype(f.dtype)


def mla_project(parts, lp, positional):
    b, s, _ = parts['mla_q'].shape
    q = (rms_norm(parts['mla_q'], lp['mla_q_norm']) @ lp['w_uq'])
    q = q.reshape(b, s, MLA_HEADS, MLA_NOPE + MLA_ROPE).transpose(0, 2, 1, 3)
    q_nope, q_rope = q[..., :MLA_NOPE], q[..., MLA_NOPE:]
    ckv = rms_norm(parts['mla_kv'][..., :MLA_KV_LORA], lp['mla_kv_norm'])
    k_rope = parts['mla_kv'][..., MLA_KV_LORA:]
    if positional:
        q_rope = axial_rope(q_rope)
        k_rope = axial_rope(k_rope)
    return jnp.concatenate([q_nope, q_rope], -1), ckv, k_rope


def mla_expand(ckv, k_rope, w_ukv):
    b, s, _ = ckv.shape
    kv = (ckv @ w_ukv).reshape(b, s, MLA_HEADS, MLA_NOPE + MLA_V).transpose(0, 2, 1, 3)
    k_nope, v = kv[..., :MLA_NOPE], kv[..., MLA_NOPE:]
    k = jnp.concatenate([k_nope, jnp.broadcast_to(k_rope[:, None], (b, MLA_HEADS, s, MLA_ROPE))], -1)
    return k, v


def mla_attend(q, k, v):
    o = dense_attention(q[:, :, None], k, v, MLA_SCALE)[:, :, 0]
    b, h, s, dv = o.shape
    return o.transpose(0, 2, 1, 3).reshape(b, s, h * dv)


def swa_project(parts, positional):
    b, s, _ = parts['swa_q'].shape
    q = parts['swa_q'].reshape(b, s, SWA_KV_HEADS, SWA_GROUP, SWA_HEAD_DIM).transpose(0, 2, 3, 1, 4)
    k = parts['swa_k'].reshape(b, s, SWA_KV_HEADS, SWA_HEAD_DIM).transpose(0, 2, 1, 3)
    v = parts['swa_v'].reshape(b, s, SWA_KV_HEADS, SWA_HEAD_DIM).transpose(0, 2, 1, 3)
    if positional:
        q = axial_rope(q)
        k = axial_rope(k)
    return q, k, v


def swa_merge_heads(o):
    b, hk, g, s, dh = o.shape
    return o.transpose(0, 3, 1, 2, 4).reshape(b, s, hk * g * dh)


def context_branches(parts, lp):
    q_a, ckv, k_rope = mla_project(parts, lp, positional=False)
    k_a, v_a = mla_expand(ckv, k_rope, lp['w_ukv'])
    o_a = mla_attend(q_a, k_a, v_a)
    o_b = fourier_mix(parts['fnet'])
    q_d, k_d, v_d = swa_project(parts, positional=False)
    o_d = swa_merge_heads(dense_attention(q_d, k_d, v_d, SWA_SCALE,
                                          lp['swa_sink'].reshape(SWA_KV_HEADS, SWA_GROUP)))
    return (o_a, o_b, o_d), (ckv, k_rope, k_d, v_d)


def latent_branches(parts, lp, cache):
    ckv_c, krope_c, k_ctx, v_ctx = cache
    q_a, ckv, k_rope = mla_project(parts, lp, positional=True)
    k_lat, v_lat = mla_expand(ckv, k_rope, lp['w_ukv'])
    k_c, v_c = mla_expand(ckv_c, krope_c, lp['w_ukv'])
    o_a = mla_attend(q_a, jnp.concatenate([k_c, k_lat], axis=2), jnp.concatenate([v_c, v_lat], axis=2))
    o_b = fourier_mix(parts['fnet'])
    q_d, k_d, v_d = swa_project(parts, positional=True)
    o_d = swa_merge_heads(windowed_attention(q_d, k_d, v_d, k_ctx, v_ctx,
                                             lp['swa_sink'].reshape(SWA_KV_HEADS, SWA_GROUP)))
    return o_a, o_b, o_d


def _ln(x):
    mu = jnp.mean(x, -1, keepdims=True)
    xc = x - mu
    var = jnp.mean(xc * xc, -1, keepdims=True)
    return xc * lax.rsqrt(var + NORM_EPS)


def _in_proj_kernel(x_ref, sh_ref, sc_ref, w_ref, o_ref):
    u = _ln(x_ref[...]) * (1.0 + sc_ref[...]) + sh_ref[...]
    o_ref[...] = jnp.dot(u.astype(jnp.bfloat16), w_ref[...], preferred_element_type=jnp.float32)


def in_proj(x, mod, tile_cond, w, *, tt):
    t, d = x.shape
    n = w.shape[1]
    row = lambda k: pl.BlockSpec((None, None, 1, d), lambda i: (tile_cond(i, tt), k, 0, 0))
    return pl.pallas_call(
        _in_proj_kernel,
        grid=(t // tt,),
        in_specs=[pl.BlockSpec((tt, d), lambda i: (i, 0)), row(0), row(1), pl.BlockSpec((d, n), lambda i: (0, 0))],
        out_specs=pl.BlockSpec((tt, n), lambda i: (i, 0)),
        out_shape=jax.ShapeDtypeStruct((t, n), jnp.float32),
        compiler_params=pltpu.CompilerParams(dimension_semantics=("arbitrary",), vmem_limit_bytes=VMEM_LIMIT),
        name="in_proj",
    )(x, mod, mod, w)


def _merge_kernel(x_ref, sh_ref, sc_ref, g_ref, ba_ref, bb_ref, bc_ref, bd_ref, wg_ref, wb_ref, wo_ref,
                  lng_ref, lnb_ref, y_ref):
    x = x_ref[...]
    u = (_ln(x) * (1.0 + sc_ref[...]) + sh_ref[...]).astype(jnp.bfloat16)
    acc = jnp.zeros(x.shape, jnp.float32)
    for b, br_ref in enumerate((ba_ref, bb_ref, bc_ref, bd_ref)):
        gate = jnp.dot(u, wg_ref[:, b * D_MODEL:(b + 1) * D_MODEL], preferred_element_type=jnp.float32)
        proj = jnp.dot(br_ref[...].astype(jnp.bfloat16), wb_ref[b], preferred_element_type=jnp.float32)
        acc = acc + jax.nn.sigmoid(gate) * proj
    mix = jnp.dot(acc.astype(jnp.bfloat16), wo_ref[...], preferred_element_type=jnp.float32)
    y_ref[...] = _ln(DEEPNORM_ALPHA * x + g_ref[...] * mix) * lng_ref[...] + lnb_ref[...]


def merge(x, mod, tile_cond, branches, w_gates, w_branch, w_out, ln_g, ln_b, *, tt):
    t, d = x.shape
    row = lambda k: pl.BlockSpec((None, None, 1, d), lambda i: (tile_cond(i, tt), k, 0, 0))
    full = lambda a: pl.BlockSpec(a.shape, lambda i: (0,) * a.ndim)
    return pl.pallas_call(
        _merge_kernel,
        grid=(t // tt,),
        in_specs=[pl.BlockSpec((tt, d), lambda i: (i, 0)), row(0), row(1), row(2),
                  *[pl.BlockSpec((tt, BRANCH_W), lambda i: (i, 0)) for _ in range(N_BRANCH)],
                  full(w_gates), full(w_branch), full(w_out), full(ln_g), full(ln_b)],
        out_specs=pl.BlockSpec((tt, d), lambda i: (i, 0)),
        out_shape=jax.ShapeDtypeStruct((t, d), jnp.float32),
        compiler_params=pltpu.CompilerParams(dimension_semantics=("arbitrary",), vmem_limit_bytes=VMEM_LIMIT),
        name="merge",
    )(x, mod, mod, mod, *branches, w_gates, w_branch, w_out, ln_g, ln_b)


def _gla_local_kernel(q_ref, k_ref, v_ref, z_ref, w2_ref, b2_ref, o_ref, qd_ref, st_ref, dl_ref,
                      s_scr, d_scr, stage, ta, tk, tq, *, reverse):
    i = pl.program_id(0)
    nseg = q_ref.shape[0]

    @pl.when(i == 0)
    def _():
        s_scr[...] = jnp.zeros_like(s_scr)
        d_scr[...] = jnp.ones_like(d_scr)
        stage[...] = jnp.zeros_like(stage)

    def to_lanes(x):
        stage[0:nseg, :] = x
        return stage[...].T

    def from_lanes(xt):
        return xt.T[0:nseg, :]

    steps = range(GLA_STEPS - 1, -1, -1) if reverse else range(GLA_STEPS)
    for j in steps:
        z = jnp.dot(z_ref[:, j, :].astype(jnp.bfloat16), w2_ref[...], preferred_element_type=jnp.float32) + b2_ref[...]
        ta[...] = to_lanes(jnp.exp(jax.nn.log_sigmoid(z) / GLA_TAU))
        tq[...] = to_lanes(q_ref[:, j, :] * (GLA_DK ** -0.5))
        tk[...] = to_lanes(k_ref[:, j, :])
        d = d_scr[...] * ta[...]
        d_scr[...] = d
        qd_ref[:, j, :] = from_lanes(tq[...] * d)
        vt = [to_lanes(v_ref[:, j, LANES * c:LANES * (c + 1)]) for c in range(HV // LANES)]
        outs = []
        for h in range(GLA_HEADS):
            off = (h * GLA_DV) % LANES
            vh = vt[(h * GLA_DV) // LANES][off:off + GLA_DV]

            def rows(g, acc, h=h, vh=vh):
                r0 = pl.multiple_of(h * GLA_DK + g * SUBLANES, SUBLANES)
                a8 = ta[pl.ds(r0, SUBLANES), :]
                k8 = tk[pl.ds(r0, SUBLANES), :]
                q8 = tq[pl.ds(r0, SUBLANES), :]
                for r in range(SUBLANES):
                    s_new = a8[r:r + 1] * s_scr[r0 + r] + k8[r:r + 1] * vh
                    s_scr[r0 + r] = s_new
                    acc = acc + q8[r:r + 1] * s_new
                return acc

            outs.append(lax.fori_loop(0, GLA_DK // SUBLANES, rows, jnp.zeros((GLA_DV, LANES), jnp.float32)))
        for c in range(HV // LANES):
            per = LANES // GLA_DV
            o_ref[:, j, LANES * c:LANES * (c + 1)] = from_lanes(jnp.concatenate(outs[per * c:per * (c + 1)], axis=0))

    @pl.when(i == pl.num_programs(0) - 1)
    def _():
        st_ref[...] = s_scr[...]
        dl_ref[...] = d_scr[...].T


def gla_local(hs3, w2pad, b2, cols, *, reverse):
    nseg, seg, _ = hs3.shape
    nb = seg // GLA_STEPS
    tmap = (lambda i: nb - 1 - i) if reverse else (lambda i: i)
    blk = lambda width, cb: pl.BlockSpec((nseg, GLA_STEPS, width), lambda i: (0, tmap(i), cb))
    t3 = lambda width: jax.ShapeDtypeStruct((nseg, seg, width), jnp.float32)
    return pl.pallas_call(
        functools.partial(_gla_local_kernel, reverse=reverse),
        grid=(nb,),
        in_specs=[blk(HK, cols[0]), blk(HK, cols[1]), blk(HV, cols[2]), blk(LANES, cols[3]),
                  pl.BlockSpec((LANES, HK), lambda i: (0, 0)), pl.BlockSpec((1, HK), lambda i: (0, 0))],
        out_specs=[pl.BlockSpec((nseg, GLA_STEPS, HV), lambda i: (0, tmap(i), 0)),
                   pl.BlockSpec((nseg, GLA_STEPS, HK), lambda i: (0, tmap(i), 0)),
                   pl.BlockSpec((HK, GLA_DV, LANES), lambda i: (0, 0, 0)),
                   pl.BlockSpec((LANES, HK), lambda i: (0, 0))],
        out_shape=[t3(HV), t3(HK), jax.ShapeDtypeStruct((HK, GLA_DV, LANES), jnp.float32),
                   jax.ShapeDtypeStruct((LANES, HK), jnp.float32)],
        scratch_shapes=[pltpu.VMEM((HK, GLA_DV, LANES), jnp.float32), pltpu.VMEM((HK, LANES), jnp.float32),
                        pltpu.VMEM((LANES, LANES), jnp.float32), pltpu.VMEM((HK, LANES), jnp.float32),
                        pltpu.VMEM((HK, LANES), jnp.float32), pltpu.VMEM((HK, LANES), jnp.float32)],
        compiler_params=pltpu.CompilerParams(dimension_semantics=("arbitrary",), vmem_limit_bytes=VMEM_LIMIT),
        name="gla_local_bwd" if reverse else "gla_local_fwd",
    )(hs3, hs3, hs3, hs3, w2pad, b2)


def _seg_seq(c, n_ctx_seg, ctx_per, lat_per):
    return jnp.where(c < n_ctx_seg, c // ctx_per, n_ctx_seg // ctx_per + (c - n_ctx_seg) // lat_per)


def _seg_pos(c, n_ctx_seg, ctx_per, lat_per):
    return (jnp.where(c < n_ctx_seg, c % ctx_per, (c - n_ctx_seg) % lat_per),
            jnp.where(c < n_ctx_seg, ctx_per, lat_per))


def _gla_chain_kernel(*refs, reverse, combine, seg_args):
    if combine:
        ol_ref, qd_ref, sl_ref, dl_ref, s0_ref, of_ref, gate_ref, gn_ref, o_ref, fin_ref, st = refs
    else:
        ol_ref, qd_ref, sl_ref, dl_ref, s0_ref, o_ref, fin_ref, st = refs
    i = pl.program_id(0)
    c = pl.num_programs(0) - 1 - i if reverse else i
    pos, per = _seg_pos(c, *seg_args)
    first = (pos == per - 1) if reverse else (pos == 0)

    @pl.when(first)
    def _():
        st[...] = s0_ref[...]

    s_in = st[...]
    o = ol_ref[...] + lax.dot_general(qd_ref[...].astype(jnp.bfloat16), s_in.astype(jnp.bfloat16),
                                      (((1,), (1,)), ((), ())), preferred_element_type=jnp.float32)
    row_head = lax.broadcasted_iota(jnp.int32, (HV, HK), 0) // GLA_DV
    col_head = lax.broadcasted_iota(jnp.int32, (HV, HK), 1) // GLA_DK
    s_loc = jnp.where(row_head == col_head, jnp.concatenate([sl_ref[...]] * GLA_HEADS, axis=0), 0.0)
    s_new = dl_ref[...] * s_in + s_loc
    st[...] = s_new
    fin_ref[...] = s_new
    if combine:
        x = o + of_ref[...]
        lane_head = lax.broadcasted_iota(jnp.int32, x.shape, 1) // GLA_DV
        x2 = x * x
        scale = jnp.zeros_like(x)
        for h in range(GLA_HEADS):
            ms = jnp.sum(jnp.where(lane_head == h, x2, 0.0), axis=-1, keepdims=True) * (1.0 / GLA_DV)
            scale = jnp.where(lane_head == h, lax.rsqrt(ms + NORM_EPS), scale)
        o = x * scale * gn_ref[...] * jax.nn.silu(gate_ref[...])
    o_ref[...] = o


def gla_chain(o_loc, qd, sl_t, dl, s0_t, seg_args, *, reverse, o_fwd=None, hs3=None, gate_col=None, gnorm=None):
    nseg, seg, _ = o_loc.shape
    n_seq = s0_t.shape[0]
    cmap = (lambda i: nseg - 1 - i) if reverse else (lambda i: i)
    seq = lambda i: _seg_seq(cmap(i), *seg_args)
    combine = o_fwd is not None
    in_specs = [pl.BlockSpec((None, seg, HV), lambda i: (cmap(i), 0, 0)),
                pl.BlockSpec((None, seg, HK), lambda i: (cmap(i), 0, 0)),
                pl.BlockSpec((None, GLA_DV, HK), lambda i: (cmap(i), 0, 0)),
                pl.BlockSpec((None, 1, HK), lambda i: (cmap(i), 0, 0)),
                pl.BlockSpec((None, HV, HK), lambda i: (seq(i), 0, 0))]
    args = [o_loc, qd, sl_t, dl.reshape(dl.shape[0], 1, HK), s0_t]
    if combine:
        in_specs += [pl.BlockSpec((None, seg, HV), lambda i: (cmap(i), 0, 0)),
                     pl.BlockSpec((None, seg, HV), lambda i: (cmap(i), 0, gate_col)),
                     pl.BlockSpec((1, HV), lambda i: (0, 0))]
        args += [o_fwd, hs3, gnorm]
    return pl.pallas_call(
        functools.partial(_gla_chain_kernel, reverse=reverse, combine=combine, seg_args=seg_args),
        grid=(nseg,),
        in_specs=in_specs,
        out_specs=[pl.BlockSpec((None, seg, HV), lambda i: (cmap(i), 0, 0)),
                   pl.BlockSpec((None, HV, HK), lambda i: (seq(i), 0, 0))],
        out_shape=[jax.ShapeDtypeStruct((nseg, seg, HV), jnp.float32),
                   jax.ShapeDtypeStruct((n_seq, HV, HK), jnp.float32)],
        scratch_shapes=[pltpu.VMEM((HV, HK), jnp.float32)],
        compiler_params=pltpu.CompilerParams(dimension_semantics=("arbitrary",), vmem_limit_bytes=VMEM_LIMIT),
        name="gla_chain_bwd" if reverse else "gla_chain_fwd",
    )(*args)


def _state_to_chain(s):
    n = s.shape[0]
    eye = jnp.eye(GLA_HEADS, dtype=s.dtype)
    return jnp.einsum('nhkv,hg->nhvgk', s, eye).reshape(n, HV, HK)


def _state_from_chain(f):
    n = f.shape[0]
    f5 = f.reshape(n, GLA_HEADS, GLA_DV, GLA_HEADS, GLA_DK)
    return jnp.stack([f5[:, h, :, h, :] for h in range(GLA_HEADS)], axis=1).transpose(0, 1, 3, 2)


def gla_branch(hs3, cols, w2f, b2f, w2b, b2b, gnorm, s0_fwd, s0_bwd, seg_args):
    nseg = hs3.shape[0]
    o_f = None
    for reverse, w2, b2, s0 in ((False, w2f, b2f, s0_fwd), (True, w2b, b2b, s0_bwd)):
        o_loc, qd, st, dl = gla_local(hs3, w2, b2, cols[:4], reverse=reverse)
        sl_t = jnp.transpose(st, (2, 1, 0))[:nseg]
        if not reverse:
            o_f, fin_f = gla_chain(o_loc, qd, sl_t, dl[:nseg], _state_to_chain(s0), seg_args, reverse=False)
        else:
            o, fin_b = gla_chain(o_loc, qd, sl_t, dl[:nseg], _state_to_chain(s0), seg_args, reverse=True,
                                 o_fwd=o_f, hs3=hs3, gate_col=cols[4], gnorm=gnorm)
    return o, _state_from_chain(fin_f), _state_from_chain(fin_b)


def _oddeven_merge_sort_pairs(n):
    pairs = []
    p = 1
    while p < n:
        k = p
        while k >= 1:
            for j in range(k % p, n - k, 2 * k):
                for i in range(min(k, n - j - k)):
                    if (i + j) // (p * 2) == (i + j + k) // (p * 2):
                        pairs.append((i + j, i + j + k))
            k //= 2
        p *= 2
    return pairs


def _bitonic_merge_pairs(n):
    pairs = []
    k = n // 2
    while k >= 1:
        pairs += [(i, i + k) for i in range(n) if (i & k) == 0]
        k //= 2
    return pairs


def _compare_exchange(v, pairs):
    for i, j in pairs:
        v[i], v[j] = jnp.maximum(v[i], v[j]), jnp.minimum(v[i], v[j])


def _merge_top(v, shifts):
    nv = len(v)
    dropped = None
    for r in shifts:
        other = [pltpu.roll(v[nv - 1 - i], SUBLANES - r, 0) for i in range(nv)]
        lo = [jnp.minimum(v[i], other[i]) for i in range(nv)]
        v = [jnp.maximum(v[i], other[i]) for i in range(nv)]
        _compare_exchange(v, _bitonic_merge_pairs(nv))
        while len(lo) > 1:
            lo = [jnp.maximum(lo[2 * i], lo[2 * i + 1]) for i in range(len(lo) // 2)]
        d = lo[0]
        if dropped is not None:
            d = jnp.maximum(d, jnp.maximum(dropped, pltpu.roll(dropped, SUBLANES - r, 0)))
        dropped = d
    return v, dropped


def _sorted_top(p):
    nv = PEER_N_KEYS // SUBLANES
    v = [p[SUBLANES * i:SUBLANES * (i + 1)] for i in range(nv)]
    _compare_exchange(v, _oddeven_merge_sort_pairs(nv))
    return _merge_top(v, (4, 2, 1))


def _rank16_17(c):
    v = list(c)
    _compare_exchange(v, _oddeven_merge_sort_pairs(len(v)))
    v = v + [pltpu.roll(t, SUBLANES - 4, 0) for t in reversed(v)]
    _compare_exchange(v, _bitonic_merge_pairs(len(v)))
    v, dropped = _merge_top(v, (2, 1))
    return v[PEER_TOPK - 1][0:1], dropped[0:1]


_INNER_PAIRS = [(i, j) for i in range(1, 8) for j in range(1, 8) if (i + 1) * (j + 1) <= PEER_TOPK + 1]


def _peer_route_kernel(x_ref, sh_ref, sc_ref, wq_ref, keys_ref, ut_ref, th_ref, e1_ref, p2_ref, e2_ref,
                       sc_scr, a_scr, b_scr, a1_scr, ai_scr, bi_scr):
    tt = x_ref.shape[0]
    u = _ln(x_ref[...]) * (1.0 + sc_ref[...]) + sh_ref[...]
    ut = u.T.astype(jnp.bfloat16)
    ut_ref[...] = ut
    qt = jnp.dot(wq_ref[...], ut, preferred_element_type=jnp.float32)
    for hp in range(2 * PEER_HEADS):
        q_hp = qt[hp * PEER_HALF:(hp + 1) * PEER_HALF, :].astype(jnp.bfloat16)
        sc_scr[hp] = jnp.dot(keys_ref[hp], q_hp, preferred_element_type=jnp.float32)

    for scr in (a_scr, b_scr, a1_scr, ai_scr, bi_scr):
        scr[...] = jnp.full(scr.shape, NEG_INF, jnp.float32)

    def per_head(h, carry):
        for tc in range(tt // LANES):
            ls = slice(tc * LANES, (tc + 1) * LANES)
            s1 = sc_scr[2 * h, :, ls]
            s2 = sc_scr[2 * h + 1, :, ls]
            p1 = s1 - jnp.max(s1, axis=0, keepdims=True)
            p2 = s2 - jnp.max(s2, axis=0, keepdims=True)
            a, a16 = _sorted_top(p1)
            b, b16 = _sorted_top(p2)
            for i in range(PEER_TOPK + 1):
                ai = a[i][0:1] if i < PEER_TOPK else a16[0:1]
                bi = b[i][0:1] if i < PEER_TOPK else b16[0:1]
                a_scr[i:i + 1, :] = ai
                b_scr[i:i + 1, :] = bi
                if i >= 1:
                    a1_scr[i - 1:i, :] = ai
                for r, (pi, pj) in enumerate(_INNER_PAIRS):
                    if pi == i:
                        ai_scr[r:r + 1, :] = ai
                    if pj == i:
                        bi_scr[r:r + 1, :] = bi
            bb = b_scr[...]
            cand_tiles = [a_scr[0:1, :] + bb[SUBLANES * k:SUBLANES * (k + 1)] for k in range(3)]
            cand_tiles += [a1_scr[SUBLANES * k:SUBLANES * (k + 1), :] + bb[0:1] for k in range(2)]
            cand_tiles += [ai_scr[SUBLANES * k:SUBLANES * (k + 1), :] + bi_scr[SUBLANES * k:SUBLANES * (k + 1), :]
                           for k in range(3)]
            c16, c17 = _rank16_17(cand_tiles)
            tau = 0.5 * (c16 + c17)
            cand = jnp.concatenate(cand_tiles, axis=0)
            z = jnp.sum(jnp.where(cand >= tau, jnp.exp(cand), 0.0), axis=0, keepdims=True)
            th_ref[h, :, :, ls] = (tau - p1).reshape(PEER_N_KEYS // SUBLANES, SUBLANES, LANES)
            e1_ref[h, :, :, ls] = (jnp.exp(p1) / z).reshape(PEER_N_KEYS // SUBLANES, SUBLANES, LANES)
            p2_ref[h, :, ls] = p2
            e2_ref[h, :, ls] = jnp.exp(p2)
        return carry

    lax.fori_loop(0, PEER_HEADS, per_head, 0)


def _peer_dense_kernel(ut_ref, th_ref, e1_ref, p2_ref, e2_ref, u_ref, vt_ref, o_ref, act_ref, gt_ref):
    j = pl.program_id(1)
    eb, tt = act_ref.shape
    n_i1 = eb // PEER_N_KEYS
    assert n_i1 == SUBLANES
    nch = tt // DENSE_CHUNK

    @pl.when(j == 0)
    def _():
        o_ref[...] = jnp.zeros_like(o_ref)

    def mm1(c):
        cs = slice(c * DENSE_CHUNK, (c + 1) * DENSE_CHUNK)
        act_ref[:, cs] = jnp.dot(u_ref[...], ut_ref[:, cs], preferred_element_type=jnp.float32)

    def mm2(c):
        cs = slice(c * DENSE_CHUNK, (c + 1) * DENSE_CHUNK)
        o_ref[:, cs] += jnp.dot(vt_ref[...], gt_ref[:, cs], preferred_element_type=jnp.float32)

    def weights(c):
        for half in range(DENSE_CHUNK // LANES):
            ls = slice(c * DENSE_CHUNK + half * LANES, c * DENSE_CHUNK + (half + 1) * LANES)
            for il in range(n_i1):
                w = None
                for h in range(PEER_HEADS):
                    th_rows = th_ref[h, j, :, ls]
                    e1_rows = e1_ref[h, j, :, ls]
                    wh = jnp.where(p2_ref[h, :, ls] >= th_rows[il:il + 1], e2_ref[h, :, ls] * e1_rows[il:il + 1], 0.0)
                    w = wh if w is None else w + wh
                rs = slice(il * PEER_N_KEYS, (il + 1) * PEER_N_KEYS)
                a = act_ref[rs, ls]
                gelu = 0.5 * a * (1.0 + lax.erf(a * np.float32(np.sqrt(0.5))))
                gt_ref[rs, ls] = (w * gelu).astype(jnp.bfloat16)

    mm1(0)
    for c in range(nch):
        if c + 1 < nch:
            mm1(c + 1)
        weights(c)
        if c >= 1:
            mm2(c - 1)
    mm2(nch - 1)


def _residual_ln_kernel(x_ref, g_ref, pt_ref, lng_ref, lnb_ref, y_ref):
    r = DEEPNORM_ALPHA * x_ref[...] + g_ref[...] * pt_ref[...].T
    y_ref[...] = _ln(r) * lng_ref[...] + lnb_ref[...]


def peer_layer(x, mod, tile_cond, wq_t, keys, u_tab, vt_tab, ln_g, ln_b, *, tt_route, tt, eb):
    t, d = x.shape
    nh = PEER_HEADS
    row = lambda k, tsz: pl.BlockSpec((None, None, 1, d), lambda i, *_: (tile_cond(i, tsz), 3 + k, 0, 0))
    rows_shape = (nh, PEER_N_KEYS // SUBLANES, SUBLANES, t)
    tile_shape = (nh, PEER_N_KEYS, t)
    ut, th, e1, p2, e2 = pl.pallas_call(
        _peer_route_kernel,
        grid=(t // tt_route,),
        in_specs=[pl.BlockSpec((tt_route, d), lambda i: (i, 0)), row(0, tt_route), row(1, tt_route),
                  pl.BlockSpec(wq_t.shape, lambda i: (0, 0)),
                  pl.BlockSpec(keys.shape, lambda i: (0, 0, 0))],
        out_specs=[pl.BlockSpec((d, tt_route), lambda i: (0, i)),
                   pl.BlockSpec(rows_shape[:3] + (tt_route,), lambda i: (0, 0, 0, i)),
                   pl.BlockSpec(rows_shape[:3] + (tt_route,), lambda i: (0, 0, 0, i)),
                   pl.BlockSpec(tile_shape[:2] + (tt_route,), lambda i: (0, 0, i)),
                   pl.BlockSpec(tile_shape[:2] + (tt_route,), lambda i: (0, 0, i))],
        out_shape=[jax.ShapeDtypeStruct((d, t), jnp.bfloat16),
                   jax.ShapeDtypeStruct(rows_shape, jnp.float32), jax.ShapeDtypeStruct(rows_shape, jnp.float32),
                   jax.ShapeDtypeStruct(tile_shape, jnp.float32), jax.ShapeDtypeStruct(tile_shape, jnp.float32)],
        scratch_shapes=[pltpu.VMEM((2 * nh, PEER_N_KEYS, tt_route), jnp.float32),
                        pltpu.VMEM((3 * SUBLANES, LANES), jnp.float32), pltpu.VMEM((3 * SUBLANES, LANES), jnp.float32),
                        pltpu.VMEM((2 * SUBLANES, LANES), jnp.float32), pltpu.VMEM((3 * SUBLANES, LANES), jnp.float32),
                        pltpu.VMEM((3 * SUBLANES, LANES), jnp.float32)],
        compiler_params=pltpu.CompilerParams(dimension_semantics=("arbitrary",), vmem_limit_bytes=VMEM_LIMIT),
        name="peer_route",
    )(x, mod, mod, wq_t, keys)

    ne = PEER_N_EXPERTS // eb
    once = dict(pipeline_mode=pl.Buffered(1))
    pt = pl.pallas_call(
        _peer_dense_kernel,
        grid=(t // tt, ne),
        in_specs=[pl.BlockSpec((d, tt), lambda i, j: (0, i)),
                  pl.BlockSpec(rows_shape[:3] + (tt,), lambda i, j: (0, 0, 0, i), **once),
                  pl.BlockSpec(rows_shape[:3] + (tt,), lambda i, j: (0, 0, 0, i), **once),
                  pl.BlockSpec(tile_shape[:2] + (tt,), lambda i, j: (0, 0, i), **once),
                  pl.BlockSpec(tile_shape[:2] + (tt,), lambda i, j: (0, 0, i), **once),
                  pl.BlockSpec((eb, d), lambda i, j: (j, 0)),
                  pl.BlockSpec((d, eb), lambda i, j: (0, j))],
        out_specs=pl.BlockSpec((d, tt), lambda i, j: (0, i)),
        out_shape=jax.ShapeDtypeStruct((d, t), jnp.float32),
        scratch_shapes=[pltpu.VMEM((eb, tt), jnp.float32), pltpu.VMEM((eb, tt), jnp.bfloat16)],
        compiler_params=pltpu.CompilerParams(dimension_semantics=("arbitrary", "arbitrary"),
                                             vmem_limit_bytes=VMEM_LIMIT),
        name="peer_dense",
    )(ut, th, e1, p2, e2, u_tab, vt_tab)

    tl = PEER_ROUTE_TILE
    return pl.pallas_call(
        _residual_ln_kernel,
        grid=(t // tl,),
        in_specs=[pl.BlockSpec((tl, d), lambda i: (i, 0)), row(2, tl), pl.BlockSpec((d, tl), lambda i: (0, i)),
                  pl.BlockSpec((1, d), lambda i: (0, 0)), pl.BlockSpec((1, d), lambda i: (0, 0))],
        out_specs=pl.BlockSpec((tl, d), lambda i: (i, 0)),
        out_shape=jax.ShapeDtypeStruct((t, d), jnp.float32),
        compiler_params=pltpu.CompilerParams(dimension_semantics=("arbitrary",), vmem_limit_bytes=VMEM_LIMIT),
        name="peer_residual_ln",
    )(x, mod, pt, ln_g, ln_b)


def _tile_cond(i, tt):
    ctx_tiles = BATCH * SEQ // tt
    return jnp.where(i < ctx_tiles, 0, 1 + (i - ctx_tiles) // (DEC_SEQ // tt))


def kernel(x_prompt, x_sample, c, cache_mla_ckv, cache_mla_krope, cache_swa_k, cache_swa_v, state_gla,
           c_ctx, w_ada, b_ada, w_in, mla_q_norm, w_uq, mla_kv_norm, w_ukv,
           w_gla_a_fwd, b_gla_a_fwd, w_gla_a_bwd, b_gla_a_bwd, gla_norm, swa_sink,
           w_branch, w_out, ln1_g, ln1_b, ln2_g, ln2_b, w_peer_q, peer_keys, peer_u, peer_v):
    def layer_params(l):
        return {
            'mla_q_norm': mla_q_norm[l], 'w_uq': w_uq[l], 'mla_kv_norm': mla_kv_norm[l], 'w_ukv': w_ukv[l],
            'swa_sink': swa_sink[l],
        }

    bf16 = jnp.bfloat16
    n_ctx, n_lat = BATCH * SEQ, DEC_BATCH * DEC_SEQ
    n_tok = n_ctx + n_lat
    conds = jnp.concatenate([c_ctx[None, :], c], axis=0)
    x = jnp.concatenate([x_prompt.reshape(n_ctx, D_MODEL), x_sample.reshape(n_lat, D_MODEL)], axis=0)
    src = dict(zip(IN_NAMES, (0,) + IN_OFFSETS))
    seg_args = (n_ctx // GLA_SEG, SEQ // GLA_SEG, DEC_SEQ // GLA_SEG)
    lane_block = lambda name, width: HS_OFFSET[name] // width
    gla_cols = (lane_block('gla_q', HK), lane_block('gla_k', HK), lane_block('gla_v', HV),
                HS_OFFSET['gla_af'] // LANES, lane_block('gla_g', HV))
    z_row = lambda name: HS_OFFSET[name] - gla_cols[3] * LANES
    ctx_states = []
    for l in range(DEPTH):
        lp = layer_params(l)
        mod = (jax.nn.silu(conds) @ w_ada[l] + b_ada[l]).reshape(1 + DEC_BATCH, 6, 1, D_MODEL)
        w_parts = jnp.concatenate([w_in[l][:, src[n]:src[n] + _widths[n]] for n in HS_ORDER], axis=1)
        w_parts = jnp.pad(w_parts, ((0, 0), (0, PARTS_WIDTH - GATES_OFFSET))).astype(bf16)
        hs = in_proj(x, mod, _tile_cond, w_parts, tt=512)
        split = lambda a: {n: a[..., HS_OFFSET[n]:HS_OFFSET[n] + _widths[n]] for n in HS_ORDER}
        (a_ctx, b_ctx, d_ctx), st = context_branches(split(hs[:n_ctx].reshape(BATCH, SEQ, PARTS_WIDTH)), lp)
        cache = (cache_mla_ckv[:, l], cache_mla_krope[:, l], cache_swa_k[:, l], cache_swa_v[:, l])
        a_lat, b_lat, d_lat = latent_branches(split(hs[n_ctx:].reshape(DEC_BATCH, DEC_SEQ, PARTS_WIDTH)), lp, cache)
        both = lambda u, v: jnp.concatenate([u.reshape(n_ctx, BRANCH_W), v.reshape(n_lat, BRANCH_W)], axis=0)
        w2 = lambda w, name: (jnp.zeros((LANES, HK), jnp.float32)
                              .at[z_row(name):z_row(name) + GLA_GATE_RANK].set(w).astype(bf16))
        zero = jnp.zeros((BATCH, GLA_HEADS, GLA_DK, GLA_DV), jnp.float32)
        o_c, s_f, s_b = gla_branch(
            hs.reshape(n_tok // GLA_SEG, GLA_SEG, PARTS_WIDTH), gla_cols,
            w2(w_gla_a_fwd[l], 'gla_af'), b_gla_a_fwd[l][None, :], w2(w_gla_a_bwd[l], 'gla_ab'), b_gla_a_bwd[l][None, :],
            jnp.tile(gla_norm[l], GLA_HEADS)[None, :],
            jnp.concatenate([zero, state_gla[:, l, 0]], axis=0), jnp.concatenate([zero, state_gla[:, l, 1]], axis=0),
            seg_args)
        ctx_states.append(st + (jnp.stack([s_f[:BATCH], s_b[:BATCH]], axis=1),))
        branches = (both(a_ctx, a_lat), both(b_ctx, b_lat), o_c.reshape(n_tok, BRANCH_W), both(d_ctx, d_lat))
        x1 = merge(x, mod, _tile_cond, branches, w_in[l][:, GATES_OFFSET:].astype(bf16), w_branch[l].astype(bf16),
                   w_out[l].astype(bf16), ln1_g[l][None, :], ln1_b[l][None, :], tt=256)
        x = peer_layer(
            x1, mod, _tile_cond,
            w_peer_q[l].T.astype(bf16),
            peer_keys[l].reshape(2 * PEER_HEADS, PEER_N_KEYS, PEER_HALF).astype(bf16),
            peer_u[l].astype(bf16), peer_v[l].T.astype(bf16),
            ln2_g[l][None, :], ln2_b[l][None, :],
            tt_route=PEER_ROUTE_TILE, tt=PEER_TOKEN_TILE, eb=PEER_EXPERT_BLOCK)

    h = x[:n_ctx].reshape(BATCH, SEQ, D_MODEL)
    z = x[n_ctx:].reshape(DEC_BATCH, DEC_SEQ, D_MODEL)
    new_mla_ckv = jnp.stack([st[0] for st in ctx_states], axis=1)
    new_mla_krope = jnp.stack([st[1] for st in ctx_states], axis=1)
    new_swa_k = jnp.stack([st[2] for st in ctx_states], axis=1)
    new_swa_v = jnp.stack([st[3] for st in ctx_states], axis=1)
    new_gla_state = jnp.stack([st[4] for st in ctx_states], axis=1)
    return (h, z, new_mla_ckv, new_mla_krope, new_swa_k, new_swa_v, new_gla_state)
```

```python
import functools

import jax
import jax.numpy as jnp
from jax import lax
import numpy as np
from jax.experimental import pallas as pl
from jax.experimental.pallas import tpu as pltpu

D_MODEL = 1024
BATCH = 32
SEQ = 256
DEPTH = 2
DEC_BATCH = 2
DEC_SEQ = 2048
PAST_LEN = 512

GRID_W = 64
N_BRANCH = 4
BRANCH_W = 256
MLA_HEADS = 4
MLA_Q_LORA = 256
MLA_KV_LORA = 128
MLA_NOPE = 64
MLA_ROPE = 32
MLA_V = 64
MLA_SCALE = (MLA_NOPE + MLA_ROPE) ** -0.5
FNET_GROUPS = 4
FNET_CH = BRANCH_W // FNET_GROUPS
GLA_HEADS = 4
GLA_DK = 32
GLA_DV = 64
GLA_GATE_RANK = 16
GLA_TAU = 16.0
GLA_CHUNK = 64
SWA_HEADS = 4
SWA_KV_HEADS = 2
SWA_GROUP = SWA_HEADS // SWA_KV_HEADS
SWA_HEAD_DIM = 64
SWA_WINDOW = 128
SWA_SCALE = SWA_HEAD_DIM ** -0.5
ATTN_BLOCK = 128
PEER_HEADS = 8
PEER_N_KEYS = 128
PEER_N_EXPERTS = PEER_N_KEYS * PEER_N_KEYS
PEER_KEY_DIM = 256
PEER_HALF = PEER_KEY_DIM // 2
PEER_TOPK = 16
PEER_ROUTE_TILE = 512
PEER_TOKEN_TILE = 1024
PEER_EXPERT_BLOCK = 8 * PEER_N_KEYS
DENSE_CHUNK = 256
GLA_SEG = 128
GLA_STEPS = 8
HK = GLA_HEADS * GLA_DK
HV = GLA_HEADS * GLA_DV
LAT_QB = SWA_WINDOW

ROPE_THETA = 10000.0
NORM_EPS = 1e-6
DEEPNORM_ALPHA = (2.0 * DEPTH) ** 0.25
DEEPNORM_BETA = (8.0 * DEPTH) ** -0.25

IN_SPLITS = (
    ('mla_q', MLA_Q_LORA),
    ('mla_kv', MLA_KV_LORA + MLA_ROPE),
    ('fnet', BRANCH_W),
    ('gla_q', GLA_HEADS * GLA_DK),
    ('gla_k', GLA_HEADS * GLA_DK),
    ('gla_v', GLA_HEADS * GLA_DV),
    ('gla_g', BRANCH_W),
    ('gla_af', GLA_GATE_RANK),
    ('gla_ab', GLA_GATE_RANK),
    ('swa_q', SWA_HEADS * SWA_HEAD_DIM),
    ('swa_k', SWA_KV_HEADS * SWA_HEAD_DIM),
    ('swa_v', SWA_KV_HEADS * SWA_HEAD_DIM),
    ('gates', N_BRANCH * D_MODEL),
)
IN_NAMES = tuple(n for n, _ in IN_SPLITS)
IN_OFFSETS = tuple(int(o) for o in np.cumsum([w for _, w in IN_SPLITS])[:-1])
IN_WIDTH = int(sum(w for _, w in IN_SPLITS))
GATES_OFFSET = IN_OFFSETS[-1]
HS_ORDER = ('mla_q', 'fnet', 'gla_q', 'gla_k', 'gla_v', 'gla_g', 'swa_q', 'swa_k', 'swa_v', 'mla_kv', 'gla_af', 'gla_ab')
_widths = dict(IN_SPLITS)
HS_OFFSET = {n: int(o) for n, o in zip(HS_ORDER, np.cumsum([0] + [_widths[n] for n in HS_ORDER])[:-1])}
PARTS_WIDTH = -(-GATES_OFFSET // 128) * 128

LANES = 128
SUBLANES = 8
NEG_INF = float('-inf')
VMEM_LIMIT = 56 * 1024 * 1024


def _ln(x):
    mu = jnp.mean(x, -1, keepdims=True)
    xc = x - mu
    var = jnp.mean(xc * xc, -1, keepdims=True)
    return xc * lax.rsqrt(var + NORM_EPS)


def _in_proj_kernel(x_ref, sh_ref, sc_ref, w_ref, o_ref):
    u = _ln(x_ref[...]) * (1.0 + sc_ref[...]) + sh_ref[...]
    o_ref[...] = jnp.dot(u.astype(jnp.bfloat16), w_ref[...], preferred_element_type=jnp.float32)


def in_proj(x, mod, tile_cond, w, *, tt):
    t, d = x.shape
    n = w.shape[1]
    row = lambda k: pl.BlockSpec((None, None, 1, d), lambda i: (tile_cond(i, tt), k, 0, 0))
    return pl.pallas_call(
        _in_proj_kernel,
        grid=(t // tt,),
        in_specs=[pl.BlockSpec((tt, d), lambda i: (i, 0)), row(0), row(1), pl.BlockSpec((d, n), lambda i: (0, 0))],
        out_specs=pl.BlockSpec((tt, n), lambda i: (i, 0)),
        out_shape=jax.ShapeDtypeStruct((t, n), jnp.float32),
        compiler_params=pltpu.CompilerParams(dimension_semantics=("arbitrary",), vmem_limit_bytes=VMEM_LIMIT),
        name="in_proj",
    )(x, mod, mod, w)


def _merge_kernel(x_ref, sh_ref, sc_ref, g_ref, ba_ref, bb_ref, bc_ref, bd_ref, wg_ref, wb_ref, wo_ref,
                  lng_ref, lnb_ref, y_ref):
    x = x_ref[...]
    u = (_ln(x) * (1.0 + sc_ref[...]) + sh_ref[...]).astype(jnp.bfloat16)
    acc = jnp.zeros(x.shape, jnp.float32)
    for b, br_ref in enumerate((ba_ref, bb_ref, bc_ref, bd_ref)):
        gate = jnp.dot(u, wg_ref[:, b * D_MODEL:(b + 1) * D_MODEL], preferred_element_type=jnp.float32)
        proj = jnp.dot(br_ref[...].astype(jnp.bfloat16), wb_ref[b], preferred_element_type=jnp.float32)
        acc = acc + jax.nn.sigmoid(gate) * proj
    mix = jnp.dot(acc.astype(jnp.bfloat16), wo_ref[...], preferred_element_type=jnp.float32)
    y_ref[...] = _ln(DEEPNORM_ALPHA * x + g_ref[...] * mix) * lng_ref[...] + lnb_ref[...]


def merge(x, mod, tile_cond, branches, w_gates, w_branch, w_out, ln_g, ln_b, *, tt):
    t, d = x.shape
    row = lambda k: pl.BlockSpec((None, None, 1, d), lambda i: (tile_cond(i, tt), k, 0, 0))
    full = lambda a: pl.BlockSpec(a.shape, lambda i: (0,) * a.ndim)
    return pl.pallas_call(
        _merge_kernel,
        grid=(t // tt,),
        in_specs=[pl.BlockSpec((tt, d), lambda i: (i, 0)), row(0), row(1), row(2),
                  *[pl.BlockSpec((tt, BRANCH_W), lambda i: (i, 0)) for _ in range(N_BRANCH)],
                  full(w_gates), full(w_branch), full(w_out), full(ln_g), full(ln_b)],
        out_specs=pl.BlockSpec((tt, d), lambda i: (i, 0)),
        out_shape=jax.ShapeDtypeStruct((t, d), jnp.float32),
        compiler_params=pltpu.CompilerParams(dimension_semantics=("arbitrary",), vmem_limit_bytes=VMEM_LIMIT),
        name="merge",
    )(x, mod, mod, mod, *branches, w_gates, w_branch, w_out, ln_g, ln_b)


def _gla_local_kernel(q_ref, k_ref, v_ref, z_ref, w2_ref, b2_ref, o_ref, qd_ref, st_ref, dl_ref,
                      s_scr, d_scr, stage, ta, tk, tq, *, reverse):
    i = pl.program_id(0)
    nseg = q_ref.shape[0]

    @pl.when(i == 0)
    def _():
        s_scr[...] = jnp.zeros_like(s_scr)
        d_scr[...] = jnp.ones_like(d_scr)
        stage[...] = jnp.zeros_like(stage)

    def to_lanes(x):
        stage[0:nseg, :] = x
        return stage[...].T

    def from_lanes(xt):
        return xt.T[0:nseg, :]

    steps = range(GLA_STEPS - 1, -1, -1) if reverse else range(GLA_STEPS)
    for j in steps:
        z = jnp.dot(z_ref[:, j, :].astype(jnp.bfloat16), w2_ref[...], preferred_element_type=jnp.float32) + b2_ref[...]
        ta[...] = to_lanes(jnp.exp(jax.nn.log_sigmoid(z) / GLA_TAU))
        tq[...] = to_lanes(q_ref[:, j, :] * (GLA_DK ** -0.5))
        tk[...] = to_lanes(k_ref[:, j, :])
        d = d_scr[...] * ta[...]
        d_scr[...] = d
        qd_ref[:, j, :] = from_lanes(tq[...] * d)
        vt = [to_lanes(v_ref[:, j, LANES * c:LANES * (c + 1)]) for c in range(HV // LANES)]
        outs = []
        for h in range(GLA_HEADS):
            off = (h * GLA_DV) % LANES
            vh = vt[(h * GLA_DV) // LANES][off:off + GLA_DV]

            def rows(g, acc, h=h, vh=vh):
                r0 = pl.multiple_of(h * GLA_DK + g * SUBLANES, SUBLANES)
                a8 = ta[pl.ds(r0, SUBLANES), :]
                k8 = tk[pl.ds(r0, SUBLANES), :]
                q8 = tq[pl.ds(r0, SUBLANES), :]
                for r in range(SUBLANES):
                    s_new = a8[r:r + 1] * s_scr[r0 + r] + k8[r:r + 1] * vh
                    s_scr[r0 + r] = s_new
                    acc = acc + q8[r:r + 1] * s_new
                return acc

            outs.append(lax.fori_loop(0, GLA_DK // SUBLANES, rows, jnp.zeros((GLA_DV, LANES), jnp.float32)))
        for c in range(HV // LANES):
            per = LANES // GLA_DV
            o_ref[:, j, LANES * c:LANES * (c + 1)] = from_lanes(jnp.concatenate(outs[per * c:per * (c + 1)], axis=0))

    @pl.when(i == pl.num_programs(0) - 1)
    def _():
        st_ref[...] = s_scr[...]
        dl_ref[...] = d_scr[...].T


def gla_local(hs3, w2pad, b2, cols, *, reverse):
    nseg, seg, _ = hs3.shape
    nb = seg // GLA_STEPS
    tmap = (lambda i: nb - 1 - i) if reverse else (lambda i: i)
    blk = lambda width, cb: pl.BlockSpec((nseg, GLA_STEPS, width), lambda i: (0, tmap(i), cb))
    t3 = lambda width: jax.ShapeDtypeStruct((nseg, seg, width), jnp.float32)
    return pl.pallas_call(
        functools.partial(_gla_local_kernel, reverse=reverse),
        grid=(nb,),
        in_specs=[blk(HK, cols[0]), blk(HK, cols[1]), blk(HV, cols[2]), blk(LANES, cols[3]),
                  pl.BlockSpec((LANES, HK), lambda i: (0, 0)), pl.BlockSpec((1, HK), lambda i: (0, 0))],
        out_specs=[pl.BlockSpec((nseg, GLA_STEPS, HV), lambda i: (0, tmap(i), 0)),
                   pl.BlockSpec((nseg, GLA_STEPS, HK), lambda i: (0, tmap(i), 0)),
                   pl.BlockSpec((HK, GLA_DV, LANES), lambda i: (0, 0, 0)),
                   pl.BlockSpec((LANES, HK), lambda i: (0, 0))],
        out_shape=[t3(HV), t3(HK), jax.ShapeDtypeStruct((HK, GLA_DV, LANES), jnp.float32),
                   jax.ShapeDtypeStruct((LANES, HK), jnp.float32)],
        scratch_shapes=[pltpu.VMEM((HK, GLA_DV, LANES), jnp.float32), pltpu.VMEM((HK, LANES), jnp.float32),
                        pltpu.VMEM((LANES, LANES), jnp.float32), pltpu.VMEM((HK, LANES), jnp.float32),
                        pltpu.VMEM((HK, LANES), jnp.float32), pltpu.VMEM((HK, LANES), jnp.float32)],
        compiler_params=pltpu.CompilerParams(dimension_semantics=("arbitrary",), vmem_limit_bytes=VMEM_LIMIT),
        name="gla_local_bwd" if reverse else "gla_local_fwd",
    )(hs3, hs3, hs3, hs3, w2pad, b2)


def _seg_seq(c, n_ctx_seg, ctx_per, lat_per):
    return jnp.where(c < n_ctx_seg, c // ctx_per, n_ctx_seg // ctx_per + (c - n_ctx_seg) // lat_per)


def _seg_pos(c, n_ctx_seg, ctx_per, lat_per):
    return (jnp.where(c < n_ctx_seg, c % ctx_per, (c - n_ctx_seg) % lat_per),
            jnp.where(c < n_ctx_seg, ctx_per, lat_per))


def _gla_chain_kernel(*refs, reverse, combine, seg_args):
    if combine:
        ol_ref, qd_ref, sl_ref, dl_ref, s0_ref, of_ref, gate_ref, gn_ref, o_ref, fin_ref, st = refs
    else:
        ol_ref, qd_ref, sl_ref, dl_ref, s0_ref, o_ref, fin_ref, st = refs
    i = pl.program_id(0)
    c = pl.num_programs(0) - 1 - i if reverse else i
    pos, per = _seg_pos(c, *seg_args)
    first = (pos == per - 1) if reverse else (pos == 0)

    @pl.when(first)
    def _():
        st[...] = s0_ref[...]

    s_in = st[...]
    o = ol_ref[...] + lax.dot_general(qd_ref[...].astype(jnp.bfloat16), s_in.astype(jnp.bfloat16),
                                      (((1,), (1,)), ((), ())), preferred_element_type=jnp.float32)
    row_head = lax.broadcasted_iota(jnp.int32, (HV, HK), 0) // GLA_DV
    col_head = lax.broadcasted_iota(jnp.int32, (HV, HK), 1) // GLA_DK
    s_loc = jnp.where(row_head == col_head, jnp.concatenate([sl_ref[...]] * GLA_HEADS, axis=0), 0.0)
    s_new = dl_ref[...] * s_in + s_loc
    st[...] = s_new
    fin_ref[...] = s_new
    if combine:
        x = o + of_ref[...]
        lane_head = lax.broadcasted_iota(jnp.int32, x.shape, 1) // GLA_DV
        x2 = x * x
        scale = jnp.zeros_like(x)
        for h in range(GLA_HEADS):
            ms = jnp.sum(jnp.where(lane_head == h, x2, 0.0), axis=-1, keepdims=True) * (1.0 / GLA_DV)
            scale = jnp.where(lane_head == h, lax.rsqrt(ms + NORM_EPS), scale)
        o = x * scale * gn_ref[...] * jax.nn.silu(gate_ref[...])
    o_ref[...] = o


def gla_chain(o_loc, qd, sl_t, dl, s0_t, seg_args, *, reverse, o_fwd=None, hs3=None, gate_col=None, gnorm=None):
    nseg, seg, _ = o_loc.shape
    n_seq = s0_t.shape[0]
    cmap = (lambda i: nseg - 1 - i) if reverse else (lambda i: i)
    seq = lambda i: _seg_seq(cmap(i), *seg_args)
    combine = o_fwd is not None
    in_specs = [pl.BlockSpec((None, seg, HV), lambda i: (cmap(i), 0, 0)),
                pl.BlockSpec((None, seg, HK), lambda i: (cmap(i), 0, 0)),
                pl.BlockSpec((None, GLA_DV, HK), lambda i: (cmap(i), 0, 0)),
                pl.BlockSpec((None, 1, HK), lambda i: (cmap(i), 0, 0)),
                pl.BlockSpec((None, HV, HK), lambda i: (seq(i), 0, 0))]
    args = [o_loc, qd, sl_t, dl.reshape(dl.shape[0], 1, HK), s0_t]
    if combine:
        in_specs += [pl.BlockSpec((None, seg, HV), lambda i: (cmap(i), 0, 0)),
                     pl.BlockSpec((None, seg, HV), lambda i: (cmap(i), 0, gate_col)),
                     pl.BlockSpec((1, HV), lambda i: (0, 0))]
        args += [o_fwd, hs3, gnorm]
    return pl.pallas_call(
        functools.partial(_gla_chain_kernel, reverse=reverse, combine=combine, seg_args=seg_args),
        grid=(nseg,),
        in_specs=in_specs,
        out_specs=[pl.BlockSpec((None, seg, HV), lambda i: (cmap(i), 0, 0)),
                   pl.BlockSpec((None, HV, HK), lambda i: (seq(i), 0, 0))],
        out_shape=[jax.ShapeDtypeStruct((nseg, seg, HV), jnp.float32),
                   jax.ShapeDtypeStruct((n_seq, HV, HK), jnp.float32)],
        scratch_shapes=[pltpu.VMEM((HV, HK), jnp.float32)],
        compiler_params=pltpu.CompilerParams(dimension_semantics=("arbitrary",), vmem_limit_bytes=VMEM_LIMIT),
        name="gla_chain_bwd" if reverse else "gla_chain_fwd",
    )(*args)


def _state_to_chain(s):
    n = s.shape[0]
    eye = jnp.eye(GLA_HEADS, dtype=s.dtype)
    return jnp.einsum('nhkv,hg->nhvgk', s, eye).reshape(n, HV, HK)


def _state_from_chain(f):
    n = f.shape[0]
    f5 = f.reshape(n, GLA_HEADS, GLA_DV, GLA_HEADS, GLA_DK)
    return jnp.stack([f5[:, h, :, h, :] for h in range(GLA_HEADS)], axis=1).transpose(0, 1, 3, 2)


def gla_branch(hs3, cols, w2f, b2f, w2b, b2b, gnorm, s0_fwd, s0_bwd, seg_args):
    nseg = hs3.shape[0]
    o_f = None
    for reverse, w2, b2, s0 in ((False, w2f, b2f, s0_fwd), (True, w2b, b2b, s0_bwd)):
        o_loc, qd, st, dl = gla_local(hs3, w2, b2, cols[:4], reverse=reverse)
        sl_t = jnp.transpose(st, (2, 1, 0))[:nseg]
        if not reverse:
            o_f, fin_f = gla_chain(o_loc, qd, sl_t, dl[:nseg], _state_to_chain(s0), seg_args, reverse=False)
        else:
            o, fin_b = gla_chain(o_loc, qd, sl_t, dl[:nseg], _state_to_chain(s0), seg_args, reverse=True,
                                 o_fwd=o_f, hs3=hs3, gate_col=cols[4], gnorm=gnorm)
    return o, _state_from_chain(fin_f), _state_from_chain(fin_b)


_NT = (((1,), (1,)), ((), ()))
_F32 = dict(preferred_element_type=jnp.float32)


def _rms(x, g):
    return x * lax.rsqrt(jnp.mean(x * x, -1, keepdims=True) + NORM_EPS) * g


def _lane_mask(width, seg, h):
    lane = lax.broadcasted_iota(jnp.int32, (1, width), 1)
    return (lane // seg == h).astype(jnp.float32)


def _softmax_rows(s, sink=None):
    m = jnp.max(s, axis=-1, keepdims=True)
    if sink is not None:
        m = jnp.maximum(m, sink)
    e = jnp.exp(s - m)
    den = jnp.sum(e, axis=-1, keepdims=True)
    if sink is not None:
        den = den + jnp.exp(sink - m)
    return e / den


def _dft(x, cs, ss, cc_bd, sc_bd, scale):
    hi = dict(preferred_element_type=jnp.float32, precision=lax.Precision.HIGHEST)
    xc = jnp.dot(x, cc_bd, **hi)
    xs = jnp.dot(x, sc_bd, **hi)
    return (jnp.dot(cs, xc, **hi) - jnp.dot(ss, xs, **hi)) * scale


def _ctx_branches_kernel(sink_ref, mq_ref, fn_ref, sq_ref, skv_ref, mkv_ref, qn_ref, kvn_ref, wuq_ref, wukv_ref,
                         krsel_ref, ksel_ref, vsel_ref, cs_ref, ss_ref, ccbd_ref, scbd_ref,
                         oa_ref, ob_ref, od_ref, ckv_ref):
    bf16 = jnp.bfloat16
    qn = _rms(mq_ref[...], qn_ref[...]).astype(bf16)
    q = jnp.dot(qn, wuq_ref[...], **_F32)
    q_nope = q[:, :MLA_HEADS * MLA_NOPE]
    q_rope = q[:, MLA_HEADS * MLA_NOPE:]
    mkv = mkv_ref[...]
    ckv = _rms(mkv[:, :MLA_KV_LORA], kvn_ref[...])
    ckv_ref[...] = ckv
    kv = jnp.dot(ckv.astype(bf16), wukv_ref[...], **_F32)
    k_nope = kv[:, :MLA_HEADS * MLA_NOPE].astype(bf16)
    v = kv[:, MLA_HEADS * MLA_NOPE:].astype(bf16)
    k_rope4 = jnp.dot(mkv.astype(bf16), krsel_ref[...], **_F32).astype(bf16)
    o_a = jnp.zeros((SEQ, MLA_HEADS * MLA_V), jnp.float32)
    for h in range(MLA_HEADS):
        mn = _lane_mask(MLA_HEADS * MLA_NOPE, MLA_NOPE, h)
        mr = _lane_mask(MLA_HEADS * MLA_ROPE, MLA_ROPE, h)
        s = (lax.dot_general((q_nope * mn).astype(bf16), k_nope, _NT, **_F32)
             + lax.dot_general((q_rope * mr).astype(bf16), k_rope4, _NT, **_F32)) * MLA_SCALE
        p = _softmax_rows(s)
        o_a = o_a + jnp.dot(p.astype(bf16), v, **_F32) * _lane_mask(MLA_HEADS * MLA_V, MLA_V, h)
    oa_ref[...] = o_a
    ob_ref[...] = _dft(fn_ref[...], cs_ref[...], ss_ref[...], ccbd_ref[...], scbd_ref[...],
                       float((SEQ * FNET_CH) ** -0.5))
    sq = sq_ref[...]
    skv = skv_ref[...].astype(bf16)
    k4 = jnp.dot(skv, ksel_ref[...], **_F32).astype(bf16)
    v4 = jnp.dot(skv, vsel_ref[...], **_F32).astype(bf16)
    o_d = jnp.zeros((SEQ, SWA_HEADS * SWA_HEAD_DIM), jnp.float32)
    for h in range(SWA_HEADS):
        mh = _lane_mask(SWA_HEADS * SWA_HEAD_DIM, SWA_HEAD_DIM, h)
        s = lax.dot_general((sq * mh).astype(bf16), k4, _NT, **_F32) * SWA_SCALE
        p = _softmax_rows(s, sink_ref[h])
        o_d = o_d + jnp.dot(p.astype(bf16), v4, **_F32) * mh
    od_ref[...] = o_d


def dft_mats(n):
    k = np.arange(n)
    ang = 2.0 * np.pi * np.outer(k, k) / n
    return np.cos(ang).astype(np.float32), np.sin(ang).astype(np.float32)


def block_diag(m, reps):
    n = m.shape[0]
    out = np.zeros((n * reps, n * reps), m.dtype)
    for r in range(reps):
        out[r * n:(r + 1) * n, r * n:(r + 1) * n] = m
    return out


def selection(rows, cols, pairs):
    m = np.zeros((rows, cols), np.float32)
    for r, c in pairs:
        m[r, c] = 1.0
    return m


def ctx_branches(hs, n_seq, cols, sink, qn, kvn, wuq_p, wukv_p):
    bf16 = jnp.bfloat16
    cs, ss = dft_mats(SEQ)
    cc, sc = dft_mats(FNET_CH)
    krsel = selection(BRANCH_W, MLA_HEADS * MLA_ROPE,
                      [(MLA_KV_LORA + r, MLA_ROPE * h + r) for h in range(MLA_HEADS) for r in range(MLA_ROPE)])
    kv_w = SWA_KV_HEADS * SWA_HEAD_DIM
    ksel = selection(BRANCH_W, BRANCH_W, [((h // SWA_GROUP) * SWA_HEAD_DIM + r, h * SWA_HEAD_DIM + r)
                                          for h in range(SWA_HEADS) for r in range(SWA_HEAD_DIM)])
    vsel = selection(BRANCH_W, BRANCH_W, [(kv_w + (h // SWA_GROUP) * SWA_HEAD_DIM + r, h * SWA_HEAD_DIM + r)
                                          for h in range(SWA_HEADS) for r in range(SWA_HEAD_DIM)])
    consts = [jnp.asarray(krsel, bf16), jnp.asarray(ksel, bf16), jnp.asarray(vsel, bf16),
              jnp.asarray(cs), jnp.asarray(ss), jnp.asarray(block_diag(cc, FNET_GROUPS)),
              jnp.asarray(block_diag(sc, FNET_GROUPS))]
    col = lambda cb: pl.BlockSpec((SEQ, BRANCH_W), lambda b: (b, cb))
    full = lambda a: pl.BlockSpec(a.shape, lambda b: (0,) * a.ndim)
    weights = [qn, kvn, wuq_p, wukv_p] + consts
    out = lambda w: jax.ShapeDtypeStruct((n_seq * SEQ, w), jnp.float32)
    return pl.pallas_call(
        _ctx_branches_kernel,
        grid=(n_seq,),
        in_specs=[pl.BlockSpec(memory_space=pltpu.SMEM)] + [col(cb) for cb in cols] + [full(a) for a in weights],
        out_specs=[pl.BlockSpec((SEQ, BRANCH_W), lambda b: (b, 0))] * 3 + [pl.BlockSpec((SEQ, MLA_KV_LORA), lambda b: (b, 0))],
        out_shape=[out(BRANCH_W)] * 3 + [out(MLA_KV_LORA)],
        compiler_params=pltpu.CompilerParams(dimension_semantics=("arbitrary",), vmem_limit_bytes=VMEM_LIMIT),
        name="ctx_branches",
    )(sink, *[hs] * len(cols), *weights)


def regroup_uq(w_uq):
    w = w_uq.reshape(MLA_Q_LORA, MLA_HEADS, MLA_NOPE + MLA_ROPE)
    return jnp.concatenate([w[:, :, :MLA_NOPE].reshape(MLA_Q_LORA, -1), w[:, :, MLA_NOPE:].reshape(MLA_Q_LORA, -1)], axis=1)


def regroup_ukv(w_ukv):
    w = w_ukv.reshape(MLA_KV_LORA, MLA_HEADS, MLA_NOPE + MLA_V)
    return jnp.concatenate([w[:, :, :MLA_NOPE].reshape(MLA_KV_LORA, -1), w[:, :, MLA_NOPE:].reshape(MLA_KV_LORA, -1)], axis=1)


def rope_tables(n, dim, reps):
    half = dim // 2
    t = np.arange(n)
    freqs = (ROPE_THETA ** (-np.arange(0, half, 2, dtype=np.float32) / half)).astype(np.float32)
    ang = [(t // GRID_W).astype(np.float32)[:, None] * freqs[None, :], (t % GRID_W).astype(np.float32)[:, None] * freqs[None, :]]
    cos = np.concatenate([np.cos(a) for a in ang for _ in range(2)], axis=1)
    sin = np.concatenate([s * np.sin(a) for a in ang for s in (-1.0, 1.0)], axis=1)
    return (jnp.asarray(np.tile(cos, (1, reps)), jnp.float32), jnp.asarray(np.tile(sin, (1, reps)), jnp.float32))


def _rope(x, cos, sin_signed, quarter):
    w = x.shape[-1]
    lane = lax.broadcasted_iota(jnp.int32, x.shape, 1)
    swapped = jnp.where(lane % (2 * quarter) < quarter, pltpu.roll(x, w - quarter, 1), pltpu.roll(x, quarter, 1))
    return x * cos + swapped * sin_signed


def _lat_branches_kernel(sink_ref, mq_ref, sq_ref, fn_ref, skv_ref, mkv_ref, cckv_ref, ckr_ref, csk_ref, csv_ref,
                         qn_ref, kvn_ref, wuq_ref, wukv_ref, krsel_ref, ksel_ref, vsel_ref, ckrsel_ref, csel_ref,
                         cq32_ref, sq32_ref, ck32_ref, sk32_ref, cq64_ref, sq64_ref, ck64_ref, sk64_ref,
                         cs_ref, ss_ref, ccbd_ref, scbd_ref,
                         oa_ref, ob_ref, od_ref,
                         kn, vv, kr4, sk4, sv4, ckn, cvv, ckr4, csk4, csv4, xc, xs):
    bf16 = jnp.bfloat16
    j = pl.program_id(1)
    hi = dict(preferred_element_type=jnp.float32, precision=lax.Precision.HIGHEST)

    @pl.when(j == 0)
    def _():
        mkv = mkv_ref[...]
        ckv = _rms(mkv[:, :MLA_KV_LORA], kvn_ref[...])
        kv = jnp.dot(ckv.astype(bf16), wukv_ref[...], **_F32)
        kn[...] = kv[:, :MLA_HEADS * MLA_NOPE].astype(bf16)
        vv[...] = kv[:, MLA_HEADS * MLA_NOPE:].astype(bf16)
        kr = jnp.dot(mkv, krsel_ref[...], **hi)
        kr4[...] = _rope(kr, ck32_ref[...], sk32_ref[...], MLA_ROPE // 4).astype(bf16)
        skv = skv_ref[...]
        k4 = jnp.dot(skv, ksel_ref[...], **hi)
        sk4[...] = _rope(k4, ck64_ref[...], sk64_ref[...], SWA_HEAD_DIM // 4).astype(bf16)
        sv4[...] = jnp.dot(skv.astype(bf16), vsel_ref[...].astype(bf16), **_F32).astype(bf16)
        ckv_c = jnp.dot(cckv_ref[...].astype(bf16), wukv_ref[...], **_F32)
        ckn[...] = ckv_c[:, :MLA_HEADS * MLA_NOPE].astype(bf16)
        cvv[...] = ckv_c[:, MLA_HEADS * MLA_NOPE:].astype(bf16)
        ckr4[...] = jnp.dot(ckr_ref[...].astype(bf16), ckrsel_ref[...], **_F32).astype(bf16)
        for src, dst in ((csk_ref, csk4), (csv_ref, csv4)):
            acc = jnp.zeros(dst.shape, jnp.float32)
            for g in range(SWA_KV_HEADS):
                acc = acc + jnp.dot(src[g].astype(bf16), csel_ref[g], **_F32)
            dst[...] = acc.astype(bf16)
        x = fn_ref[...]
        xc[...] = jnp.dot(x, ccbd_ref[...], **hi)
        xs[...] = jnp.dot(x, scbd_ref[...], **hi)

    qn = _rms(mq_ref[...], qn_ref[...]).astype(bf16)
    q = jnp.dot(qn, wuq_ref[...], **_F32)
    q_nope = q[:, :MLA_HEADS * MLA_NOPE]
    q_rope = _rope(q[:, MLA_HEADS * MLA_NOPE:], cq32_ref[...], sq32_ref[...], MLA_ROPE // 4)
    o_a = jnp.zeros((LAT_QB, MLA_HEADS * MLA_V), jnp.float32)
    for h in range(MLA_HEADS):
        qh_n = (q_nope * _lane_mask(MLA_HEADS * MLA_NOPE, MLA_NOPE, h)).astype(bf16)
        qh_r = (q_rope * _lane_mask(MLA_HEADS * MLA_ROPE, MLA_ROPE, h)).astype(bf16)
        s_c = (lax.dot_general(qh_n, ckn[...], _NT, **_F32) + lax.dot_general(qh_r, ckr4[...], _NT, **_F32)) * MLA_SCALE
        s_l = (lax.dot_general(qh_n, kn[...], _NT, **_F32) + lax.dot_general(qh_r, kr4[...], _NT, **_F32)) * MLA_SCALE
        m = jnp.maximum(jnp.max(s_c, axis=-1, keepdims=True), jnp.max(s_l, axis=-1, keepdims=True))
        e_c, e_l = jnp.exp(s_c - m), jnp.exp(s_l - m)
        inv = 1.0 / (jnp.sum(e_c, axis=-1, keepdims=True) + jnp.sum(e_l, axis=-1, keepdims=True))
        o = jnp.dot((e_c * inv).astype(bf16), cvv[...], **_F32) + jnp.dot((e_l * inv).astype(bf16), vv[...], **_F32)
        o_a = o_a + o * _lane_mask(MLA_HEADS * MLA_V, MLA_V, h)
    oa_ref[...] = o_a
    ob_ref[...] = (jnp.dot(cs_ref[...], xc[...], **hi) - jnp.dot(ss_ref[...], xs[...], **hi)) * float((DEC_SEQ * FNET_CH) ** -0.5)
    sq = _rope(sq_ref[...], cq64_ref[...], sq64_ref[...], SWA_HEAD_DIM // 4)
    band = 3 * SWA_WINDOW
    start = pl.multiple_of(jnp.clip((j - 1) * LAT_QB, 0, DEC_SEQ - band), LAT_QB)
    kb = sk4[pl.ds(start, band), :]
    vb = sv4[pl.ds(start, band), :]
    qpos = j * LAT_QB + lax.broadcasted_iota(jnp.int32, (LAT_QB, band), 0)
    kpos = start + lax.broadcasted_iota(jnp.int32, (LAT_QB, band), 1)
    near = jnp.abs(kpos - qpos) <= SWA_WINDOW
    o_d = jnp.zeros((LAT_QB, SWA_HEADS * SWA_HEAD_DIM), jnp.float32)
    for h in range(SWA_HEADS):
        mh = _lane_mask(SWA_HEADS * SWA_HEAD_DIM, SWA_HEAD_DIM, h)
        qh = (sq * mh).astype(bf16)
        s_b = jnp.where(near, lax.dot_general(qh, kb, _NT, **_F32) * SWA_SCALE, NEG_INF)
        s_c = lax.dot_general(qh, csk4[...], _NT, **_F32) * SWA_SCALE
        sink = sink_ref[h]
        m = jnp.maximum(jnp.maximum(jnp.max(s_b, axis=-1, keepdims=True), jnp.max(s_c, axis=-1, keepdims=True)), sink)
        e_b, e_c = jnp.exp(s_b - m), jnp.exp(s_c - m)
        inv = 1.0 / (jnp.sum(e_b, axis=-1, keepdims=True) + jnp.sum(e_c, axis=-1, keepdims=True) + jnp.exp(sink - m))
        o = jnp.dot((e_b * inv).astype(bf16), vb, **_F32) + jnp.dot((e_c * inv).astype(bf16), csv4[...], **_F32)
        o_d = o_d + o * mh
    od_ref[...] = o_d


def lat_branches(hs, row0, n_seq, cols, sink, qn, kvn, wuq_p, wukv_p, c_ckv, c_krope, c_swa_k, c_swa_v):
    bf16 = jnp.bfloat16
    f32 = jnp.float32
    nq = DEC_SEQ // LAT_QB
    kv_w = SWA_KV_HEADS * SWA_HEAD_DIM
    krsel = selection(BRANCH_W, MLA_HEADS * MLA_ROPE,
                      [(MLA_KV_LORA + r, MLA_ROPE * h + r) for h in range(MLA_HEADS) for r in range(MLA_ROPE)])
    ksel = selection(BRANCH_W, BRANCH_W, [((h // SWA_GROUP) * SWA_HEAD_DIM + r, h * SWA_HEAD_DIM + r)
                                          for h in range(SWA_HEADS) for r in range(SWA_HEAD_DIM)])
    vsel = selection(BRANCH_W, BRANCH_W, [(kv_w + (h // SWA_GROUP) * SWA_HEAD_DIM + r, h * SWA_HEAD_DIM + r)
                                          for h in range(SWA_HEADS) for r in range(SWA_HEAD_DIM)])
    ckrsel = selection(MLA_ROPE, MLA_HEADS * MLA_ROPE, [(r, MLA_ROPE * h + r) for h in range(MLA_HEADS) for r in range(MLA_ROPE)])
    csel = np.stack([selection(SWA_HEAD_DIM, BRANCH_W, [(r, h * SWA_HEAD_DIM + r) for h in range(SWA_HEADS)
                                                        if h // SWA_GROUP == g for r in range(SWA_HEAD_DIM)])
                     for g in range(SWA_KV_HEADS)])
    c32 = rope_tables(DEC_SEQ, MLA_ROPE, MLA_HEADS)
    c64q = rope_tables(DEC_SEQ, SWA_HEAD_DIM, SWA_HEADS)
    t = jnp.arange(DEC_SEQ, dtype=jnp.int32)
    ang = ((t[:, None] * t[None, :]) % DEC_SEQ).astype(f32) * f32(2.0 * np.pi / DEC_SEQ)
    cs, ss = jnp.cos(ang), jnp.sin(ang)
    cc, sc = dft_mats(FNET_CH)
    rb = row0 // DEC_SEQ
    qb0 = row0 // LAT_QB
    qblk = lambda cb: pl.BlockSpec((LAT_QB, BRANCH_W), lambda b, j: (qb0 + b * nq + j, cb))
    sblk = lambda cb: pl.BlockSpec((DEC_SEQ, BRANCH_W), lambda b, j: (rb + b, cb), pipeline_mode=pl.Buffered(1))
    per_b = lambda a: pl.BlockSpec((None,) + a.shape[1:], lambda b, j: (b,) + (0,) * (a.ndim - 1))
    full = lambda a: pl.BlockSpec(a.shape, lambda b, j: (0,) * a.ndim)
    qtab = lambda a: pl.BlockSpec((LAT_QB, a.shape[1]), lambda b, j: (j, 0))
    consts = [qn, kvn, wuq_p, wukv_p, jnp.asarray(krsel), jnp.asarray(ksel), jnp.asarray(vsel),
              jnp.asarray(ckrsel, bf16), jnp.asarray(csel, bf16)]
    out = jax.ShapeDtypeStruct((n_seq * DEC_SEQ, BRANCH_W), f32)
    sc_bf = lambda r, w: pltpu.VMEM((r, w), bf16)
    return pl.pallas_call(
        _lat_branches_kernel,
        grid=(n_seq, nq),
        in_specs=[pl.BlockSpec(memory_space=pltpu.SMEM), qblk(cols[0]), qblk(cols[1]), sblk(cols[2]), sblk(cols[3]),
                  sblk(cols[4]), per_b(c_ckv), per_b(c_krope), per_b(c_swa_k), per_b(c_swa_v)]
                 + [full(a) for a in consts]
                 + [qtab(c32[0]), qtab(c32[1]), full(c32[0]), full(c32[1]),
                    qtab(c64q[0]), qtab(c64q[1]), full(c64q[0]), full(c64q[1]),
                    pl.BlockSpec((LAT_QB, DEC_SEQ), lambda b, j: (j, 0)), pl.BlockSpec((LAT_QB, DEC_SEQ), lambda b, j: (j, 0)),
                    full(jnp.zeros((BRANCH_W, BRANCH_W))), full(jnp.zeros((BRANCH_W, BRANCH_W)))],
        out_specs=[pl.BlockSpec((LAT_QB, BRANCH_W), lambda b, j: (b * nq + j, 0))] * 3,
        out_shape=[out] * 3,
        scratch_shapes=[sc_bf(DEC_SEQ, 256), sc_bf(DEC_SEQ, 256), sc_bf(DEC_SEQ, 128), sc_bf(DEC_SEQ, 256), sc_bf(DEC_SEQ, 256),
                        sc_bf(PAST_LEN, 256), sc_bf(PAST_LEN, 256), sc_bf(PAST_LEN, 128), sc_bf(PAST_LEN, 256), sc_bf(PAST_LEN, 256),
                        pltpu.VMEM((DEC_SEQ, 256), f32), pltpu.VMEM((DEC_SEQ, 256), f32)],
        compiler_params=pltpu.CompilerParams(dimension_semantics=("arbitrary", "arbitrary"), vmem_limit_bytes=VMEM_LIMIT),
        name="lat_branches",
    )(sink, hs, hs, hs, hs, hs, c_ckv, c_krope, c_swa_k, c_swa_v, *consts,
      c32[0], c32[1], c32[0], c32[1], c64q[0], c64q[1], c64q[0], c64q[1], cs, ss,
      jnp.asarray(block_diag(cc, FNET_GROUPS)), jnp.asarray(block_diag(sc, FNET_GROUPS)))


def _oddeven_merge_sort_pairs(n):
    pairs = []
    p = 1
    while p < n:
        k = p
        while k >= 1:
            for j in range(k % p, n - k, 2 * k):
                for i in range(min(k, n - j - k)):
                    if (i + j) // (p * 2) == (i + j + k) // (p * 2):
                        pairs.append((i + j, i + j + k))
            k //= 2
        p *= 2
    return pairs


def _bitonic_merge_pairs(n):
    pairs = []
    k = n // 2
    while k >= 1:
        pairs += [(i, i + k) for i in range(n) if (i & k) == 0]
        k //= 2
    return pairs


def _compare_exchange(v, pairs):
    for i, j in pairs:
        v[i], v[j] = jnp.maximum(v[i], v[j]), jnp.minimum(v[i], v[j])


def _merge_top(v, shifts):
    nv = len(v)
    dropped = None
    for r in shifts:
        other = [pltpu.roll(v[nv - 1 - i], SUBLANES - r, 0) for i in range(nv)]
        lo = [jnp.minimum(v[i], other[i]) for i in range(nv)]
        v = [jnp.maximum(v[i], other[i]) for i in range(nv)]
        _compare_exchange(v, _bitonic_merge_pairs(nv))
        while len(lo) > 1:
            lo = [jnp.maximum(lo[2 * i], lo[2 * i + 1]) for i in range(len(lo) // 2)]
        d = lo[0]
        if dropped is not None:
            d = jnp.maximum(d, jnp.maximum(dropped, pltpu.roll(dropped, SUBLANES - r, 0)))
        dropped = d
    return v, dropped


def _sorted_top(p):
    nv = PEER_N_KEYS // SUBLANES
    v = [p[SUBLANES * i:SUBLANES * (i + 1)] for i in range(nv)]
    _compare_exchange(v, _oddeven_merge_sort_pairs(nv))
    return _merge_top(v, (4, 2, 1))


def _rank16_17(c):
    v = list(c)
    _compare_exchange(v, _oddeven_merge_sort_pairs(len(v)))
    v = v + [pltpu.roll(t, SUBLANES - 4, 0) for t in reversed(v)]
    _compare_exchange(v, _bitonic_merge_pairs(len(v)))
    v, dropped = _merge_top(v, (2, 1))
    return v[PEER_TOPK - 1][0:1], dropped[0:1]


_INNER_PAIRS = [(i, j) for i in range(1, 8) for j in range(1, 8) if (i + 1) * (j + 1) <= PEER_TOPK + 1]


def _peer_route_kernel(x_ref, sh_ref, sc_ref, wq_ref, keys_ref, ut_ref, th_ref, e1_ref, p2_ref, e2_ref,
                       sc_scr, a_scr, b_scr, a1_scr, ai_scr, bi_scr):
    tt = x_ref.shape[0]
    u = _ln(x_ref[...]) * (1.0 + sc_ref[...]) + sh_ref[...]
    ut = u.T.astype(jnp.bfloat16)
    ut_ref[...] = ut
    qt = jnp.dot(wq_ref[...], ut, preferred_element_type=jnp.float32)
    for hp in range(2 * PEER_HEADS):
        q_hp = qt[hp * PEER_HALF:(hp + 1) * PEER_HALF, :].astype(jnp.bfloat16)
        sc_scr[hp] = jnp.dot(keys_ref[hp], q_hp, preferred_element_type=jnp.float32)

    for scr in (a_scr, b_scr, a1_scr, ai_scr, bi_scr):
        scr[...] = jnp.full(scr.shape, NEG_INF, jnp.float32)

    def per_head(h, carry):
        for tc in range(tt // LANES):
            ls = slice(tc * LANES, (tc + 1) * LANES)
            s1 = sc_scr[2 * h, :, ls]
            s2 = sc_scr[2 * h + 1, :, ls]
            p1 = s1 - jnp.max(s1, axis=0, keepdims=True)
            p2 = s2 - jnp.max(s2, axis=0, keepdims=True)
            a, a16 = _sorted_top(p1)
            b, b16 = _sorted_top(p2)
            for i in range(PEER_TOPK + 1):
                ai = a[i][0:1] if i < PEER_TOPK else a16[0:1]
                bi = b[i][0:1] if i < PEER_TOPK else b16[0:1]
                a_scr[i:i + 1, :] = ai
                b_scr[i:i + 1, :] = bi
                if i >= 1:
                    a1_scr[i - 1:i, :] = ai
                for r, (pi, pj) in enumerate(_INNER_PAIRS):
                    if pi == i:
                        ai_scr[r:r + 1, :] = ai
                    if pj == i:
                        bi_scr[r:r + 1, :] = bi
            bb = b_scr[...]
            cand_tiles = [a_scr[0:1, :] + bb[SUBLANES * k:SUBLANES * (k + 1)] for k in range(3)]
            cand_tiles += [a1_scr[SUBLANES * k:SUBLANES * (k + 1), :] + bb[0:1] for k in range(2)]
            cand_tiles += [ai_scr[SUBLANES * k:SUBLANES * (k + 1), :] + bi_scr[SUBLANES * k:SUBLANES * (k + 1), :]
                           for k in range(3)]
            c16, c17 = _rank16_17(cand_tiles)
            tau = 0.5 * (c16 + c17)
            cand = jnp.concatenate(cand_tiles, axis=0)
            z = jnp.sum(jnp.where(cand >= tau, jnp.exp(cand), 0.0), axis=0, keepdims=True)
            th_ref[h, :, :, ls] = (tau - p1).reshape(PEER_N_KEYS // SUBLANES, SUBLANES, LANES)
            e1_ref[h, :, :, ls] = (jnp.exp(p1) / z).reshape(PEER_N_KEYS // SUBLANES, SUBLANES, LANES)
            p2_ref[h, :, ls] = p2
            e2_ref[h, :, ls] = jnp.exp(p2)
        return carry

    lax.fori_loop(0, PEER_HEADS, per_head, 0)


def _peer_dense_kernel(ut_ref, th_ref, e1_ref, p2_ref, e2_ref, u_ref, vt_ref, o_ref, act_ref, gt_ref):
    j = pl.program_id(1)
    eb, tt = act_ref.shape
    n_i1 = eb // PEER_N_KEYS
    assert n_i1 == SUBLANES
    nch = tt // DENSE_CHUNK

    @pl.when(j == 0)
    def _():
        o_ref[...] = jnp.zeros_like(o_ref)

    def mm1(c):
        cs = slice(c * DENSE_CHUNK, (c + 1) * DENSE_CHUNK)
        act_ref[:, cs] = jnp.dot(u_ref[...], ut_ref[:, cs], preferred_element_type=jnp.float32)

    def mm2(c):
        cs = slice(c * DENSE_CHUNK, (c + 1) * DENSE_CHUNK)
        o_ref[:, cs] += jnp.dot(vt_ref[...], gt_ref[:, cs], preferred_element_type=jnp.float32)

    def weights(c):
        for half in range(DENSE_CHUNK // LANES):
            ls = slice(c * DENSE_CHUNK + half * LANES, c * DENSE_CHUNK + (half + 1) * LANES)
            for il in range(n_i1):
                w = None
                for h in range(PEER_HEADS):
                    th_rows = th_ref[h, j, :, ls]
                    e1_rows = e1_ref[h, j, :, ls]
                    wh = jnp.where(p2_ref[h, :, ls] >= th_rows[il:il + 1], e2_ref[h, :, ls] * e1_rows[il:il + 1], 0.0)
                    w = wh if w is None else w + wh
                rs = slice(il * PEER_N_KEYS, (il + 1) * PEER_N_KEYS)
                a = act_ref[rs, ls]
                gelu = 0.5 * a * (1.0 + lax.erf(a * np.float32(np.sqrt(0.5))))
                gt_ref[rs, ls] = (w * gelu).astype(jnp.bfloat16)

    mm1(0)
    for c in range(nch):
        if c + 1 < nch:
            mm1(c + 1)
        weights(c)
        if c >= 1:
            mm2(c - 1)
    mm2(nch - 1)


def _residual_ln_kernel(x_ref, g_ref, pt_ref, lng_ref, lnb_ref, y_ref):
    r = DEEPNORM_ALPHA * x_ref[...] + g_ref[...] * pt_ref[...].T
    y_ref[...] = _ln(r) * lng_ref[...] + lnb_ref[...]


def peer_layer(x, mod, tile_cond, wq_t, keys, u_tab, vt_tab, ln_g, ln_b, *, tt_route, tt, eb):
    t, d = x.shape
    nh = PEER_HEADS
    row = lambda k, tsz: pl.BlockSpec((None, None, 1, d), lambda i, *_: (tile_cond(i, tsz), 3 + k, 0, 0))
    rows_shape = (nh, PEER_N_KEYS // SUBLANES, SUBLANES, t)
    tile_shape = (nh, PEER_N_KEYS, t)
    ut, th, e1, p2, e2 = pl.pallas_call(
        _peer_route_kernel,
        grid=(t // tt_route,),
        in_specs=[pl.BlockSpec((tt_route, d), lambda i: (i, 0)), row(0, tt_route), row(1, tt_route),
                  pl.BlockSpec(wq_t.shape, lambda i: (0, 0)),
                  pl.BlockSpec(keys.shape, lambda i: (0, 0, 0))],
        out_specs=[pl.BlockSpec((d, tt_route), lambda i: (0, i)),
                   pl.BlockSpec(rows_shape[:3] + (tt_route,), lambda i: (0, 0, 0, i)),
                   pl.BlockSpec(rows_shape[:3] + (tt_route,), lambda i: (0, 0, 0, i)),
                   pl.BlockSpec(tile_shape[:2] + (tt_route,), lambda i: (0, 0, i)),
                   pl.BlockSpec(tile_shape[:2] + (tt_route,), lambda i: (0, 0, i))],
        out_shape=[jax.ShapeDtypeStruct((d, t), jnp.bfloat16),
                   jax.ShapeDtypeStruct(rows_shape, jnp.float32), jax.ShapeDtypeStruct(rows_shape, jnp.float32),
                   jax.ShapeDtypeStruct(tile_shape, jnp.float32), jax.ShapeDtypeStruct(tile_shape, jnp.float32)],
        scratch_shapes=[pltpu.VMEM((2 * nh, PEER_N_KEYS, tt_route), jnp.float32),
                        pltpu.VMEM((3 * SUBLANES, LANES), jnp.float32), pltpu.VMEM((3 * SUBLANES, LANES), jnp.float32),
                        pltpu.VMEM((2 * SUBLANES, LANES), jnp.float32), pltpu.VMEM((3 * SUBLANES, LANES), jnp.float32),
                        pltpu.VMEM((3 * SUBLANES, LANES), jnp.float32)],
        compiler_params=pltpu.CompilerParams(dimension_semantics=("arbitrary",), vmem_limit_bytes=VMEM_LIMIT),
        name="peer_route",
    )(x, mod, mod, wq_t, keys)

    ne = PEER_N_EXPERTS // eb
    once = dict(pipeline_mode=pl.Buffered(1))
    pt = pl.pallas_call(
        _peer_dense_kernel,
        grid=(t // tt, ne),
        in_specs=[pl.BlockSpec((d, tt), lambda i, j: (0, i)),
                  pl.BlockSpec(rows_shape[:3] + (tt,), lambda i, j: (0, 0, 0, i), **once),
                  pl.BlockSpec(rows_shape[:3] + (tt,), lambda i, j: (0, 0, 0, i), **once),
                  pl.BlockSpec(tile_shape[:2] + (tt,), lambda i, j: (0, 0, i), **once),
                  pl.BlockSpec(tile_shape[:2] + (tt,), lambda i, j: (0, 0, i), **once),
                  pl.BlockSpec((eb, d), lambda i, j: (j, 0)),
                  pl.BlockSpec((d, eb), lambda i, j: (0, j))],
        out_specs=pl.BlockSpec((d, tt), lambda i, j: (0, i)),
        out_shape=jax.ShapeDtypeStruct((d, t), jnp.float32),
        scratch_shapes=[pltpu.VMEM((eb, tt), jnp.float32), pltpu.VMEM((eb, tt), jnp.bfloat16)],
        compiler_params=pltpu.CompilerParams(dimension_semantics=("arbitrary", "arbitrary"),
                                             vmem_limit_bytes=VMEM_LIMIT),
        name="peer_dense",
    )(ut, th, e1, p2, e2, u_tab, vt_tab)

    tl = PEER_ROUTE_TILE
    return pl.pallas_call(
        _residual_ln_kernel,
        grid=(t // tl,),
        in_specs=[pl.BlockSpec((tl, d), lambda i: (i, 0)), row(2, tl), pl.BlockSpec((d, tl), lambda i: (0, i)),
                  pl.BlockSpec((1, d), lambda i: (0, 0)), pl.BlockSpec((1, d), lambda i: (0, 0))],
        out_specs=pl.BlockSpec((tl, d), lambda i: (i, 0)),
        out_shape=jax.ShapeDtypeStruct((t, d), jnp.float32),
        compiler_params=pltpu.CompilerParams(dimension_semantics=("arbitrary",), vmem_limit_bytes=VMEM_LIMIT),
        name="peer_residual_ln",
    )(x, mod, pt, ln_g, ln_b)


def _tile_cond(i, tt):
    ctx_tiles = BATCH * SEQ // tt
    return jnp.where(i < ctx_tiles, 0, 1 + (i - ctx_tiles) // (DEC_SEQ // tt))


def kernel(x_prompt, x_sample, c, cache_mla_ckv, cache_mla_krope, cache_swa_k, cache_swa_v, state_gla,
           c_ctx, w_ada, b_ada, w_in, mla_q_norm, w_uq, mla_kv_norm, w_ukv,
           w_gla_a_fwd, b_gla_a_fwd, w_gla_a_bwd, b_gla_a_bwd, gla_norm, swa_sink,
           w_branch, w_out, ln1_g, ln1_b, ln2_g, ln2_b, w_peer_q, peer_keys, peer_u, peer_v):
    bf16 = jnp.bfloat16
    n_ctx, n_lat = BATCH * SEQ, DEC_BATCH * DEC_SEQ
    n_tok = n_ctx + n_lat
    conds = jnp.concatenate([c_ctx[None, :], c], axis=0)
    x = jnp.concatenate([x_prompt.reshape(n_ctx, D_MODEL), x_sample.reshape(n_lat, D_MODEL)], axis=0)
    src = dict(zip(IN_NAMES, (0,) + IN_OFFSETS))
    seg_args = (n_ctx // GLA_SEG, SEQ // GLA_SEG, DEC_SEQ // GLA_SEG)
    lane_block = lambda name, width: HS_OFFSET[name] // width
    gla_cols = (lane_block('gla_q', HK), lane_block('gla_k', HK), lane_block('gla_v', HV),
                HS_OFFSET['gla_af'] // LANES, lane_block('gla_g', HV))
    z_row = lambda name: HS_OFFSET[name] - gla_cols[3] * LANES
    ctx_states = []
    for l in range(DEPTH):
        mod = (jax.nn.silu(conds) @ w_ada[l] + b_ada[l]).reshape(1 + DEC_BATCH, 6, 1, D_MODEL)
        w_parts = jnp.concatenate([w_in[l][:, src[n]:src[n] + _widths[n]] for n in HS_ORDER], axis=1)
        w_parts = jnp.pad(w_parts, ((0, 0), (0, PARTS_WIDTH - GATES_OFFSET))).astype(bf16)
        hs = in_proj(x, mod, _tile_cond, w_parts, tt=512)
        blk = lambda name: HS_OFFSET[name] // BRANCH_W
        mla_w = (mla_q_norm[l][None, :], mla_kv_norm[l][None, :], regroup_uq(w_uq[l]).astype(bf16),
                 regroup_ukv(w_ukv[l]).astype(bf16))
        a_ctx, b_ctx, d_ctx, ckv = ctx_branches(
            hs, BATCH, (blk('mla_q'), blk('fnet'), blk('swa_q'), blk('swa_k'), blk('mla_kv')), swa_sink[l], *mla_w)
        a_lat, b_lat, d_lat = lat_branches(
            hs, n_ctx, DEC_BATCH, (blk('mla_q'), blk('swa_q'), blk('fnet'), blk('swa_k'), blk('mla_kv')), swa_sink[l],
            *mla_w, cache_mla_ckv[:, l], cache_mla_krope[:, l], cache_swa_k[:, l], cache_swa_v[:, l])
        ctx_part = lambda name, lo, hi: hs[:n_ctx, HS_OFFSET[name] + lo:HS_OFFSET[name] + hi]
        kv_heads = lambda name: (ctx_part(name, 0, SWA_KV_HEADS * SWA_HEAD_DIM)
                                 .reshape(BATCH, SEQ, SWA_KV_HEADS, SWA_HEAD_DIM).transpose(0, 2, 1, 3))
        st = (ckv.reshape(BATCH, SEQ, MLA_KV_LORA),
              ctx_part('mla_kv', MLA_KV_LORA, MLA_KV_LORA + MLA_ROPE).reshape(BATCH, SEQ, MLA_ROPE),
              kv_heads('swa_k'), kv_heads('swa_v'))
        both = lambda u, v: jnp.concatenate([u, v], axis=0)
        w2 = lambda w, name: (jnp.zeros((LANES, HK), jnp.float32)
                              .at[z_row(name):z_row(name) + GLA_GATE_RANK].set(w).astype(bf16))
        zero = jnp.zeros((BATCH, GLA_HEADS, GLA_DK, GLA_DV), jnp.float32)
        o_c, s_f, s_b = gla_branch(
            hs.reshape(n_tok // GLA_SEG, GLA_SEG, PARTS_WIDTH), gla_cols,
            w2(w_gla_a_fwd[l], 'gla_af'), b_gla_a_fwd[l][None, :], w2(w_gla_a_bwd[l], 'gla_ab'), b_gla_a_bwd[l][None, :],
            jnp.tile(gla_norm[l], GLA_HEADS)[None, :],
            jnp.concatenate([zero, state_gla[:, l, 0]], axis=0), jnp.concatenate([zero, state_gla[:, l, 1]], axis=0),
            seg_args)
        ctx_states.append(st + (jnp.stack([s_f[:BATCH], s_b[:BATCH]], axis=1),))
        branches = (both(a_ctx, a_lat), both(b_ctx, b_lat), o_c.reshape(n_tok, BRANCH_W), both(d_ctx, d_lat))
        x1 = merge(x, mod, _tile_cond, branches, w_in[l][:, GATES_OFFSET:].astype(bf16), w_branch[l].astype(bf16),
                   w_out[l].astype(bf16), ln1_g[l][None, :], ln1_b[l][None, :], tt=256)
        x = peer_layer(
            x1, mod, _tile_cond,
            w_peer_q[l].T.astype(bf16),
            peer_keys[l].reshape(2 * PEER_HEADS, PEER_N_KEYS, PEER_HALF).astype(bf16),
            peer_u[l].astype(bf16), peer_v[l].T.astype(bf16),
            ln2_g[l][None, :], ln2_b[l][None, :],
            tt_route=PEER_ROUTE_TILE, tt=PEER_TOKEN_TILE, eb=PEER_EXPERT_BLOCK)

    h = x[:n_ctx].reshape(BATCH, SEQ, D_MODEL)
    z = x[n_ctx:].reshape(DEC_BATCH, DEC_SEQ, D_MODEL)
    new_mla_ckv = jnp.stack([st[0] for st in ctx_states], axis=1)
    new_mla_krope = jnp.stack([st[1] for st in ctx_states], axis=1)
    new_swa_k = jnp.stack([st[2] for st in ctx_states], axis=1)
    new_swa_v = jnp.stack([st[3] for st in ctx_states], axis=1)
    new_gla_state = jnp.stack([st[4] for st in ctx_states], axis=1)
    return (h, z, new_mla_ckv, new_mla_krope, new_swa_k, new_swa_v, new_gla_state)
```

```python
import functools

import jax
import jax.numpy as jnp
from jax import lax
import numpy as np
from jax.experimental import pallas as pl
from jax.experimental.pallas import tpu as pltpu

D_MODEL = 1024
BATCH = 32
SEQ = 256
DEPTH = 2
DEC_BATCH = 2
DEC_SEQ = 2048
PAST_LEN = 512

GRID_W = 64
N_BRANCH = 4
BRANCH_W = 256
MLA_HEADS = 4
MLA_Q_LORA = 256
MLA_KV_LORA = 128
MLA_NOPE = 64
MLA_ROPE = 32
MLA_V = 64
MLA_SCALE = (MLA_NOPE + MLA_ROPE) ** -0.5
FNET_GROUPS = 4
FNET_CH = BRANCH_W // FNET_GROUPS
GLA_HEADS = 4
GLA_DK = 32
GLA_DV = 64
GLA_GATE_RANK = 16
GLA_TAU = 16.0
GLA_CHUNK = 64
SWA_HEADS = 4
SWA_KV_HEADS = 2
SWA_GROUP = SWA_HEADS // SWA_KV_HEADS
SWA_HEAD_DIM = 64
SWA_WINDOW = 128
SWA_SCALE = SWA_HEAD_DIM ** -0.5
ATTN_BLOCK = 128
PEER_HEADS = 8
PEER_N_KEYS = 128
PEER_N_EXPERTS = PEER_N_KEYS * PEER_N_KEYS
PEER_KEY_DIM = 256
PEER_HALF = PEER_KEY_DIM // 2
PEER_TOPK = 16
PEER_ROUTE_TILE = 512
PEER_TOKEN_TILE = 1024
PEER_EXPERT_BLOCK = 8 * PEER_N_KEYS
DENSE_CHUNK = 256
GLA_SEG = 128
GLA_STEPS = 8
HK = GLA_HEADS * GLA_DK
HV = GLA_HEADS * GLA_DV
LAT_QB = SWA_WINDOW

ROPE_THETA = 10000.0
NORM_EPS = 1e-6
DEEPNORM_ALPHA = (2.0 * DEPTH) ** 0.25
DEEPNORM_BETA = (8.0 * DEPTH) ** -0.25

IN_SPLITS = (
    ('mla_q', MLA_Q_LORA),
    ('mla_kv', MLA_KV_LORA + MLA_ROPE),
    ('fnet', BRANCH_W),
    ('gla_q', GLA_HEADS * GLA_DK),
    ('gla_k', GLA_HEADS * GLA_DK),
    ('gla_v', GLA_HEADS * GLA_DV),
    ('gla_g', BRANCH_W),
    ('gla_af', GLA_GATE_RANK),
    ('gla_ab', GLA_GATE_RANK),
    ('swa_q', SWA_HEADS * SWA_HEAD_DIM),
    ('swa_k', SWA_KV_HEADS * SWA_HEAD_DIM),
    ('swa_v', SWA_KV_HEADS * SWA_HEAD_DIM),
    ('gates', N_BRANCH * D_MODEL),
)
IN_NAMES = tuple(n for n, _ in IN_SPLITS)
IN_OFFSETS = tuple(int(o) for o in np.cumsum([w for _, w in IN_SPLITS])[:-1])
IN_WIDTH = int(sum(w for _, w in IN_SPLITS))
GATES_OFFSET = IN_OFFSETS[-1]
HS_ORDER = ('mla_q', 'fnet', 'gla_q', 'gla_k', 'gla_v', 'gla_g', 'swa_q', 'swa_k', 'swa_v', 'mla_kv', 'gla_af', 'gla_ab')
_widths = dict(IN_SPLITS)
HS_OFFSET = {n: int(o) for n, o in zip(HS_ORDER, np.cumsum([0] + [_widths[n] for n in HS_ORDER])[:-1])}
PARTS_WIDTH = -(-GATES_OFFSET // 128) * 128

LANES = 128
SUBLANES = 8
NEG_INF = float('-inf')
VMEM_LIMIT = 56 * 1024 * 1024


def _ln(x):
    mu = jnp.mean(x, -1, keepdims=True)
    xc = x - mu
    var = jnp.mean(xc * xc, -1, keepdims=True)
    return xc * lax.rsqrt(var + NORM_EPS)


def _in_proj_kernel(x_ref, sh_ref, sc_ref, w_ref, o_ref):
    u = _ln(x_ref[...]) * (1.0 + sc_ref[...]) + sh_ref[...]
    o_ref[...] = jnp.dot(u.astype(jnp.bfloat16), w_ref[...], preferred_element_type=jnp.float32)


def in_proj(x, mod, tile_cond, w, *, tt):
    t, d = x.shape
    n = w.shape[1]
    row = lambda k: pl.BlockSpec((None, None, 1, d), lambda i: (tile_cond(i, tt), k, 0, 0))
    return pl.pallas_call(
        _in_proj_kernel,
        grid=(t // tt,),
        in_specs=[pl.BlockSpec((tt, d), lambda i: (i, 0)), row(0), row(1), pl.BlockSpec((d, n), lambda i: (0, 0))],
        out_specs=pl.BlockSpec((tt, n), lambda i: (i, 0)),
        out_shape=jax.ShapeDtypeStruct((t, n), jnp.float32),
        compiler_params=pltpu.CompilerParams(dimension_semantics=("arbitrary",), vmem_limit_bytes=VMEM_LIMIT),
        name="in_proj",
    )(x, mod, mod, w)


def _merge_kernel(x_ref, sh_ref, sc_ref, g_ref, ba_ref, bb_ref, bc_ref, bd_ref, wg_ref, wb_ref, wo_ref,
                  lng_ref, lnb_ref, y_ref):
    x = x_ref[...]
    u = (_ln(x) * (1.0 + sc_ref[...]) + sh_ref[...]).astype(jnp.bfloat16)
    acc = jnp.zeros(x.shape, jnp.float32)
    for b, br_ref in enumerate((ba_ref, bb_ref, bc_ref, bd_ref)):
        gate = jnp.dot(u, wg_ref[:, b * D_MODEL:(b + 1) * D_MODEL], preferred_element_type=jnp.float32)
        proj = jnp.dot(br_ref[...].astype(jnp.bfloat16), wb_ref[b], preferred_element_type=jnp.float32)
        acc = acc + jax.nn.sigmoid(gate) * proj
    mix = jnp.dot(acc.astype(jnp.bfloat16), wo_ref[...], preferred_element_type=jnp.float32)
    y_ref[...] = _ln(DEEPNORM_ALPHA * x + g_ref[...] * mix) * lng_ref[...] + lnb_ref[...]


def merge(x, mod, tile_cond, branches, w_gates, w_branch, w_out, ln_g, ln_b, *, tt):
    t, d = x.shape
    row = lambda k: pl.BlockSpec((None, None, 1, d), lambda i: (tile_cond(i, tt), k, 0, 0))
    full = lambda a: pl.BlockSpec(a.shape, lambda i: (0,) * a.ndim)
    return pl.pallas_call(
        _merge_kernel,
        grid=(t // tt,),
        in_specs=[pl.BlockSpec((tt, d), lambda i: (i, 0)), row(0), row(1), row(2),
                  *[pl.BlockSpec((tt, BRANCH_W), lambda i: (i, 0)) for _ in range(N_BRANCH)],
                  full(w_gates), full(w_branch), full(w_out), full(ln_g), full(ln_b)],
        out_specs=pl.BlockSpec((tt, d), lambda i: (i, 0)),
        out_shape=jax.ShapeDtypeStruct((t, d), jnp.float32),
        compiler_params=pltpu.CompilerParams(dimension_semantics=("arbitrary",), vmem_limit_bytes=VMEM_LIMIT),
        name="merge",
    )(x, mod, mod, mod, *branches, w_gates, w_branch, w_out, ln_g, ln_b)


def _gla_local_kernel(q_ref, k_ref, v_ref, z_ref, w2_ref, b2_ref, o_ref, qd_ref, st_ref, dl_ref,
                      s_scr, d_scr, stage, ta, tk, tq, *, reverse):
    i = pl.program_id(0)
    nseg = q_ref.shape[0]

    @pl.when(i == 0)
    def _():
        s_scr[...] = jnp.zeros_like(s_scr)
        d_scr[...] = jnp.ones_like(d_scr)
        stage[...] = jnp.zeros_like(stage)

    def to_lanes(x):
        stage[0:nseg, :] = x
        return stage[...].T

    def from_lanes(xt):
        return xt.T[0:nseg, :]

    steps = range(GLA_STEPS - 1, -1, -1) if reverse else range(GLA_STEPS)
    for j in steps:
        z = jnp.dot(z_ref[:, j, :].astype(jnp.bfloat16), w2_ref[...], preferred_element_type=jnp.float32) + b2_ref[...]
        ta[...] = to_lanes(jnp.exp(jax.nn.log_sigmoid(z) / GLA_TAU))
        tq[...] = to_lanes(q_ref[:, j, :] * (GLA_DK ** -0.5))
        tk[...] = to_lanes(k_ref[:, j, :])
        d = d_scr[...] * ta[...]
        d_scr[...] = d
        qd_ref[:, j, :] = from_lanes(tq[...] * d)
        vt = [to_lanes(v_ref[:, j, LANES * c:LANES * (c + 1)]) for c in range(HV // LANES)]
        outs = []
        for h in range(GLA_HEADS):
            off = (h * GLA_DV) % LANES
            vh = vt[(h * GLA_DV) // LANES][off:off + GLA_DV]

            def rows(g, acc, h=h, vh=vh):
                r0 = pl.multiple_of(h * GLA_DK + g * SUBLANES, SUBLANES)
                a8 = ta[pl.ds(r0, SUBLANES), :]
                k8 = tk[pl.ds(r0, SUBLANES), :]
                q8 = tq[pl.ds(r0, SUBLANES), :]
                for r in range(SUBLANES):
                    s_new = a8[r:r + 1] * s_scr[r0 + r] + k8[r:r + 1] * vh
                    s_scr[r0 + r] = s_new
                    acc = acc + q8[r:r + 1] * s_new
                return acc

            outs.append(lax.fori_loop(0, GLA_DK // SUBLANES, rows, jnp.zeros((GLA_DV, LANES), jnp.float32)))
        for c in range(HV // LANES):
            per = LANES // GLA_DV
            o_ref[:, j, LANES * c:LANES * (c + 1)] = from_lanes(jnp.concatenate(outs[per * c:per * (c + 1)], axis=0))

    @pl.when(i == pl.num_programs(0) - 1)
    def _():
        st_ref[...] = s_scr[...]
        dl_ref[...] = d_scr[...].T


def gla_local(hs3, w2pad, b2, cols, *, reverse):
    nseg, seg, _ = hs3.shape
    nb = seg // GLA_STEPS
    tmap = (lambda i: nb - 1 - i) if reverse else (lambda i: i)
    blk = lambda width, cb: pl.BlockSpec((nseg, GLA_STEPS, width), lambda i: (0, tmap(i), cb))
    t3 = lambda width: jax.ShapeDtypeStruct((nseg, seg, width), jnp.float32)
    return pl.pallas_call(
        functools.partial(_gla_local_kernel, reverse=reverse),
        grid=(nb,),
        in_specs=[blk(HK, cols[0]), blk(HK, cols[1]), blk(HV, cols[2]), blk(LANES, cols[3]),
                  pl.BlockSpec((LANES, HK), lambda i: (0, 0)), pl.BlockSpec((1, HK), lambda i: (0, 0))],
        out_specs=[pl.BlockSpec((nseg, GLA_STEPS, HV), lambda i: (0, tmap(i), 0)),
                   pl.BlockSpec((nseg, GLA_STEPS, HK), lambda i: (0, tmap(i), 0)),
                   pl.BlockSpec((HK, GLA_DV, LANES), lambda i: (0, 0, 0)),
                   pl.BlockSpec((LANES, HK), lambda i: (0, 0))],
        out_shape=[t3(HV), t3(HK), jax.ShapeDtypeStruct((HK, GLA_DV, LANES), jnp.float32),
                   jax.ShapeDtypeStruct((LANES, HK), jnp.float32)],
        scratch_shapes=[pltpu.VMEM((HK, GLA_DV, LANES), jnp.float32), pltpu.VMEM((HK, LANES), jnp.float32),
                        pltpu.VMEM((LANES, LANES), jnp.float32), pltpu.VMEM((HK, LANES), jnp.float32),
                        pltpu.VMEM((HK, LANES), jnp.float32), pltpu.VMEM((HK, LANES), jnp.float32)],
        compiler_params=pltpu.CompilerParams(dimension_semantics=("arbitrary",), vmem_limit_bytes=VMEM_LIMIT),
        name="gla_local_bwd" if reverse else "gla_local_fwd",
    )(hs3, hs3, hs3, hs3, w2pad, b2)


def _seg_seq(c, n_ctx_seg, ctx_per, lat_per):
    return jnp.where(c < n_ctx_seg, c // ctx_per, n_ctx_seg // ctx_per + (c - n_ctx_seg) // lat_per)


def _seg_pos(c, n_ctx_seg, ctx_per, lat_per):
    return (jnp.where(c < n_ctx_seg, c % ctx_per, (c - n_ctx_seg) % lat_per),
            jnp.where(c < n_ctx_seg, ctx_per, lat_per))


def _gla_chain_kernel(*refs, reverse, combine, seg_args):
    if combine:
        ol_ref, qd_ref, sl_ref, dl_ref, s0_ref, of_ref, gate_ref, gn_ref, o_ref, fin_ref, st = refs
    else:
        ol_ref, qd_ref, sl_ref, dl_ref, s0_ref, o_ref, fin_ref, st = refs
    i = pl.program_id(0)
    c = pl.num_programs(0) - 1 - i if reverse else i
    pos, per = _seg_pos(c, *seg_args)
    first = (pos == per - 1) if reverse else (pos == 0)

    @pl.when(first)
    def _():
        st[...] = s0_ref[...]

    s_in = st[...]
    o = ol_ref[...] + lax.dot_general(qd_ref[...].astype(jnp.bfloat16), s_in.astype(jnp.bfloat16),
                                      (((1,), (1,)), ((), ())), preferred_element_type=jnp.float32)
    row_head = lax.broadcasted_iota(jnp.int32, (HV, HK), 0) // GLA_DV
    col_head = lax.broadcasted_iota(jnp.int32, (HV, HK), 1) // GLA_DK
    s_loc = jnp.where(row_head == col_head, jnp.concatenate([sl_ref[...]] * GLA_HEADS, axis=0), 0.0)
    s_new = dl_ref[...] * s_in + s_loc
    st[...] = s_new
    fin_ref[...] = s_new
    if combine:
        x = o + of_ref[...]
        lane_head = lax.broadcasted_iota(jnp.int32, x.shape, 1) // GLA_DV
        x2 = x * x
        scale = jnp.zeros_like(x)
        for h in range(GLA_HEADS):
            ms = jnp.sum(jnp.where(lane_head == h, x2, 0.0), axis=-1, keepdims=True) * (1.0 / GLA_DV)
            scale = jnp.where(lane_head == h, lax.rsqrt(ms + NORM_EPS), scale)
        o = x * scale * gn_ref[...] * jax.nn.silu(gate_ref[...])
    o_ref[...] = o


def gla_chain(o_loc, qd, sl_t, dl, s0_t, seg_args, *, reverse, o_fwd=None, hs3=None, gate_col=None, gnorm=None):
    nseg, seg, _ = o_loc.shape
    n_seq = s0_t.shape[0]
    cmap = (lambda i: nseg - 1 - i) if reverse else (lambda i: i)
    seq = lambda i: _seg_seq(cmap(i), *seg_args)
    combine = o_fwd is not None
    in_specs = [pl.BlockSpec((None, seg, HV), lambda i: (cmap(i), 0, 0)),
                pl.BlockSpec((None, seg, HK), lambda i: (cmap(i), 0, 0)),
                pl.BlockSpec((None, GLA_DV, HK), lambda i: (cmap(i), 0, 0)),
                pl.BlockSpec((None, 1, HK), lambda i: (cmap(i), 0, 0)),
                pl.BlockSpec((None, HV, HK), lambda i: (seq(i), 0, 0))]
    args = [o_loc, qd, sl_t, dl.reshape(dl.shape[0], 1, HK), s0_t]
    if combine:
        in_specs += [pl.BlockSpec((None, seg, HV), lambda i: (cmap(i), 0, 0)),
                     pl.BlockSpec((None, seg, HV), lambda i: (cmap(i), 0, gate_col)),
                     pl.BlockSpec((1, HV), lambda i: (0, 0))]
        args += [o_fwd, hs3, gnorm]
    return pl.pallas_call(
        functools.partial(_gla_chain_kernel, reverse=reverse, combine=combine, seg_args=seg_args),
        grid=(nseg,),
        in_specs=in_specs,
        out_specs=[pl.BlockSpec((None, seg, HV), lambda i: (cmap(i), 0, 0)),
                   pl.BlockSpec((None, HV, HK), lambda i: (seq(i), 0, 0))],
        out_shape=[jax.ShapeDtypeStruct((nseg, seg, HV), jnp.float32),
                   jax.ShapeDtypeStruct((n_seq, HV, HK), jnp.float32)],
        scratch_shapes=[pltpu.VMEM((HV, HK), jnp.float32)],
        compiler_params=pltpu.CompilerParams(dimension_semantics=("arbitrary",), vmem_limit_bytes=VMEM_LIMIT),
        name="gla_chain_bwd" if reverse else "gla_chain_fwd",
    )(*args)


def _state_to_chain(s):
    n = s.shape[0]
    eye = jnp.eye(GLA_HEADS, dtype=s.dtype)
    return jnp.einsum('nhkv,hg->nhvgk', s, eye).reshape(n, HV, HK)


def _state_from_chain(f):
    n = f.shape[0]
    f5 = f.reshape(n, GLA_HEADS, GLA_DV, GLA_HEADS, GLA_DK)
    return jnp.stack([f5[:, h, :, h, :] for h in range(GLA_HEADS)], axis=1).transpose(0, 1, 3, 2)


def gla_branch(hs3, cols, w2f, b2f, w2b, b2b, gnorm, s0_fwd, s0_bwd, seg_args):
    nseg = hs3.shape[0]
    o_f = None
    for reverse, w2, b2, s0 in ((False, w2f, b2f, s0_fwd), (True, w2b, b2b, s0_bwd)):
        o_loc, qd, st, dl = gla_local(hs3, w2, b2, cols[:4], reverse=reverse)
        sl_t = jnp.transpose(st, (2, 1, 0))[:nseg]
        if not reverse:
            o_f, fin_f = gla_chain(o_loc, qd, sl_t, dl[:nseg], _state_to_chain(s0), seg_args, reverse=False)
        else:
            o, fin_b = gla_chain(o_loc, qd, sl_t, dl[:nseg], _state_to_chain(s0), seg_args, reverse=True,
                                 o_fwd=o_f, hs3=hs3, gate_col=cols[4], gnorm=gnorm)
    return o, _state_from_chain(fin_f), _state_from_chain(fin_b)


_NT = (((1,), (1,)), ((), ()))
_F32 = dict(preferred_element_type=jnp.float32)


def _rms(x, g):
    return x * lax.rsqrt(jnp.mean(x * x, -1, keepdims=True) + NORM_EPS) * g


def _lane_mask(width, seg, h):
    lane = lax.broadcasted_iota(jnp.int32, (1, width), 1)
    return (lane // seg == h).astype(jnp.float32)


def _softmax_rows(s, sink=None):
    m = jnp.max(s, axis=-1, keepdims=True)
    if sink is not None:
        m = jnp.maximum(m, sink)
    e = jnp.exp(s - m)
    den = jnp.sum(e, axis=-1, keepdims=True)
    if sink is not None:
        den = den + jnp.exp(sink - m)
    return e / den


def _dft(x, cs, ss, cc_bd, sc_bd, scale):
    hi = dict(preferred_element_type=jnp.float32, precision=lax.Precision.HIGHEST)
    xc = jnp.dot(x, cc_bd, **hi)
    xs = jnp.dot(x, sc_bd, **hi)
    return (jnp.dot(cs, xc, **hi) - jnp.dot(ss, xs, **hi)) * scale


def _ctx_branches_kernel(sink_ref, mq_ref, fn_ref, sq_ref, skv_ref, mkv_ref, qn_ref, kvn_ref, wuq_ref, wukv_ref,
                         krsel_ref, ksel_ref, vsel_ref, cs_ref, ss_ref, ccbd_ref, scbd_ref,
                         oa_ref, ob_ref, od_ref, ckv_ref):
    bf16 = jnp.bfloat16
    qn = _rms(mq_ref[...], qn_ref[...]).astype(bf16)
    q = jnp.dot(qn, wuq_ref[...], **_F32)
    q_nope = q[:, :MLA_HEADS * MLA_NOPE]
    q_rope = q[:, MLA_HEADS * MLA_NOPE:]
    mkv = mkv_ref[...]
    ckv = _rms(mkv[:, :MLA_KV_LORA], kvn_ref[...])
    ckv_ref[...] = ckv
    kv = jnp.dot(ckv.astype(bf16), wukv_ref[...], **_F32)
    k_nope = kv[:, :MLA_HEADS * MLA_NOPE].astype(bf16)
    v = kv[:, MLA_HEADS * MLA_NOPE:].astype(bf16)
    k_rope4 = jnp.dot(mkv.astype(bf16), krsel_ref[...], **_F32).astype(bf16)
    o_a = jnp.zeros((SEQ, MLA_HEADS * MLA_V), jnp.float32)
    for h in range(MLA_HEADS):
        mn = _lane_mask(MLA_HEADS * MLA_NOPE, MLA_NOPE, h)
        mr = _lane_mask(MLA_HEADS * MLA_ROPE, MLA_ROPE, h)
        s = (lax.dot_general((q_nope * mn).astype(bf16), k_nope, _NT, **_F32)
             + lax.dot_general((q_rope * mr).astype(bf16), k_rope4, _NT, **_F32)) * MLA_SCALE
        p = _softmax_rows(s)
        o_a = o_a + jnp.dot(p.astype(bf16), v, **_F32) * _lane_mask(MLA_HEADS * MLA_V, MLA_V, h)
    oa_ref[...] = o_a
    ob_ref[...] = _dft(fn_ref[...], cs_ref[...], ss_ref[...], ccbd_ref[...], scbd_ref[...],
                       float((SEQ * FNET_CH) ** -0.5))
    sq = sq_ref[...]
    skv = skv_ref[...].astype(bf16)
    k4 = jnp.dot(skv, ksel_ref[...], **_F32).astype(bf16)
    v4 = jnp.dot(skv, vsel_ref[...], **_F32).astype(bf16)
    o_d = jnp.zeros((SEQ, SWA_HEADS * SWA_HEAD_DIM), jnp.float32)
    for h in range(SWA_HEADS):
        mh = _lane_mask(SWA_HEADS * SWA_HEAD_DIM, SWA_HEAD_DIM, h)
        s = lax.dot_general((sq * mh).astype(bf16), k4, _NT, **_F32) * SWA_SCALE
        p = _softmax_rows(s, sink_ref[h])
        o_d = o_d + jnp.dot(p.astype(bf16), v4, **_F32) * mh
    od_ref[...] = o_d


def dft_mats(n):
    k = np.arange(n)
    ang = 2.0 * np.pi * np.outer(k, k) / n
    return np.cos(ang).astype(np.float32), np.sin(ang).astype(np.float32)


def block_diag(m, reps):
    n = m.shape[0]
    out = np.zeros((n * reps, n * reps), m.dtype)
    for r in range(reps):
        out[r * n:(r + 1) * n, r * n:(r + 1) * n] = m
    return out


def selection(rows, cols, pairs):
    m = np.zeros((rows, cols), np.float32)
    for r, c in pairs:
        m[r, c] = 1.0
    return m


def ctx_branches(hs, n_seq, cols, sink, qn, kvn, wuq_p, wukv_p):
    bf16 = jnp.bfloat16
    cs, ss = dft_mats(SEQ)
    cc, sc = dft_mats(FNET_CH)
    krsel = selection(BRANCH_W, MLA_HEADS * MLA_ROPE,
                      [(MLA_KV_LORA + r, MLA_ROPE * h + r) for h in range(MLA_HEADS) for r in range(MLA_ROPE)])
    kv_w = SWA_KV_HEADS * SWA_HEAD_DIM
    ksel = selection(BRANCH_W, BRANCH_W, [((h // SWA_GROUP) * SWA_HEAD_DIM + r, h * SWA_HEAD_DIM + r)
                                          for h in range(SWA_HEADS) for r in range(SWA_HEAD_DIM)])
    vsel = selection(BRANCH_W, BRANCH_W, [(kv_w + (h // SWA_GROUP) * SWA_HEAD_DIM + r, h * SWA_HEAD_DIM + r)
                                          for h in range(SWA_HEADS) for r in range(SWA_HEAD_DIM)])
    consts = [jnp.asarray(krsel, bf16), jnp.asarray(ksel, bf16), jnp.asarray(vsel, bf16),
              jnp.asarray(cs), jnp.asarray(ss), jnp.asarray(block_diag(cc, FNET_GROUPS)),
              jnp.asarray(block_diag(sc, FNET_GROUPS))]
    col = lambda cb: pl.BlockSpec((SEQ, BRANCH_W), lambda b: (b, cb))
    full = lambda a: pl.BlockSpec(a.shape, lambda b: (0,) * a.ndim)
    weights = [qn, kvn, wuq_p, wukv_p] + consts
    out = lambda rows, w: jax.ShapeDtypeStruct((rows, w), jnp.float32)
    return pl.pallas_call(
        _ctx_branches_kernel,
        grid=(n_seq,),
        in_specs=[pl.BlockSpec(memory_space=pltpu.SMEM)] + [col(cb) for cb in cols] + [full(a) for a in weights],
        out_specs=[pl.BlockSpec((SEQ, BRANCH_W), lambda b: (b, 0))] * 3 + [pl.BlockSpec((SEQ, MLA_KV_LORA), lambda b: (b, 0))],
        out_shape=[out(hs.shape[0], BRANCH_W)] * 3 + [out(n_seq * SEQ, MLA_KV_LORA)],
        compiler_params=pltpu.CompilerParams(dimension_semantics=("arbitrary",), vmem_limit_bytes=VMEM_LIMIT),
        name="ctx_branches",
    )(sink, *[hs] * len(cols), *weights)


def regroup_uq(w_uq):
    w = w_uq.reshape(MLA_Q_LORA, MLA_HEADS, MLA_NOPE + MLA_ROPE)
    return jnp.concatenate([w[:, :, :MLA_NOPE].reshape(MLA_Q_LORA, -1), w[:, :, MLA_NOPE:].reshape(MLA_Q_LORA, -1)], axis=1)


def regroup_ukv(w_ukv):
    w = w_ukv.reshape(MLA_KV_LORA, MLA_HEADS, MLA_NOPE + MLA_V)
    return jnp.concatenate([w[:, :, :MLA_NOPE].reshape(MLA_KV_LORA, -1), w[:, :, MLA_NOPE:].reshape(MLA_KV_LORA, -1)], axis=1)


def rope_tables(n, dim, reps):
    half = dim // 2
    t = np.arange(n)
    freqs = (ROPE_THETA ** (-np.arange(0, half, 2, dtype=np.float32) / half)).astype(np.float32)
    ang = [(t // GRID_W).astype(np.float32)[:, None] * freqs[None, :], (t % GRID_W).astype(np.float32)[:, None] * freqs[None, :]]
    cos = np.concatenate([np.cos(a) for a in ang for _ in range(2)], axis=1)
    sin = np.concatenate([s * np.sin(a) for a in ang for s in (-1.0, 1.0)], axis=1)
    return (jnp.asarray(np.tile(cos, (1, reps)), jnp.float32), jnp.asarray(np.tile(sin, (1, reps)), jnp.float32))


def _rope(x, cos, sin_signed, quarter):
    w = x.shape[-1]
    lane = lax.broadcasted_iota(jnp.int32, x.shape, 1)
    swapped = jnp.where(lane % (2 * quarter) < quarter, pltpu.roll(x, w - quarter, 1), pltpu.roll(x, quarter, 1))
    return x * cos + swapped * sin_signed


def _lat_branches_kernel(sink_ref, mq_ref, sq_ref, fn_ref, skv_ref, mkv_ref, cckv_ref, ckr_ref, csk_ref, csv_ref,
                         qn_ref, kvn_ref, wuq_ref, wukv_ref, krsel_ref, ksel_ref, vsel_ref, ckrsel_ref, csel_ref,
                         cq32_ref, sq32_ref, ck32_ref, sk32_ref, cq64_ref, sq64_ref, ck64_ref, sk64_ref,
                         cs_ref, ss_ref, ccbd_ref, scbd_ref, oa_in, ob_in, od_in,
                         oa_ref, ob_ref, od_ref,
                         kn, vv, kr4, sk4, sv4, ckn, cvv, ckr4, csk4, csv4, xc, xs):
    bf16 = jnp.bfloat16
    j = pl.program_id(1)
    hi = dict(preferred_element_type=jnp.float32, precision=lax.Precision.HIGHEST)

    @pl.when(j == 0)
    def _():
        mkv = mkv_ref[...]
        ckv = _rms(mkv[:, :MLA_KV_LORA], kvn_ref[...])
        kv = jnp.dot(ckv.astype(bf16), wukv_ref[...], **_F32)
        kn[...] = kv[:, :MLA_HEADS * MLA_NOPE].astype(bf16)
        vv[...] = kv[:, MLA_HEADS * MLA_NOPE:].astype(bf16)
        kr = jnp.dot(mkv, krsel_ref[...], **hi)
        kr4[...] = _rope(kr, ck32_ref[...], sk32_ref[...], MLA_ROPE // 4).astype(bf16)
        skv = skv_ref[...]
        k4 = jnp.dot(skv, ksel_ref[...], **hi)
        sk4[...] = _rope(k4, ck64_ref[...], sk64_ref[...], SWA_HEAD_DIM // 4).astype(bf16)
        sv4[...] = jnp.dot(skv.astype(bf16), vsel_ref[...].astype(bf16), **_F32).astype(bf16)
        ckv_c = jnp.dot(cckv_ref[...].astype(bf16), wukv_ref[...], **_F32)
        ckn[...] = ckv_c[:, :MLA_HEADS * MLA_NOPE].astype(bf16)
        cvv[...] = ckv_c[:, MLA_HEADS * MLA_NOPE:].astype(bf16)
        ckr4[...] = jnp.dot(ckr_ref[...].astype(bf16), ckrsel_ref[...], **_F32).astype(bf16)
        for src, dst in ((csk_ref, csk4), (csv_ref, csv4)):
            acc = jnp.zeros(dst.shape, jnp.float32)
            for g in range(SWA_KV_HEADS):
                acc = acc + jnp.dot(src[g].astype(bf16), csel_ref[g], **_F32)
            dst[...] = acc.astype(bf16)
        x = fn_ref[...]
        xc[...] = jnp.dot(x, ccbd_ref[...], **hi)
        xs[...] = jnp.dot(x, scbd_ref[...], **hi)

    qn = _rms(mq_ref[...], qn_ref[...]).astype(bf16)
    q = jnp.dot(qn, wuq_ref[...], **_F32)
    q_nope = q[:, :MLA_HEADS * MLA_NOPE]
    q_rope = _rope(q[:, MLA_HEADS * MLA_NOPE:], cq32_ref[...], sq32_ref[...], MLA_ROPE // 4)
    o_a = jnp.zeros((LAT_QB, MLA_HEADS * MLA_V), jnp.float32)
    for h in range(MLA_HEADS):
        qh_n = (q_nope * _lane_mask(MLA_HEADS * MLA_NOPE, MLA_NOPE, h)).astype(bf16)
        qh_r = (q_rope * _lane_mask(MLA_HEADS * MLA_ROPE, MLA_ROPE, h)).astype(bf16)
        s_c = (lax.dot_general(qh_n, ckn[...], _NT, **_F32) + lax.dot_general(qh_r, ckr4[...], _NT, **_F32)) * MLA_SCALE
        s_l = (lax.dot_general(qh_n, kn[...], _NT, **_F32) + lax.dot_general(qh_r, kr4[...], _NT, **_F32)) * MLA_SCALE
        m = jnp.maximum(jnp.max(s_c, axis=-1, keepdims=True), jnp.max(s_l, axis=-1, keepdims=True))
        e_c, e_l = jnp.exp(s_c - m), jnp.exp(s_l - m)
        inv = 1.0 / (jnp.sum(e_c, axis=-1, keepdims=True) + jnp.sum(e_l, axis=-1, keepdims=True))
        o = jnp.dot((e_c * inv).astype(bf16), cvv[...], **_F32) + jnp.dot((e_l * inv).astype(bf16), vv[...], **_F32)
        o_a = o_a + o * _lane_mask(MLA_HEADS * MLA_V, MLA_V, h)
    oa_ref[...] = o_a
    ob_ref[...] = (jnp.dot(cs_ref[...], xc[...], **hi) - jnp.dot(ss_ref[...], xs[...], **hi)) * float((DEC_SEQ * FNET_CH) ** -0.5)
    sq = _rope(sq_ref[...], cq64_ref[...], sq64_ref[...], SWA_HEAD_DIM // 4)
    band = 3 * SWA_WINDOW
    start = pl.multiple_of(jnp.clip((j - 1) * LAT_QB, 0, DEC_SEQ - band), LAT_QB)
    kb = sk4[pl.ds(start, band), :]
    vb = sv4[pl.ds(start, band), :]
    qpos = j * LAT_QB + lax.broadcasted_iota(jnp.int32, (LAT_QB, band), 0)
    kpos = start + lax.broadcasted_iota(jnp.int32, (LAT_QB, band), 1)
    near = jnp.abs(kpos - qpos) <= SWA_WINDOW
    o_d = jnp.zeros((LAT_QB, SWA_HEADS * SWA_HEAD_DIM), jnp.float32)
    for h in range(SWA_HEADS):
        mh = _lane_mask(SWA_HEADS * SWA_HEAD_DIM, SWA_HEAD_DIM, h)
        qh = (sq * mh).astype(bf16)
        s_b = jnp.where(near, lax.dot_general(qh, kb, _NT, **_F32) * SWA_SCALE, NEG_INF)
        s_c = lax.dot_general(qh, csk4[...], _NT, **_F32) * SWA_SCALE
        sink = sink_ref[h]
        m = jnp.maximum(jnp.maximum(jnp.max(s_b, axis=-1, keepdims=True), jnp.max(s_c, axis=-1, keepdims=True)), sink)
        e_b, e_c = jnp.exp(s_b - m), jnp.exp(s_c - m)
        inv = 1.0 / (jnp.sum(e_b, axis=-1, keepdims=True) + jnp.sum(e_c, axis=-1, keepdims=True) + jnp.exp(sink - m))
        o = jnp.dot((e_b * inv).astype(bf16), vb, **_F32) + jnp.dot((e_c * inv).astype(bf16), csv4[...], **_F32)
        o_d = o_d + o * mh
    od_ref[...] = o_d


def lat_branches(hs, row0, n_seq, cols, sink, qn, kvn, wuq_p, wukv_p, c_ckv, c_krope, c_swa_k, c_swa_v, branch_bufs):
    bf16 = jnp.bfloat16
    f32 = jnp.float32
    nq = DEC_SEQ // LAT_QB
    kv_w = SWA_KV_HEADS * SWA_HEAD_DIM
    krsel = selection(BRANCH_W, MLA_HEADS * MLA_ROPE,
                      [(MLA_KV_LORA + r, MLA_ROPE * h + r) for h in range(MLA_HEADS) for r in range(MLA_ROPE)])
    ksel = selection(BRANCH_W, BRANCH_W, [((h // SWA_GROUP) * SWA_HEAD_DIM + r, h * SWA_HEAD_DIM + r)
                                          for h in range(SWA_HEADS) for r in range(SWA_HEAD_DIM)])
    vsel = selection(BRANCH_W, BRANCH_W, [(kv_w + (h // SWA_GROUP) * SWA_HEAD_DIM + r, h * SWA_HEAD_DIM + r)
                                          for h in range(SWA_HEADS) for r in range(SWA_HEAD_DIM)])
    ckrsel = selection(MLA_ROPE, MLA_HEADS * MLA_ROPE, [(r, MLA_ROPE * h + r) for h in range(MLA_HEADS) for r in range(MLA_ROPE)])
    csel = np.stack([selection(SWA_HEAD_DIM, BRANCH_W, [(r, h * SWA_HEAD_DIM + r) for h in range(SWA_HEADS)
                                                        if h // SWA_GROUP == g for r in range(SWA_HEAD_DIM)])
                     for g in range(SWA_KV_HEADS)])
    c32 = rope_tables(DEC_SEQ, MLA_ROPE, MLA_HEADS)
    c64q = rope_tables(DEC_SEQ, SWA_HEAD_DIM, SWA_HEADS)
    t = jnp.arange(DEC_SEQ, dtype=jnp.int32)
    ang = ((t[:, None] * t[None, :]) % DEC_SEQ).astype(f32) * f32(2.0 * np.pi / DEC_SEQ)
    cs, ss = jnp.cos(ang), jnp.sin(ang)
    cc, sc = dft_mats(FNET_CH)
    rb = row0 // DEC_SEQ
    qb0 = row0 // LAT_QB
    qblk = lambda cb: pl.BlockSpec((LAT_QB, BRANCH_W), lambda b, j: (qb0 + b * nq + j, cb))
    sblk = lambda cb: pl.BlockSpec((DEC_SEQ, BRANCH_W), lambda b, j: (rb + b, cb), pipeline_mode=pl.Buffered(1))
    per_b = lambda a: pl.BlockSpec((None,) + a.shape[1:], lambda b, j: (b,) + (0,) * (a.ndim - 1))
    full = lambda a: pl.BlockSpec(a.shape, lambda b, j: (0,) * a.ndim)
    qtab = lambda a: pl.BlockSpec((LAT_QB, a.shape[1]), lambda b, j: (j, 0))
    consts = [qn, kvn, wuq_p, wukv_p, jnp.asarray(krsel), jnp.asarray(ksel), jnp.asarray(vsel),
              jnp.asarray(ckrsel, bf16), jnp.asarray(csel, bf16)]
    sc_bf = lambda r, w: pltpu.VMEM((r, w), bf16)
    args = [sink, hs, hs, hs, hs, hs, c_ckv, c_krope, c_swa_k, c_swa_v, *consts,
            c32[0], c32[1], c32[0], c32[1], c64q[0], c64q[1], c64q[0], c64q[1], cs, ss,
            jnp.asarray(block_diag(cc, FNET_GROUPS)), jnp.asarray(block_diag(sc, FNET_GROUPS))]
    return pl.pallas_call(
        _lat_branches_kernel,
        grid=(n_seq, nq),
        in_specs=[pl.BlockSpec(memory_space=pltpu.SMEM), qblk(cols[0]), qblk(cols[1]), sblk(cols[2]), sblk(cols[3]),
                  sblk(cols[4]), per_b(c_ckv), per_b(c_krope), per_b(c_swa_k), per_b(c_swa_v)]
                 + [full(a) for a in consts]
                 + [qtab(c32[0]), qtab(c32[1]), full(c32[0]), full(c32[1]),
                    qtab(c64q[0]), qtab(c64q[1]), full(c64q[0]), full(c64q[1]),
                    pl.BlockSpec((LAT_QB, DEC_SEQ), lambda b, j: (j, 0)), pl.BlockSpec((LAT_QB, DEC_SEQ), lambda b, j: (j, 0)),
                    full(jnp.zeros((BRANCH_W, BRANCH_W))), full(jnp.zeros((BRANCH_W, BRANCH_W)))]
                 + [pl.BlockSpec(memory_space=pl.ANY)] * len(branch_bufs),
        input_output_aliases={len(args) + k: k for k in range(len(branch_bufs))},
        out_specs=[pl.BlockSpec((LAT_QB, BRANCH_W), lambda b, j: (qb0 + b * nq + j, 0))] * len(branch_bufs),
        out_shape=[jax.ShapeDtypeStruct(a.shape, a.dtype) for a in branch_bufs],
        scratch_shapes=[sc_bf(DEC_SEQ, 256), sc_bf(DEC_SEQ, 256), sc_bf(DEC_SEQ, 128), sc_bf(DEC_SEQ, 256), sc_bf(DEC_SEQ, 256),
                        sc_bf(PAST_LEN, 256), sc_bf(PAST_LEN, 256), sc_bf(PAST_LEN, 128), sc_bf(PAST_LEN, 256), sc_bf(PAST_LEN, 256),
                        pltpu.VMEM((DEC_SEQ, 256), f32), pltpu.VMEM((DEC_SEQ, 256), f32)],
        compiler_params=pltpu.CompilerParams(dimension_semantics=("arbitrary", "arbitrary"), vmem_limit_bytes=VMEM_LIMIT),
        name="lat_branches",
    )(*args, *branch_bufs)


def _oddeven_merge_sort_pairs(n):
    pairs = []
    p = 1
    while p < n:
        k = p
        while k >= 1:
            for j in range(k % p, n - k, 2 * k):
                for i in range(min(k, n - j - k)):
                    if (i + j) // (p * 2) == (i + j + k) // (p * 2):
                        pairs.append((i + j, i + j + k))
            k //= 2
        p *= 2
    return pairs


def _bitonic_merge_pairs(n):
    pairs = []
    k = n // 2
    while k >= 1:
        pairs += [(i, i + k) for i in range(n) if (i & k) == 0]
        k //= 2
    return pairs


def _compare_exchange(v, pairs):
    for i, j in pairs:
        v[i], v[j] = jnp.maximum(v[i], v[j]), jnp.minimum(v[i], v[j])


def _merge_top(v, shifts):
    nv = len(v)
    dropped = None
    for r in shifts:
        other = [pltpu.roll(v[nv - 1 - i], SUBLANES - r, 0) for i in range(nv)]
        lo = [jnp.minimum(v[i], other[i]) for i in range(nv)]
        v = [jnp.maximum(v[i], other[i]) for i in range(nv)]
        _compare_exchange(v, _bitonic_merge_pairs(nv))
        while len(lo) > 1:
            lo = [jnp.maximum(lo[2 * i], lo[2 * i + 1]) for i in range(len(lo) // 2)]
        d = lo[0]
        if dropped is not None:
            d = jnp.maximum(d, jnp.maximum(dropped, pltpu.roll(dropped, SUBLANES - r, 0)))
        dropped = d
    return v, dropped


def _sorted_top(p):
    nv = PEER_N_KEYS // SUBLANES
    v = [p[SUBLANES * i:SUBLANES * (i + 1)] for i in range(nv)]
    _compare_exchange(v, _oddeven_merge_sort_pairs(nv))
    return _merge_top(v, (4, 2, 1))


def _rank16_17(c):
    v = list(c)
    _compare_exchange(v, _oddeven_merge_sort_pairs(len(v)))
    v = v + [pltpu.roll(t, SUBLANES - 4, 0) for t in reversed(v)]
    _compare_exchange(v, _bitonic_merge_pairs(len(v)))
    v, dropped = _merge_top(v, (2, 1))
    return v[PEER_TOPK - 1][0:1], dropped[0:1]


_INNER_PAIRS = [(i, j) for i in range(1, 8) for j in range(1, 8) if (i + 1) * (j + 1) <= PEER_TOPK + 1]


def _peer_route_kernel(x_ref, sh_ref, sc_ref, wq_ref, keys_ref, ut_ref, th_ref, e1_ref, p2_ref, e2_ref,
                       sc_scr, a_scr, b_scr, a1_scr, ai_scr, bi_scr):
    tt = x_ref.shape[0]
    u = _ln(x_ref[...]) * (1.0 + sc_ref[...]) + sh_ref[...]
    ut = u.T.astype(jnp.bfloat16)
    ut_ref[...] = ut
    qt = jnp.dot(wq_ref[...], ut, preferred_element_type=jnp.float32)
    for hp in range(2 * PEER_HEADS):
        q_hp = qt[hp * PEER_HALF:(hp + 1) * PEER_HALF, :].astype(jnp.bfloat16)
        sc_scr[hp] = jnp.dot(keys_ref[hp], q_hp, preferred_element_type=jnp.float32)

    for scr in (a_scr, b_scr, a1_scr, ai_scr, bi_scr):
        scr[...] = jnp.full(scr.shape, NEG_INF, jnp.float32)

    def per_head(h, carry):
        for tc in range(tt // LANES):
            ls = slice(tc * LANES, (tc + 1) * LANES)
            s1 = sc_scr[2 * h, :, ls]
            s2 = sc_scr[2 * h + 1, :, ls]
            p1 = s1 - jnp.max(s1, axis=0, keepdims=True)
            p2 = s2 - jnp.max(s2, axis=0, keepdims=True)
            a, a16 = _sorted_top(p1)
            b, b16 = _sorted_top(p2)
            for i in range(PEER_TOPK + 1):
                ai = a[i][0:1] if i < PEER_TOPK else a16[0:1]
                bi = b[i][0:1] if i < PEER_TOPK else b16[0:1]
                a_scr[i:i + 1, :] = ai
                b_scr[i:i + 1, :] = bi
                if i >= 1:
                    a1_scr[i - 1:i, :] = ai
                for r, (pi, pj) in enumerate(_INNER_PAIRS):
                    if pi == i:
                        ai_scr[r:r + 1, :] = ai
                    if pj == i:
                        bi_scr[r:r + 1, :] = bi
            bb = b_scr[...]
            cand_tiles = [a_scr[0:1, :] + bb[SUBLANES * k:SUBLANES * (k + 1)] for k in range(3)]
            cand_tiles += [a1_scr[SUBLANES * k:SUBLANES * (k + 1), :] + bb[0:1] for k in range(2)]
            cand_tiles += [ai_scr[SUBLANES * k:SUBLANES * (k + 1), :] + bi_scr[SUBLANES * k:SUBLANES * (k + 1), :]
                           for k in range(3)]
            c16, c17 = _rank16_17(cand_tiles)
            tau = 0.5 * (c16 + c17)
            cand = jnp.concatenate(cand_tiles, axis=0)
            z = jnp.sum(jnp.where(cand >= tau, jnp.exp(cand), 0.0), axis=0, keepdims=True)
            th_ref[h, :, :, ls] = (tau - p1).reshape(PEER_N_KEYS // SUBLANES, SUBLANES, LANES)
            e1_ref[h, :, :, ls] = (0.5 * jnp.exp(p1) / z).reshape(PEER_N_KEYS // SUBLANES, SUBLANES, LANES)
            p2_ref[h, :, ls] = p2
            e2_ref[h, :, ls] = jnp.exp(p2)
        return carry

    lax.fori_loop(0, PEER_HEADS, per_head, 0)


def _peer_dense_kernel(ut_ref, th_ref, e1_ref, p2_ref, e2_ref, u_ref, vt_ref, o_ref, act_ref, gt_ref):
    j = pl.program_id(1)
    eb, tt = act_ref.shape
    n_i1 = eb // PEER_N_KEYS
    assert n_i1 == SUBLANES
    nch = tt // DENSE_CHUNK

    def mm1(c):
        cs = slice(c * DENSE_CHUNK, (c + 1) * DENSE_CHUNK)
        act_ref[:, cs] = jnp.dot(u_ref[...], ut_ref[:, cs], preferred_element_type=jnp.float32)

    def mm2(c):
        cs = slice(c * DENSE_CHUNK, (c + 1) * DENSE_CHUNK)
        o_ref[:, cs] += jnp.dot(vt_ref[...], gt_ref[:, cs], preferred_element_type=jnp.float32)

    def weights(c):
        for half in range(DENSE_CHUNK // LANES):
            ls = slice(c * DENSE_CHUNK + half * LANES, c * DENSE_CHUNK + (half + 1) * LANES)
            for il in range(n_i1):
                w = None
                for h in range(PEER_HEADS):
                    th_rows = th_ref[h, j - 1, :, ls]
                    e1_rows = e1_ref[h, j - 1, :, ls]
                    wh = jnp.where(p2_ref[h, :, ls] >= th_rows[il:il + 1], e2_ref[h, :, ls] * e1_rows[il:il + 1], 0.0)
                    w = wh if w is None else w + wh
                rs = slice(il * PEER_N_KEYS, (il + 1) * PEER_N_KEYS)
                a = act_ref[rs, ls]
                gt_ref[rs, ls] = (w * (a * (1.0 + lax.erf(a * np.float32(np.sqrt(0.5)))))).astype(jnp.bfloat16)

    @pl.when(j == 0)
    def _():
        o_ref[...] = jnp.zeros_like(o_ref)
        for c in range(nch):
            mm1(c)

    @pl.when(j > 0)
    def _():
        for c in range(nch):
            weights(c)
            mm1(c)
            mm2(c)


def _residual_ln_kernel(x_ref, g_ref, pt_ref, lng_ref, lnb_ref, y_ref):
    r = DEEPNORM_ALPHA * x_ref[...] + g_ref[...] * pt_ref[...].T
    y_ref[...] = _ln(r) * lng_ref[...] + lnb_ref[...]


def peer_layer(x, mod, tile_cond, wq_t, keys, u_tab, vt_tab, ln_g, ln_b, *, tt_route, tt, eb):
    t, d = x.shape
    nh = PEER_HEADS
    row = lambda k, tsz: pl.BlockSpec((None, None, 1, d), lambda i, *_: (tile_cond(i, tsz), 3 + k, 0, 0))
    rows_shape = (nh, PEER_N_KEYS // SUBLANES, SUBLANES, t)
    tile_shape = (nh, PEER_N_KEYS, t)
    ut, th, e1, p2, e2 = pl.pallas_call(
        _peer_route_kernel,
        grid=(t // tt_route,),
        in_specs=[pl.BlockSpec((tt_route, d), lambda i: (i, 0)), row(0, tt_route), row(1, tt_route),
                  pl.BlockSpec(wq_t.shape, lambda i: (0, 0)),
                  pl.BlockSpec(keys.shape, lambda i: (0, 0, 0))],
        out_specs=[pl.BlockSpec((d, tt_route), lambda i: (0, i)),
                   pl.BlockSpec(rows_shape[:3] + (tt_route,), lambda i: (0, 0, 0, i)),
                   pl.BlockSpec(rows_shape[:3] + (tt_route,), lambda i: (0, 0, 0, i)),
                   pl.BlockSpec(tile_shape[:2] + (tt_route,), lambda i: (0, 0, i)),
                   pl.BlockSpec(tile_shape[:2] + (tt_route,), lambda i: (0, 0, i))],
        out_shape=[jax.ShapeDtypeStruct((d, t), jnp.bfloat16),
                   jax.ShapeDtypeStruct(rows_shape, jnp.float32), jax.ShapeDtypeStruct(rows_shape, jnp.float32),
                   jax.ShapeDtypeStruct(tile_shape, jnp.float32), jax.ShapeDtypeStruct(tile_shape, jnp.float32)],
        scratch_shapes=[pltpu.VMEM((2 * nh, PEER_N_KEYS, tt_route), jnp.float32),
                        pltpu.VMEM((3 * SUBLANES, LANES), jnp.float32), pltpu.VMEM((3 * SUBLANES, LANES), jnp.float32),
                        pltpu.VMEM((2 * SUBLANES, LANES), jnp.float32), pltpu.VMEM((3 * SUBLANES, LANES), jnp.float32),
                        pltpu.VMEM((3 * SUBLANES, LANES), jnp.float32)],
        compiler_params=pltpu.CompilerParams(dimension_semantics=("arbitrary",), vmem_limit_bytes=VMEM_LIMIT),
        name="peer_route",
    )(x, mod, mod, wq_t, keys)

    ne = PEER_N_EXPERTS // eb
    once = dict(pipeline_mode=pl.Buffered(1))
    pt = pl.pallas_call(
        _peer_dense_kernel,
        grid=(t // tt, ne + 1),
        in_specs=[pl.BlockSpec((d, tt), lambda i, j: (0, i)),
                  pl.BlockSpec(rows_shape[:3] + (tt,), lambda i, j: (0, 0, 0, i), **once),
                  pl.BlockSpec(rows_shape[:3] + (tt,), lambda i, j: (0, 0, 0, i), **once),
                  pl.BlockSpec(tile_shape[:2] + (tt,), lambda i, j: (0, 0, i), **once),
                  pl.BlockSpec(tile_shape[:2] + (tt,), lambda i, j: (0, 0, i), **once),
                  pl.BlockSpec((eb, d), lambda i, j: (jnp.minimum(j, ne - 1), 0)),
                  pl.BlockSpec((d, eb), lambda i, j: (0, jnp.maximum(j - 1, 0)))],
        out_specs=pl.BlockSpec((d, tt), lambda i, j: (0, i)),
        out_shape=jax.ShapeDtypeStruct((d, t), jnp.float32),
        scratch_shapes=[pltpu.VMEM((eb, tt), jnp.float32), pltpu.VMEM((eb, tt), jnp.bfloat16)],
        compiler_params=pltpu.CompilerParams(dimension_semantics=("arbitrary", "arbitrary"),
                                             vmem_limit_bytes=VMEM_LIMIT),
        name="peer_dense",
    )(ut, th, e1, p2, e2, u_tab, vt_tab)

    tl = PEER_ROUTE_TILE
    return pl.pallas_call(
        _residual_ln_kernel,
        grid=(t // tl,),
        in_specs=[pl.BlockSpec((tl, d), lambda i: (i, 0)), row(2, tl), pl.BlockSpec((d, tl), lambda i: (0, i)),
                  pl.BlockSpec((1, d), lambda i: (0, 0)), pl.BlockSpec((1, d), lambda i: (0, 0))],
        out_specs=pl.BlockSpec((tl, d), lambda i: (i, 0)),
        out_shape=jax.ShapeDtypeStruct((t, d), jnp.float32),
        compiler_params=pltpu.CompilerParams(dimension_semantics=("arbitrary",), vmem_limit_bytes=VMEM_LIMIT),
        name="peer_residual_ln",
    )(x, mod, pt, ln_g, ln_b)


def _tile_cond(i, tt):
    ctx_tiles = BATCH * SEQ // tt
    return jnp.where(i < ctx_tiles, 0, 1 + (i - ctx_tiles) // (DEC_SEQ // tt))


def kernel(x_prompt, x_sample, c, cache_mla_ckv, cache_mla_krope, cache_swa_k, cache_swa_v, state_gla,
           c_ctx, w_ada, b_ada, w_in, mla_q_norm, w_uq, mla_kv_norm, w_ukv,
           w_gla_a_fwd, b_gla_a_fwd, w_gla_a_bwd, b_gla_a_bwd, gla_norm, swa_sink,
           w_branch, w_out, ln1_g, ln1_b, ln2_g, ln2_b, w_peer_q, peer_keys, peer_u, peer_v):
    bf16 = jnp.bfloat16
    n_ctx, n_lat = BATCH * SEQ, DEC_BATCH * DEC_SEQ
    n_tok = n_ctx + n_lat
    conds = jnp.concatenate([c_ctx[None, :], c], axis=0)
    x = jnp.concatenate([x_prompt.reshape(n_ctx, D_MODEL), x_sample.reshape(n_lat, D_MODEL)], axis=0)
    src = dict(zip(IN_NAMES, (0,) + IN_OFFSETS))
    seg_args = (n_ctx // GLA_SEG, SEQ // GLA_SEG, DEC_SEQ // GLA_SEG)
    lane_block = lambda name, width: HS_OFFSET[name] // width
    gla_cols = (lane_block('gla_q', HK), lane_block('gla_k', HK), lane_block('gla_v', HV),
                HS_OFFSET['gla_af'] // LANES, lane_block('gla_g', HV))
    z_row = lambda name: HS_OFFSET[name] - gla_cols[3] * LANES
    ctx_states = []
    for l in range(DEPTH):
        mod = (jax.nn.silu(conds) @ w_ada[l] + b_ada[l]).reshape(1 + DEC_BATCH, 6, 1, D_MODEL)
        w_parts = jnp.concatenate([w_in[l][:, src[n]:src[n] + _widths[n]] for n in HS_ORDER], axis=1)
        w_parts = jnp.pad(w_parts, ((0, 0), (0, PARTS_WIDTH - GATES_OFFSET))).astype(bf16)
        hs = in_proj(x, mod, _tile_cond, w_parts, tt=512)
        blk = lambda name: HS_OFFSET[name] // BRANCH_W
        mla_w = (mla_q_norm[l][None, :], mla_kv_norm[l][None, :], regroup_uq(w_uq[l]).astype(bf16),
                 regroup_ukv(w_ukv[l]).astype(bf16))
        o_a, o_b, o_d, ckv = ctx_branches(
            hs, BATCH, (blk('mla_q'), blk('fnet'), blk('swa_q'), blk('swa_k'), blk('mla_kv')), swa_sink[l], *mla_w)
        o_a, o_b, o_d = lat_branches(
            hs, n_ctx, DEC_BATCH, (blk('mla_q'), blk('swa_q'), blk('fnet'), blk('swa_k'), blk('mla_kv')), swa_sink[l],
            *mla_w, cache_mla_ckv[:, l], cache_mla_krope[:, l], cache_swa_k[:, l], cache_swa_v[:, l], (o_a, o_b, o_d))
        ctx_part = lambda name, lo, hi: hs[:n_ctx, HS_OFFSET[name] + lo:HS_OFFSET[name] + hi]
        kv_heads = lambda name: (ctx_part(name, 0, SWA_KV_HEADS * SWA_HEAD_DIM)
                                 .reshape(BATCH, SEQ, SWA_KV_HEADS, SWA_HEAD_DIM).transpose(0, 2, 1, 3))
        st = (ckv.reshape(BATCH, SEQ, MLA_KV_LORA),
              ctx_part('mla_kv', MLA_KV_LORA, MLA_KV_LORA + MLA_ROPE).reshape(BATCH, SEQ, MLA_ROPE),
              kv_heads('swa_k'), kv_heads('swa_v'))
        w2 = lambda w, name: (jnp.zeros((LANES, HK), jnp.float32)
                              .at[z_row(name):z_row(name) + GLA_GATE_RANK].set(w).astype(bf16))
        zero = jnp.zeros((BATCH, GLA_HEADS, GLA_DK, GLA_DV), jnp.float32)
        o_c, s_f, s_b = gla_branch(
            hs.reshape(n_tok // GLA_SEG, GLA_SEG, PARTS_WIDTH), gla_cols,
            w2(w_gla_a_fwd[l], 'gla_af'), b_gla_a_fwd[l][None, :], w2(w_gla_a_bwd[l], 'gla_ab'), b_gla_a_bwd[l][None, :],
            jnp.tile(gla_norm[l], GLA_HEADS)[None, :],
            jnp.concatenate([zero, state_gla[:, l, 0]], axis=0), jnp.concatenate([zero, state_gla[:, l, 1]], axis=0),
            seg_args)
        ctx_states.append(st + (jnp.stack([s_f[:BATCH], s_b[:BATCH]], axis=1),))
        branches = (o_a, o_b, o_c.reshape(n_tok, BRANCH_W), o_d)
        x1 = merge(x, mod, _tile_cond, branches, w_in[l][:, GATES_OFFSET:].astype(bf16), w_branch[l].astype(bf16),
                   w_out[l].astype(bf16), ln1_g[l][None, :], ln1_b[l][None, :], tt=256)
        x = peer_layer(
            x1, mod, _tile_cond,
            w_peer_q[l].T.astype(bf16),
            peer_keys[l].reshape(2 * PEER_HEADS, PEER_N_KEYS, PEER_HALF).astype(bf16),
            peer_u[l].astype(bf16), peer_v[l].T.astype(bf16),
            ln2_g[l][None, :], ln2_b[l][None, :],
            tt_route=PEER_ROUTE_TILE, tt=PEER_TOKEN_TILE, eb=PEER_EXPERT_BLOCK)

    h = x[:n_ctx].reshape(BATCH, SEQ, D_MODEL)
    z = x[n_ctx:].reshape(DEC_BATCH, DEC_SEQ, D_MODEL)
    new_mla_ckv = jnp.stack([st[0] for st in ctx_states], axis=1)
    new_mla_krope = jnp.stack([st[1] for st in ctx_states], axis=1)
    new_swa_k = jnp.stack([st[2] for st in ctx_states], axis=1)
    new_swa_v = jnp.stack([st[3] for st in ctx_states], axis=1)
    new_gla_state = jnp.stack([st[4] for st in ctx_states], axis=1)
    return (h, z, new_mla_ckv, new_mla_krope, new_swa_k, new_swa_v, new_gla_state)
```

```python
import functools

import jax
import jax.numpy as jnp
from jax import lax
import numpy as np
from jax.experimental import pallas as pl
from jax.experimental.pallas import tpu as pltpu

D_MODEL = 1024
BATCH = 32
SEQ = 256
DEPTH = 2
DEC_BATCH = 2
DEC_SEQ = 2048
PAST_LEN = 512

GRID_W = 64
N_BRANCH = 4
BRANCH_W = 256
MLA_HEADS = 4
MLA_Q_LORA = 256
MLA_KV_LORA = 128
MLA_NOPE = 64
MLA_ROPE = 32
MLA_V = 64
MLA_SCALE = (MLA_NOPE + MLA_ROPE) ** -0.5
FNET_GROUPS = 4
FNET_CH = BRANCH_W // FNET_GROUPS
GLA_HEADS = 4
GLA_DK = 32
GLA_DV = 64
GLA_GATE_RANK = 16
GLA_TAU = 16.0
GLA_CHUNK = 64
SWA_HEADS = 4
SWA_KV_HEADS = 2
SWA_GROUP = SWA_HEADS // SWA_KV_HEADS
SWA_HEAD_DIM = 64
SWA_WINDOW = 128
SWA_SCALE = SWA_HEAD_DIM ** -0.5
ATTN_BLOCK = 128
PEER_HEADS = 8
PEER_N_KEYS = 128
PEER_N_EXPERTS = PEER_N_KEYS * PEER_N_KEYS
PEER_KEY_DIM = 256
PEER_HALF = PEER_KEY_DIM // 2
PEER_TOPK = 16
PEER_ROUTE_TILE = 512
PEER_TOKEN_TILE = 1024
PEER_EXPERT_BLOCK = 8 * PEER_N_KEYS
DENSE_CHUNK = 256
GLA_SEG = 128
GLA_STEPS = 8
HK = GLA_HEADS * GLA_DK
HV = GLA_HEADS * GLA_DV
LAT_QB = SWA_WINDOW

ROPE_THETA = 10000.0
NORM_EPS = 1e-6
DEEPNORM_ALPHA = (2.0 * DEPTH) ** 0.25
DEEPNORM_BETA = (8.0 * DEPTH) ** -0.25

IN_SPLITS = (
    ('mla_q', MLA_Q_LORA),
    ('mla_kv', MLA_KV_LORA + MLA_ROPE),
    ('fnet', BRANCH_W),
    ('gla_q', GLA_HEADS * GLA_DK),
    ('gla_k', GLA_HEADS * GLA_DK),
    ('gla_v', GLA_HEADS * GLA_DV),
    ('gla_g', BRANCH_W),
    ('gla_af', GLA_GATE_RANK),
    ('gla_ab', GLA_GATE_RANK),
    ('swa_q', SWA_HEADS * SWA_HEAD_DIM),
    ('swa_k', SWA_KV_HEADS * SWA_HEAD_DIM),
    ('swa_v', SWA_KV_HEADS * SWA_HEAD_DIM),
    ('gates', N_BRANCH * D_MODEL),
)
IN_NAMES = tuple(n for n, _ in IN_SPLITS)
IN_OFFSETS = tuple(int(o) for o in np.cumsum([w for _, w in IN_SPLITS])[:-1])
IN_WIDTH = int(sum(w for _, w in IN_SPLITS))
GATES_OFFSET = IN_OFFSETS[-1]
HS_ORDER = ('mla_q', 'fnet', 'gla_q', 'gla_k', 'gla_v', 'gla_g', 'swa_q', 'swa_k', 'swa_v', 'mla_kv', 'gla_af', 'gla_ab')
_widths = dict(IN_SPLITS)
HS_OFFSET = {n: int(o) for n, o in zip(HS_ORDER, np.cumsum([0] + [_widths[n] for n in HS_ORDER])[:-1])}
PARTS_WIDTH = -(-GATES_OFFSET // 128) * 128

LANES = 128
SUBLANES = 8
NEG_INF = float('-inf')
VMEM_LIMIT = 56 * 1024 * 1024


def _ln(x):
    mu = jnp.mean(x, -1, keepdims=True)
    xc = x - mu
    var = jnp.mean(xc * xc, -1, keepdims=True)
    return xc * lax.rsqrt(var + NORM_EPS)


def _in_proj_kernel(x_ref, sh_ref, sc_ref, w_ref, o_ref):
    u = _ln(x_ref[...]) * (1.0 + sc_ref[...]) + sh_ref[...]
    o_ref[...] = jnp.dot(u.astype(jnp.bfloat16), w_ref[...], preferred_element_type=jnp.float32)


def in_proj(x, mod, tile_cond, w, *, tt):
    t, d = x.shape
    n = w.shape[1]
    row = lambda k: pl.BlockSpec((None, None, 1, d), lambda i: (tile_cond(i, tt), k, 0, 0))
    return pl.pallas_call(
        _in_proj_kernel,
        grid=(t // tt,),
        in_specs=[pl.BlockSpec((tt, d), lambda i: (i, 0)), row(0), row(1), pl.BlockSpec((d, n), lambda i: (0, 0))],
        out_specs=pl.BlockSpec((tt, n), lambda i: (i, 0)),
        out_shape=jax.ShapeDtypeStruct((t, n), jnp.float32),
        compiler_params=pltpu.CompilerParams(dimension_semantics=("arbitrary",), vmem_limit_bytes=VMEM_LIMIT),
        name="in_proj",
    )(x, mod, mod, w)


def _merge_kernel(x_ref, sh_ref, sc_ref, g_ref, ba_ref, bb_ref, bc_ref, bd_ref, wg_ref, wb_ref, wo_ref,
                  lng_ref, lnb_ref, y_ref):
    x = x_ref[...]
    u = (_ln(x) * (1.0 + sc_ref[...]) + sh_ref[...]).astype(jnp.bfloat16)
    acc = jnp.zeros(x.shape, jnp.float32)
    for b, br_ref in enumerate((ba_ref, bb_ref, bc_ref, bd_ref)):
        gate = jnp.dot(u, wg_ref[:, b * D_MODEL:(b + 1) * D_MODEL], preferred_element_type=jnp.float32)
        proj = jnp.dot(br_ref[...].astype(jnp.bfloat16), wb_ref[b], preferred_element_type=jnp.float32)
        acc = acc + jax.nn.sigmoid(gate) * proj
    mix = jnp.dot(acc.astype(jnp.bfloat16), wo_ref[...], preferred_element_type=jnp.float32)
    y_ref[...] = _ln(DEEPNORM_ALPHA * x + g_ref[...] * mix) * lng_ref[...] + lnb_ref[...]


def merge(x, mod, tile_cond, branches, w_gates, w_branch, w_out, ln_g, ln_b, *, tt):
    t, d = x.shape
    row = lambda k: pl.BlockSpec((None, None, 1, d), lambda i: (tile_cond(i, tt), k, 0, 0))
    full = lambda a: pl.BlockSpec(a.shape, lambda i: (0,) * a.ndim)
    return pl.pallas_call(
        _merge_kernel,
        grid=(t // tt,),
        in_specs=[pl.BlockSpec((tt, d), lambda i: (i, 0)), row(0), row(1), row(2),
                  *[pl.BlockSpec((tt, BRANCH_W), lambda i: (i, 0)) for _ in range(N_BRANCH)],
                  full(w_gates), full(w_branch), full(w_out), full(ln_g), full(ln_b)],
        out_specs=pl.BlockSpec((tt, d), lambda i: (i, 0)),
        out_shape=jax.ShapeDtypeStruct((t, d), jnp.float32),
        compiler_params=pltpu.CompilerParams(dimension_semantics=("arbitrary",), vmem_limit_bytes=VMEM_LIMIT),
        name="merge",
    )(x, mod, mod, mod, *branches, w_gates, w_branch, w_out, ln_g, ln_b)


def _gla_local_kernel(q_ref, k_ref, v_ref, z_ref, w2_ref, b2_ref, o_ref, qd_ref, st_ref, dl_ref,
                      s_scr, d_scr, stage, ta, tk, tq, *, reverse):
    i = pl.program_id(0)
    nseg = q_ref.shape[0]

    @pl.when(i == 0)
    def _():
        s_scr[...] = jnp.zeros_like(s_scr)
        d_scr[...] = jnp.ones_like(d_scr)
        stage[...] = jnp.zeros_like(stage)

    def to_lanes(x):
        stage[0:nseg, :] = x
        return stage[...].T

    def from_lanes(xt):
        return xt.T[0:nseg, :]

    steps = range(GLA_STEPS - 1, -1, -1) if reverse else range(GLA_STEPS)
    for j in steps:
        z = jnp.dot(z_ref[:, j, :].astype(jnp.bfloat16), w2_ref[...], preferred_element_type=jnp.float32) + b2_ref[...]
        ta[...] = to_lanes(jnp.exp(jax.nn.log_sigmoid(z) / GLA_TAU))
        tq[...] = to_lanes(q_ref[:, j, :] * (GLA_DK ** -0.5))
        tk[...] = to_lanes(k_ref[:, j, :])
        d = d_scr[...] * ta[...]
        d_scr[...] = d
        qd_ref[:, j, :] = from_lanes(tq[...] * d)
        vt = [to_lanes(v_ref[:, j, LANES * c:LANES * (c + 1)]) for c in range(HV // LANES)]
        outs = []
        for h in range(GLA_HEADS):
            off = (h * GLA_DV) % LANES
            vh = vt[(h * GLA_DV) // LANES][off:off + GLA_DV]

            def rows(g, acc, h=h, vh=vh):
                r0 = pl.multiple_of(h * GLA_DK + g * SUBLANES, SUBLANES)
                a8 = ta[pl.ds(r0, SUBLANES), :]
                k8 = tk[pl.ds(r0, SUBLANES), :]
                q8 = tq[pl.ds(r0, SUBLANES), :]
                for r in range(SUBLANES):
                    s_new = a8[r:r + 1] * s_scr[r0 + r] + k8[r:r + 1] * vh
                    s_scr[r0 + r] = s_new
                    acc = acc + q8[r:r + 1] * s_new
                return acc

            outs.append(lax.fori_loop(0, GLA_DK // SUBLANES, rows, jnp.zeros((GLA_DV, LANES), jnp.float32)))
        for c in range(HV // LANES):
            per = LANES // GLA_DV
            o_ref[:, j, LANES * c:LANES * (c + 1)] = from_lanes(jnp.concatenate(outs[per * c:per * (c + 1)], axis=0))

    @pl.when(i == pl.num_programs(0) - 1)
    def _():
        st_ref[...] = s_scr[...]
        dl_ref[...] = d_scr[...].T


def gla_local(hs3, w2pad, b2, cols, *, reverse):
    nseg, seg, _ = hs3.shape
    nb = seg // GLA_STEPS
    tmap = (lambda i: nb - 1 - i) if reverse else (lambda i: i)
    blk = lambda width, cb: pl.BlockSpec((nseg, GLA_STEPS, width), lambda i: (0, tmap(i), cb))
    t3 = lambda width: jax.ShapeDtypeStruct((nseg, seg, width), jnp.float32)
    return pl.pallas_call(
        functools.partial(_gla_local_kernel, reverse=reverse),
        grid=(nb,),
        in_specs=[blk(HK, cols[0]), blk(HK, cols[1]), blk(HV, cols[2]), blk(LANES, cols[3]),
                  pl.BlockSpec((LANES, HK), lambda i: (0, 0)), pl.BlockSpec((1, HK), lambda i: (0, 0))],
        out_specs=[pl.BlockSpec((nseg, GLA_STEPS, HV), lambda i: (0, tmap(i), 0)),
                   pl.BlockSpec((nseg, GLA_STEPS, HK), lambda i: (0, tmap(i), 0)),
                   pl.BlockSpec((HK, GLA_DV, LANES), lambda i: (0, 0, 0)),
                   pl.BlockSpec((LANES, HK), lambda i: (0, 0))],
        out_shape=[t3(HV), t3(HK), jax.ShapeDtypeStruct((HK, GLA_DV, LANES), jnp.float32),
                   jax.ShapeDtypeStruct((LANES, HK), jnp.float32)],
        scratch_shapes=[pltpu.VMEM((HK, GLA_DV, LANES), jnp.float32), pltpu.VMEM((HK, LANES), jnp.float32),
                        pltpu.VMEM((LANES, LANES), jnp.float32), pltpu.VMEM((HK, LANES), jnp.float32),
                        pltpu.VMEM((HK, LANES), jnp.float32), pltpu.VMEM((HK, LANES), jnp.float32)],
        compiler_params=pltpu.CompilerParams(dimension_semantics=("arbitrary",), vmem_limit_bytes=VMEM_LIMIT),
        name="gla_local_bwd" if reverse else "gla_local_fwd",
    )(hs3, hs3, hs3, hs3, w2pad, b2)


def _seg_seq(c, n_ctx_seg, ctx_per, lat_per):
    return jnp.where(c < n_ctx_seg, c // ctx_per, n_ctx_seg // ctx_per + (c - n_ctx_seg) // lat_per)


def _seg_pos(c, n_ctx_seg, ctx_per, lat_per):
    return (jnp.where(c < n_ctx_seg, c % ctx_per, (c - n_ctx_seg) % lat_per),
            jnp.where(c < n_ctx_seg, ctx_per, lat_per))


def _gla_chain_kernel(*refs, reverse, combine, seg_args):
    if combine:
        ol_ref, qd_ref, sl_ref, dl_ref, s0_ref, of_ref, gate_ref, gn_ref, o_ref, fin_ref, st = refs
    else:
        ol_ref, qd_ref, sl_ref, dl_ref, s0_ref, o_ref, fin_ref, st = refs
    i = pl.program_id(0)
    c = pl.num_programs(0) - 1 - i if reverse else i
    pos, per = _seg_pos(c, *seg_args)
    first = (pos == per - 1) if reverse else (pos == 0)

    @pl.when(first)
    def _():
        st[...] = s0_ref[...]

    s_in = st[...]
    o = ol_ref[...] + lax.dot_general(qd_ref[...].astype(jnp.bfloat16), s_in.astype(jnp.bfloat16),
                                      (((1,), (1,)), ((), ())), preferred_element_type=jnp.float32)
    row_head = lax.broadcasted_iota(jnp.int32, (HV, HK), 0) // GLA_DV
    col_head = lax.broadcasted_iota(jnp.int32, (HV, HK), 1) // GLA_DK
    s_loc = jnp.where(row_head == col_head, jnp.concatenate([sl_ref[...]] * GLA_HEADS, axis=0), 0.0)
    s_new = dl_ref[...] * s_in + s_loc
    st[...] = s_new
    fin_ref[...] = s_new
    if combine:
        x = o + of_ref[...]
        lane_head = lax.broadcasted_iota(jnp.int32, x.shape, 1) // GLA_DV
        x2 = x * x
        scale = jnp.zeros_like(x)
        for h in range(GLA_HEADS):
            ms = jnp.sum(jnp.where(lane_head == h, x2, 0.0), axis=-1, keepdims=True) * (1.0 / GLA_DV)
            scale = jnp.where(lane_head == h, lax.rsqrt(ms + NORM_EPS), scale)
        o = x * scale * gn_ref[...] * jax.nn.silu(gate_ref[...])
    o_ref[...] = o


def gla_chain(o_loc, qd, sl_t, dl, s0_t, seg_args, *, reverse, o_fwd=None, hs3=None, gate_col=None, gnorm=None):
    nseg, seg, _ = o_loc.shape
    n_seq = s0_t.shape[0]
    cmap = (lambda i: nseg - 1 - i) if reverse else (lambda i: i)
    seq = lambda i: _seg_seq(cmap(i), *seg_args)
    combine = o_fwd is not None
    in_specs = [pl.BlockSpec((None, seg, HV), lambda i: (cmap(i), 0, 0)),
                pl.BlockSpec((None, seg, HK), lambda i: (cmap(i), 0, 0)),
                pl.BlockSpec((None, GLA_DV, HK), lambda i: (cmap(i), 0, 0)),
                pl.BlockSpec((None, 1, HK), lambda i: (cmap(i), 0, 0)),
                pl.BlockSpec((None, HV, HK), lambda i: (seq(i), 0, 0))]
    args = [o_loc, qd, sl_t, dl.reshape(dl.shape[0], 1, HK), s0_t]
    if combine:
        in_specs += [pl.BlockSpec((None, seg, HV), lambda i: (cmap(i), 0, 0)),
                     pl.BlockSpec((None, seg, HV), lambda i: (cmap(i), 0, gate_col)),
                     pl.BlockSpec((1, HV), lambda i: (0, 0))]
        args += [o_fwd, hs3, gnorm]
    return pl.pallas_call(
        functools.partial(_gla_chain_kernel, reverse=reverse, combine=combine, seg_args=seg_args),
        grid=(nseg,),
        in_specs=in_specs,
        out_specs=[pl.BlockSpec((None, seg, HV), lambda i: (cmap(i), 0, 0)),
                   pl.BlockSpec((None, HV, HK), lambda i: (seq(i), 0, 0))],
        out_shape=[jax.ShapeDtypeStruct((nseg, seg, HV), jnp.float32),
                   jax.ShapeDtypeStruct((n_seq, HV, HK), jnp.float32)],
        scratch_shapes=[pltpu.VMEM((HV, HK), jnp.float32)],
        compiler_params=pltpu.CompilerParams(dimension_semantics=("arbitrary",), vmem_limit_bytes=VMEM_LIMIT),
        name="gla_chain_bwd" if reverse else "gla_chain_fwd",
    )(*args)


def _state_to_chain(s):
    n = s.shape[0]
    eye = jnp.eye(GLA_HEADS, dtype=s.dtype)
    return jnp.einsum('nhkv,hg->nhvgk', s, eye).reshape(n, HV, HK)


def _state_from_chain(f):
    n = f.shape[0]
    f5 = f.reshape(n, GLA_HEADS, GLA_DV, GLA_HEADS, GLA_DK)
    return jnp.stack([f5[:, h, :, h, :] for h in range(GLA_HEADS)], axis=1).transpose(0, 1, 3, 2)


def gla_branch(hs3, cols, w2f, b2f, w2b, b2b, gnorm, s0_fwd, s0_bwd, seg_args):
    nseg = hs3.shape[0]
    o_f = None
    for reverse, w2, b2, s0 in ((False, w2f, b2f, s0_fwd), (True, w2b, b2b, s0_bwd)):
        o_loc, qd, st, dl = gla_local(hs3, w2, b2, cols[:4], reverse=reverse)
        sl_t = jnp.transpose(st, (2, 1, 0))[:nseg]
        if not reverse:
            o_f, fin_f = gla_chain(o_loc, qd, sl_t, dl[:nseg], _state_to_chain(s0), seg_args, reverse=False)
        else:
            o, fin_b = gla_chain(o_loc, qd, sl_t, dl[:nseg], _state_to_chain(s0), seg_args, reverse=True,
                                 o_fwd=o_f, hs3=hs3, gate_col=cols[4], gnorm=gnorm)
    return o, _state_from_chain(fin_f), _state_from_chain(fin_b)


_NT = (((1,), (1,)), ((), ()))
_F32 = dict(preferred_element_type=jnp.float32)


def _rms(x, g):
    return x * lax.rsqrt(jnp.mean(x * x, -1, keepdims=True) + NORM_EPS) * g


def _lane_mask(width, seg, h):
    lane = lax.broadcasted_iota(jnp.int32, (1, width), 1)
    return (lane // seg == h).astype(jnp.float32)


def _softmax_rows(s, sink=None):
    m = jnp.max(s, axis=-1, keepdims=True)
    if sink is not None:
        m = jnp.maximum(m, sink)
    e = jnp.exp(s - m)
    den = jnp.sum(e, axis=-1, keepdims=True)
    if sink is not None:
        den = den + jnp.exp(sink - m)
    return e / den


def _dft(x, cs, ss, cc_bd, sc_bd, scale):
    hi = dict(preferred_element_type=jnp.float32, precision=lax.Precision.HIGHEST)
    xc = jnp.dot(x, cc_bd, **hi)
    xs = jnp.dot(x, sc_bd, **hi)
    return (jnp.dot(cs, xc, **hi) - jnp.dot(ss, xs, **hi)) * scale


def _ctx_branches_kernel(sink_ref, mq_ref, fn_ref, sq_ref, skv_ref, mkv_ref, qn_ref, kvn_ref, wuq_ref, wukv_ref,
                         krsel_ref, ksel_ref, vsel_ref, cs_ref, ss_ref, ccbd_ref, scbd_ref,
                         oa_ref, ob_ref, od_ref, ckv_ref):
    bf16 = jnp.bfloat16
    qn = _rms(mq_ref[...], qn_ref[...]).astype(bf16)
    q = jnp.dot(qn, wuq_ref[...], **_F32)
    q_nope = q[:, :MLA_HEADS * MLA_NOPE]
    q_rope = q[:, MLA_HEADS * MLA_NOPE:]
    mkv = mkv_ref[...]
    ckv = _rms(mkv[:, :MLA_KV_LORA], kvn_ref[...])
    ckv_ref[...] = ckv
    kv = jnp.dot(ckv.astype(bf16), wukv_ref[...], **_F32)
    k_nope = kv[:, :MLA_HEADS * MLA_NOPE].astype(bf16)
    v = kv[:, MLA_HEADS * MLA_NOPE:].astype(bf16)
    k_rope4 = jnp.dot(mkv.astype(bf16), krsel_ref[...], **_F32).astype(bf16)
    o_a = jnp.zeros((SEQ, MLA_HEADS * MLA_V), jnp.float32)
    for h in range(MLA_HEADS):
        mn = _lane_mask(MLA_HEADS * MLA_NOPE, MLA_NOPE, h)
        mr = _lane_mask(MLA_HEADS * MLA_ROPE, MLA_ROPE, h)
        s = (lax.dot_general((q_nope * mn).astype(bf16), k_nope, _NT, **_F32)
             + lax.dot_general((q_rope * mr).astype(bf16), k_rope4, _NT, **_F32)) * MLA_SCALE
        p = _softmax_rows(s)
        o_a = o_a + jnp.dot(p.astype(bf16), v, **_F32) * _lane_mask(MLA_HEADS * MLA_V, MLA_V, h)
    oa_ref[...] = o_a
    ob_ref[...] = _dft(fn_ref[...], cs_ref[...], ss_ref[...], ccbd_ref[...], scbd_ref[...],
                       float((SEQ * FNET_CH) ** -0.5))
    sq = sq_ref[...]
    skv = skv_ref[...].astype(bf16)
    k4 = jnp.dot(skv, ksel_ref[...], **_F32).astype(bf16)
    v4 = jnp.dot(skv, vsel_ref[...], **_F32).astype(bf16)
    o_d = jnp.zeros((SEQ, SWA_HEADS * SWA_HEAD_DIM), jnp.float32)
    for h in range(SWA_HEADS):
        mh = _lane_mask(SWA_HEADS * SWA_HEAD_DIM, SWA_HEAD_DIM, h)
        s = lax.dot_general((sq * mh).astype(bf16), k4, _NT, **_F32) * SWA_SCALE
        p = _softmax_rows(s, sink_ref[h])
        o_d = o_d + jnp.dot(p.astype(bf16), v4, **_F32) * mh
    od_ref[...] = o_d


def dft_mats(n):
    k = np.arange(n)
    ang = 2.0 * np.pi * np.outer(k, k) / n
    return np.cos(ang).astype(np.float32), np.sin(ang).astype(np.float32)


def block_diag(m, reps):
    n = m.shape[0]
    out = np.zeros((n * reps, n * reps), m.dtype)
    for r in range(reps):
        out[r * n:(r + 1) * n, r * n:(r + 1) * n] = m
    return out


def selection(rows, cols, pairs):
    m = np.zeros((rows, cols), np.float32)
    for r, c in pairs:
        m[r, c] = 1.0
    return m


def ctx_branches(hs, n_seq, cols, sink, qn, kvn, wuq_p, wukv_p):
    bf16 = jnp.bfloat16
    cs, ss = dft_mats(SEQ)
    cc, sc = dft_mats(FNET_CH)
    krsel = selection(BRANCH_W, MLA_HEADS * MLA_ROPE,
                      [(MLA_KV_LORA + r, MLA_ROPE * h + r) for h in range(MLA_HEADS) for r in range(MLA_ROPE)])
    kv_w = SWA_KV_HEADS * SWA_HEAD_DIM
    ksel = selection(BRANCH_W, BRANCH_W, [((h // SWA_GROUP) * SWA_HEAD_DIM + r, h * SWA_HEAD_DIM + r)
                                          for h in range(SWA_HEADS) for r in range(SWA_HEAD_DIM)])
    vsel = selection(BRANCH_W, BRANCH_W, [(kv_w + (h // SWA_GROUP) * SWA_HEAD_DIM + r, h * SWA_HEAD_DIM + r)
                                          for h in range(SWA_HEADS) for r in range(SWA_HEAD_DIM)])
    consts = [jnp.asarray(krsel, bf16), jnp.asarray(ksel, bf16), jnp.asarray(vsel, bf16),
              jnp.asarray(cs), jnp.asarray(ss), jnp.asarray(block_diag(cc, FNET_GROUPS)),
              jnp.asarray(block_diag(sc, FNET_GROUPS))]
    col = lambda cb: pl.BlockSpec((SEQ, BRANCH_W), lambda b: (b, cb))
    full = lambda a: pl.BlockSpec(a.shape, lambda b: (0,) * a.ndim)
    weights = [qn, kvn, wuq_p, wukv_p] + consts
    out = lambda rows, w: jax.ShapeDtypeStruct((rows, w), jnp.float32)
    return pl.pallas_call(
        _ctx_branches_kernel,
        grid=(n_seq,),
        in_specs=[pl.BlockSpec(memory_space=pltpu.SMEM)] + [col(cb) for cb in cols] + [full(a) for a in weights],
        out_specs=[pl.BlockSpec((SEQ, BRANCH_W), lambda b: (b, 0))] * 3 + [pl.BlockSpec((SEQ, MLA_KV_LORA), lambda b: (b, 0))],
        out_shape=[out(hs.shape[0], BRANCH_W)] * 3 + [out(n_seq * SEQ, MLA_KV_LORA)],
        compiler_params=pltpu.CompilerParams(dimension_semantics=("arbitrary",), vmem_limit_bytes=VMEM_LIMIT),
        name="ctx_branches",
    )(sink, *[hs] * len(cols), *weights)


def regroup_uq(w_uq):
    w = w_uq.reshape(MLA_Q_LORA, MLA_HEADS, MLA_NOPE + MLA_ROPE)
    return jnp.concatenate([w[:, :, :MLA_NOPE].reshape(MLA_Q_LORA, -1), w[:, :, MLA_NOPE:].reshape(MLA_Q_LORA, -1)], axis=1)


def regroup_ukv(w_ukv):
    w = w_ukv.reshape(MLA_KV_LORA, MLA_HEADS, MLA_NOPE + MLA_V)
    return jnp.concatenate([w[:, :, :MLA_NOPE].reshape(MLA_KV_LORA, -1), w[:, :, MLA_NOPE:].reshape(MLA_KV_LORA, -1)], axis=1)


def rope_tables(n, dim, reps):
    half = dim // 2
    t = np.arange(n)
    freqs = (ROPE_THETA ** (-np.arange(0, half, 2, dtype=np.float32) / half)).astype(np.float32)
    ang = [(t // GRID_W).astype(np.float32)[:, None] * freqs[None, :], (t % GRID_W).astype(np.float32)[:, None] * freqs[None, :]]
    cos = np.concatenate([np.cos(a) for a in ang for _ in range(2)], axis=1)
    sin = np.concatenate([s * np.sin(a) for a in ang for s in (-1.0, 1.0)], axis=1)
    return (jnp.asarray(np.tile(cos, (1, reps)), jnp.float32), jnp.asarray(np.tile(sin, (1, reps)), jnp.float32))


def _rope(x, cos, sin_signed, quarter):
    w = x.shape[-1]
    lane = lax.broadcasted_iota(jnp.int32, x.shape, 1)
    swapped = jnp.where(lane % (2 * quarter) < quarter, pltpu.roll(x, w - quarter, 1), pltpu.roll(x, quarter, 1))
    return x * cos + swapped * sin_signed


def _lat_branches_kernel(sink_ref, mq_ref, sq_ref, fn_ref, skv_ref, mkv_ref, cckv_ref, ckr_ref, csk_ref, csv_ref,
                         qn_ref, kvn_ref, wuq_ref, wukv_ref, krsel_ref, ksel_ref, vsel_ref, ckrsel_ref, csel_ref,
                         cq32_ref, sq32_ref, ck32_ref, sk32_ref, cq64_ref, sq64_ref, ck64_ref, sk64_ref,
                         cs_ref, ss_ref, ccbd_ref, scbd_ref, oa_in, ob_in, od_in,
                         oa_ref, ob_ref, od_ref,
                         kn, vv, kr4, sk4, sv4, ckn, cvv, ckr4, csk4, csv4, xc, xs):
    bf16 = jnp.bfloat16
    j = pl.program_id(1)
    hi = dict(preferred_element_type=jnp.float32, precision=lax.Precision.HIGHEST)

    @pl.when(j == 0)
    def _():
        mkv = mkv_ref[...]
        ckv = _rms(mkv[:, :MLA_KV_LORA], kvn_ref[...])
        kv = jnp.dot(ckv.astype(bf16), wukv_ref[...], **_F32)
        kn[...] = kv[:, :MLA_HEADS * MLA_NOPE].astype(bf16)
        vv[...] = kv[:, MLA_HEADS * MLA_NOPE:].astype(bf16)
        kr = jnp.dot(mkv, krsel_ref[...], **hi)
        kr4[...] = _rope(kr, ck32_ref[...], sk32_ref[...], MLA_ROPE // 4).astype(bf16)
        skv = skv_ref[...]
        k4 = jnp.dot(skv, ksel_ref[...], **hi)
        sk4[...] = _rope(k4, ck64_ref[...], sk64_ref[...], SWA_HEAD_DIM // 4).astype(bf16)
        sv4[...] = jnp.dot(skv.astype(bf16), vsel_ref[...].astype(bf16), **_F32).astype(bf16)
        ckv_c = jnp.dot(cckv_ref[...].astype(bf16), wukv_ref[...], **_F32)
        ckn[...] = ckv_c[:, :MLA_HEADS * MLA_NOPE].astype(bf16)
        cvv[...] = ckv_c[:, MLA_HEADS * MLA_NOPE:].astype(bf16)
        ckr4[...] = jnp.dot(ckr_ref[...].astype(bf16), ckrsel_ref[...], **_F32).astype(bf16)
        for src, dst in ((csk_ref, csk4), (csv_ref, csv4)):
            acc = jnp.zeros(dst.shape, jnp.float32)
            for g in range(SWA_KV_HEADS):
                acc = acc + jnp.dot(src[g].astype(bf16), csel_ref[g], **_F32)
            dst[...] = acc.astype(bf16)
        x = fn_ref[...]
        xc[...] = jnp.dot(x, ccbd_ref[...], **hi)
        xs[...] = jnp.dot(x, scbd_ref[...], **hi)

    qn = _rms(mq_ref[...], qn_ref[...]).astype(bf16)
    q = jnp.dot(qn, wuq_ref[...], **_F32)
    q_nope = q[:, :MLA_HEADS * MLA_NOPE]
    q_rope = _rope(q[:, MLA_HEADS * MLA_NOPE:], cq32_ref[...], sq32_ref[...], MLA_ROPE // 4)
    o_a = jnp.zeros((LAT_QB, MLA_HEADS * MLA_V), jnp.float32)
    for h in range(MLA_HEADS):
        qh_n = (q_nope * _lane_mask(MLA_HEADS * MLA_NOPE, MLA_NOPE, h)).astype(bf16)
        qh_r = (q_rope * _lane_mask(MLA_HEADS * MLA_ROPE, MLA_ROPE, h)).astype(bf16)
        s_c = (lax.dot_general(qh_n, ckn[...], _NT, **_F32) + lax.dot_general(qh_r, ckr4[...], _NT, **_F32)) * MLA_SCALE
        s_l = (lax.dot_general(qh_n, kn[...], _NT, **_F32) + lax.dot_general(qh_r, kr4[...], _NT, **_F32)) * MLA_SCALE
        m = jnp.maximum(jnp.max(s_c, axis=-1, keepdims=True), jnp.max(s_l, axis=-1, keepdims=True))
        e_c, e_l = jnp.exp(s_c - m), jnp.exp(s_l - m)
        inv = 1.0 / (jnp.sum(e_c, axis=-1, keepdims=True) + jnp.sum(e_l, axis=-1, keepdims=True))
        o = jnp.dot((e_c * inv).astype(bf16), cvv[...], **_F32) + jnp.dot((e_l * inv).astype(bf16), vv[...], **_F32)
        o_a = o_a + o * _lane_mask(MLA_HEADS * MLA_V, MLA_V, h)
    oa_ref[...] = o_a
    ob_ref[...] = (jnp.dot(cs_ref[...], xc[...], **hi) - jnp.dot(ss_ref[...], xs[...], **hi)) * float((DEC_SEQ * FNET_CH) ** -0.5)
    sq = _rope(sq_ref[...], cq64_ref[...], sq64_ref[...], SWA_HEAD_DIM // 4)
    band = 3 * SWA_WINDOW
    start = pl.multiple_of(jnp.clip((j - 1) * LAT_QB, 0, DEC_SEQ - band), LAT_QB)
    kb = sk4[pl.ds(start, band), :]
    vb = sv4[pl.ds(start, band), :]
    qpos = j * LAT_QB + lax.broadcasted_iota(jnp.int32, (LAT_QB, band), 0)
    kpos = start + lax.broadcasted_iota(jnp.int32, (LAT_QB, band), 1)
    near = jnp.abs(kpos - qpos) <= SWA_WINDOW
    o_d = jnp.zeros((LAT_QB, SWA_HEADS * SWA_HEAD_DIM), jnp.float32)
    for h in range(SWA_HEADS):
        mh = _lane_mask(SWA_HEADS * SWA_HEAD_DIM, SWA_HEAD_DIM, h)
        qh = (sq * mh).astype(bf16)
        s_b = jnp.where(near, lax.dot_general(qh, kb, _NT, **_F32) * SWA_SCALE, NEG_INF)
        s_c = lax.dot_general(qh, csk4[...], _NT, **_F32) * SWA_SCALE
        sink = sink_ref[h]
        m = jnp.maximum(jnp.maximum(jnp.max(s_b, axis=-1, keepdims=True), jnp.max(s_c, axis=-1, keepdims=True)), sink)
        e_b, e_c = jnp.exp(s_b - m), jnp.exp(s_c - m)
        inv = 1.0 / (jnp.sum(e_b, axis=-1, keepdims=True) + jnp.sum(e_c, axis=-1, keepdims=True) + jnp.exp(sink - m))
        o = jnp.dot((e_b * inv).astype(bf16), vb, **_F32) + jnp.dot((e_c * inv).astype(bf16), csv4[...], **_F32)
        o_d = o_d + o * mh
    od_ref[...] = o_d


def lat_branches(hs, row0, n_seq, cols, sink, qn, kvn, wuq_p, wukv_p, c_ckv, c_krope, c_swa_k, c_swa_v, branch_bufs):
    bf16 = jnp.bfloat16
    f32 = jnp.float32
    nq = DEC_SEQ // LAT_QB
    kv_w = SWA_KV_HEADS * SWA_HEAD_DIM
    krsel = selection(BRANCH_W, MLA_HEADS * MLA_ROPE,
                      [(MLA_KV_LORA + r, MLA_ROPE * h + r) for h in range(MLA_HEADS) for r in range(MLA_ROPE)])
    ksel = selection(BRANCH_W, BRANCH_W, [((h // SWA_GROUP) * SWA_HEAD_DIM + r, h * SWA_HEAD_DIM + r)
                                          for h in range(SWA_HEADS) for r in range(SWA_HEAD_DIM)])
    vsel = selection(BRANCH_W, BRANCH_W, [(kv_w + (h // SWA_GROUP) * SWA_HEAD_DIM + r, h * SWA_HEAD_DIM + r)
                                          for h in range(SWA_HEADS) for r in range(SWA_HEAD_DIM)])
    ckrsel = selection(MLA_ROPE, MLA_HEADS * MLA_ROPE, [(r, MLA_ROPE * h + r) for h in range(MLA_HEADS) for r in range(MLA_ROPE)])
    csel = np.stack([selection(SWA_HEAD_DIM, BRANCH_W, [(r, h * SWA_HEAD_DIM + r) for h in range(SWA_HEADS)
                                                        if h // SWA_GROUP == g for r in range(SWA_HEAD_DIM)])
                     for g in range(SWA_KV_HEADS)])
    c32 = rope_tables(DEC_SEQ, MLA_ROPE, MLA_HEADS)
    c64q = rope_tables(DEC_SEQ, SWA_HEAD_DIM, SWA_HEADS)
    t = jnp.arange(DEC_SEQ, dtype=jnp.int32)
    ang = ((t[:, None] * t[None, :]) % DEC_SEQ).astype(f32) * f32(2.0 * np.pi / DEC_SEQ)
    cs, ss = jnp.cos(ang), jnp.sin(ang)
    cc, sc = dft_mats(FNET_CH)
    rb = row0 // DEC_SEQ
    qb0 = row0 // LAT_QB
    qblk = lambda cb: pl.BlockSpec((LAT_QB, BRANCH_W), lambda b, j: (qb0 + b * nq + j, cb))
    sblk = lambda cb: pl.BlockSpec((DEC_SEQ, BRANCH_W), lambda b, j: (rb + b, cb), pipeline_mode=pl.Buffered(1))
    per_b = lambda a: pl.BlockSpec((None,) + a.shape[1:], lambda b, j: (b,) + (0,) * (a.ndim - 1))
    full = lambda a: pl.BlockSpec(a.shape, lambda b, j: (0,) * a.ndim)
    qtab = lambda a: pl.BlockSpec((LAT_QB, a.shape[1]), lambda b, j: (j, 0))
    consts = [qn, kvn, wuq_p, wukv_p, jnp.asarray(krsel), jnp.asarray(ksel), jnp.asarray(vsel),
              jnp.asarray(ckrsel, bf16), jnp.asarray(csel, bf16)]
    sc_bf = lambda r, w: pltpu.VMEM((r, w), bf16)
    args = [sink, hs, hs, hs, hs, hs, c_ckv, c_krope, c_swa_k, c_swa_v, *consts,
            c32[0], c32[1], c32[0], c32[1], c64q[0], c64q[1], c64q[0], c64q[1], cs, ss,
            jnp.asarray(block_diag(cc, FNET_GROUPS)), jnp.asarray(block_diag(sc, FNET_GROUPS))]
    return pl.pallas_call(
        _lat_branches_kernel,
        grid=(n_seq, nq),
        in_specs=[pl.BlockSpec(memory_space=pltpu.SMEM), qblk(cols[0]), qblk(cols[1]), sblk(cols[2]), sblk(cols[3]),
                  sblk(cols[4]), per_b(c_ckv), per_b(c_krope), per_b(c_swa_k), per_b(c_swa_v)]
                 + [full(a) for a in consts]
                 + [qtab(c32[0]), qtab(c32[1]), full(c32[0]), full(c32[1]),
                    qtab(c64q[0]), qtab(c64q[1]), full(c64q[0]), full(c64q[1]),
                    pl.BlockSpec((LAT_QB, DEC_SEQ), lambda b, j: (j, 0)), pl.BlockSpec((LAT_QB, DEC_SEQ), lambda b, j: (j, 0)),
                    full(jnp.zeros((BRANCH_W, BRANCH_W))), full(jnp.zeros((BRANCH_W, BRANCH_W)))]
                 + [pl.BlockSpec(memory_space=pl.ANY)] * len(branch_bufs),
        input_output_aliases={len(args) + k: k for k in range(len(branch_bufs))},
        out_specs=[pl.BlockSpec((LAT_QB, BRANCH_W), lambda b, j: (qb0 + b * nq + j, 0))] * len(branch_bufs),
        out_shape=[jax.ShapeDtypeStruct(a.shape, a.dtype) for a in branch_bufs],
        scratch_shapes=[sc_bf(DEC_SEQ, 256), sc_bf(DEC_SEQ, 256), sc_bf(DEC_SEQ, 128), sc_bf(DEC_SEQ, 256), sc_bf(DEC_SEQ, 256),
                        sc_bf(PAST_LEN, 256), sc_bf(PAST_LEN, 256), sc_bf(PAST_LEN, 128), sc_bf(PAST_LEN, 256), sc_bf(PAST_LEN, 256),
                        pltpu.VMEM((DEC_SEQ, 256), f32), pltpu.VMEM((DEC_SEQ, 256), f32)],
        compiler_params=pltpu.CompilerParams(dimension_semantics=("arbitrary", "arbitrary"), vmem_limit_bytes=VMEM_LIMIT),
        name="lat_branches",
    )(*args, *branch_bufs)


def _oddeven_merge_sort_pairs(n):
    pairs = []
    p = 1
    while p < n:
        k = p
        while k >= 1:
            for j in range(k % p, n - k, 2 * k):
                for i in range(min(k, n - j - k)):
                    if (i + j) // (p * 2) == (i + j + k) // (p * 2):
                        pairs.append((i + j, i + j + k))
            k //= 2
        p *= 2
    return pairs


def _bitonic_merge_pairs(n):
    pairs = []
    k = n // 2
    while k >= 1:
        pairs += [(i, i + k) for i in range(n) if (i & k) == 0]
        k //= 2
    return pairs


def _compare_exchange(v, pairs):
    for i, j in pairs:
        v[i], v[j] = jnp.maximum(v[i], v[j]), jnp.minimum(v[i], v[j])


def _merge_top(v, shifts):
    nv = len(v)
    dropped = None
    for r in shifts:
        other = [pltpu.roll(v[nv - 1 - i], SUBLANES - r, 0) for i in range(nv)]
        lo = [jnp.minimum(v[i], other[i]) for i in range(nv)]
        v = [jnp.maximum(v[i], other[i]) for i in range(nv)]
        _compare_exchange(v, _bitonic_merge_pairs(nv))
        while len(lo) > 1:
            lo = [jnp.maximum(lo[2 * i], lo[2 * i + 1]) for i in range(len(lo) // 2)]
        d = lo[0]
        if dropped is not None:
            d = jnp.maximum(d, jnp.maximum(dropped, pltpu.roll(dropped, SUBLANES - r, 0)))
        dropped = d
    return v, dropped


def _sorted_top(p):
    nv = PEER_N_KEYS // SUBLANES
    v = [p[SUBLANES * i:SUBLANES * (i + 1)] for i in range(nv)]
    _compare_exchange(v, _oddeven_merge_sort_pairs(nv))
    return _merge_top(v, (4, 2, 1))


def _rank16_17(c):
    v = list(c)
    _compare_exchange(v, _oddeven_merge_sort_pairs(len(v)))
    v = v + [pltpu.roll(t, SUBLANES - 4, 0) for t in reversed(v)]
    _compare_exchange(v, _bitonic_merge_pairs(len(v)))
    v, dropped = _merge_top(v, (2, 1))
    return v[PEER_TOPK - 1][0:1], dropped[0:1]


_INNER_PAIRS = [(i, j) for i in range(1, 8) for j in range(1, 8) if (i + 1) * (j + 1) <= PEER_TOPK + 1]


def _peer_route_kernel(x_ref, sh_ref, sc_ref, wq_ref, keys_ref, ut_ref, th_ref, e1_ref, p2_ref, e2_ref,
                       sc_scr, a_scr, b_scr, a1_scr, ai_scr, bi_scr):
    tt = x_ref.shape[0]
    u = _ln(x_ref[...]) * (1.0 + sc_ref[...]) + sh_ref[...]
    ut = u.T.astype(jnp.bfloat16)
    ut_ref[...] = ut
    qt = jnp.dot(wq_ref[...], ut, preferred_element_type=jnp.float32)
    for hp in range(2 * PEER_HEADS):
        q_hp = qt[hp * PEER_HALF:(hp + 1) * PEER_HALF, :].astype(jnp.bfloat16)
        sc_scr[hp] = jnp.dot(keys_ref[hp], q_hp, preferred_element_type=jnp.float32)

    for scr in (a_scr, b_scr, a1_scr, ai_scr, bi_scr):
        scr[...] = jnp.full(scr.shape, NEG_INF, jnp.float32)

    def per_head(h, carry):
        for tc in range(tt // LANES):
            ls = slice(tc * LANES, (tc + 1) * LANES)
            s1 = sc_scr[2 * h, :, ls]
            s2 = sc_scr[2 * h + 1, :, ls]
            p1 = s1 - jnp.max(s1, axis=0, keepdims=True)
            p2 = s2 - jnp.max(s2, axis=0, keepdims=True)
            a, a16 = _sorted_top(p1)
            b, b16 = _sorted_top(p2)
            for i in range(PEER_TOPK + 1):
                ai = a[i][0:1] if i < PEER_TOPK else a16[0:1]
                bi = b[i][0:1] if i < PEER_TOPK else b16[0:1]
                a_scr[i:i + 1, :] = ai
                b_scr[i:i + 1, :] = bi
                if i >= 1:
                    a1_scr[i - 1:i, :] = ai
                for r, (pi, pj) in enumerate(_INNER_PAIRS):
                    if pi == i:
                        ai_scr[r:r + 1, :] = ai
                    if pj == i:
                        bi_scr[r:r + 1, :] = bi
            bb = b_scr[...]
            cand_tiles = [a_scr[0:1, :] + bb[SUBLANES * k:SUBLANES * (k + 1)] for k in range(3)]
            cand_tiles += [a1_scr[SUBLANES * k:SUBLANES * (k + 1), :] + bb[0:1] for k in range(2)]
            cand_tiles += [ai_scr[SUBLANES * k:SUBLANES * (k + 1), :] + bi_scr[SUBLANES * k:SUBLANES * (k + 1), :]
                           for k in range(3)]
            c16, c17 = _rank16_17(cand_tiles)
            tau = 0.5 * (c16 + c17)
            cand = jnp.concatenate(cand_tiles, axis=0)
            z = jnp.sum(jnp.where(cand >= tau, jnp.exp(cand), 0.0), axis=0, keepdims=True)
            th_ref[h, :, :, ls] = (tau - p1).reshape(PEER_N_KEYS // SUBLANES, SUBLANES, LANES)
            e1_ref[h, :, :, ls] = (0.5 * jnp.exp(p1) / z).reshape(PEER_N_KEYS // SUBLANES, SUBLANES, LANES)
            p2_ref[h, :, ls] = p2
            e2_ref[h, :, ls] = jnp.exp(p2)
        return carry

    lax.fori_loop(0, PEER_HEADS, per_head, 0)


def _peer_dense_kernel(ut_ref, th_ref, e1_ref, p2_ref, e2_ref, u_ref, vt_ref, o_ref, act_ref, gt_ref):
    j = pl.program_id(1)
    eb, tt = act_ref.shape
    n_i1 = eb // PEER_N_KEYS
    assert n_i1 == SUBLANES
    nch = tt // DENSE_CHUNK

    @pl.when(j == 0)
    def _():
        o_ref[...] = jnp.zeros_like(o_ref)

    def mm1(c):
        cs = slice(c * DENSE_CHUNK, (c + 1) * DENSE_CHUNK)
        act_ref[:, cs] = jnp.dot(u_ref[...], ut_ref[:, cs], preferred_element_type=jnp.float32)

    def mm2(c):
        cs = slice(c * DENSE_CHUNK, (c + 1) * DENSE_CHUNK)
        o_ref[:, cs] += jnp.dot(vt_ref[...], gt_ref[:, cs], preferred_element_type=jnp.float32)

    def weights(c):
        for half in range(DENSE_CHUNK // LANES):
            ls = slice(c * DENSE_CHUNK + half * LANES, c * DENSE_CHUNK + (half + 1) * LANES)
            for il in range(n_i1):
                w = None
                for h in range(PEER_HEADS):
                    th_rows = th_ref[h, j, :, ls]
                    e1_rows = e1_ref[h, j, :, ls]
                    wh = jnp.where(p2_ref[h, :, ls] >= th_rows[il:il + 1], e2_ref[h, :, ls] * e1_rows[il:il + 1], 0.0)
                    w = wh if w is None else w + wh
                rs = slice(il * PEER_N_KEYS, (il + 1) * PEER_N_KEYS)
                a = act_ref[rs, ls]
                gt_ref[rs, ls] = (w * (a * (1.0 + lax.erf(a * np.float32(np.sqrt(0.5)))))).astype(jnp.bfloat16)

    mm1(0)
    for c in range(nch):
        if c + 1 < nch:
            mm1(c + 1)
        weights(c)
        if c >= 1:
            mm2(c - 1)
    mm2(nch - 1)


def _residual_ln_kernel(x_ref, g_ref, pt_ref, lng_ref, lnb_ref, y_ref):
    r = DEEPNORM_ALPHA * x_ref[...] + g_ref[...] * pt_ref[...].T
    y_ref[...] = _ln(r) * lng_ref[...] + lnb_ref[...]


def peer_layer(x, mod, tile_cond, wq_t, keys, u_tab, vt_tab, ln_g, ln_b, *, tt_route, tt, eb):
    t, d = x.shape
    nh = PEER_HEADS
    row = lambda k, tsz: pl.BlockSpec((None, None, 1, d), lambda i, *_: (tile_cond(i, tsz), 3 + k, 0, 0))
    rows_shape = (nh, PEER_N_KEYS // SUBLANES, SUBLANES, t)
    tile_shape = (nh, PEER_N_KEYS, t)
    ut, th, e1, p2, e2 = pl.pallas_call(
        _peer_route_kernel,
        grid=(t // tt_route,),
        in_specs=[pl.BlockSpec((tt_route, d), lambda i: (i, 0)), row(0, tt_route), row(1, tt_route),
                  pl.BlockSpec(wq_t.shape, lambda i: (0, 0)),
                  pl.BlockSpec(keys.shape, lambda i: (0, 0, 0))],
        out_specs=[pl.BlockSpec((d, tt_route), lambda i: (0, i)),
                   pl.BlockSpec(rows_shape[:3] + (tt_route,), lambda i: (0, 0, 0, i)),
                   pl.BlockSpec(rows_shape[:3] + (tt_route,), lambda i: (0, 0, 0, i)),
                   pl.BlockSpec(tile_shape[:2] + (tt_route,), lambda i: (0, 0, i)),
                   pl.BlockSpec(tile_shape[:2] + (tt_route,), lambda i: (0, 0, i))],
        out_shape=[jax.ShapeDtypeStruct((d, t), jnp.bfloat16),
                   jax.ShapeDtypeStruct(rows_shape, jnp.float32), jax.ShapeDtypeStruct(rows_shape, jnp.float32),
                   jax.ShapeDtypeStruct(tile_shape, jnp.float32), jax.ShapeDtypeStruct(tile_shape, jnp.float32)],
        scratch_shapes=[pltpu.VMEM((2 * nh, PEER_N_KEYS, tt_route), jnp.float32),
                        pltpu.VMEM((3 * SUBLANES, LANES), jnp.float32), pltpu.VMEM((3 * SUBLANES, LANES), jnp.float32),
                        pltpu.VMEM((2 * SUBLANES, LANES), jnp.float32), pltpu.VMEM((3 * SUBLANES, LANES), jnp.float32),
                        pltpu.VMEM((3 * SUBLANES, LANES), jnp.float32)],
        compiler_params=pltpu.CompilerParams(dimension_semantics=("arbitrary",), vmem_limit_bytes=VMEM_LIMIT),
        name="peer_route",
    )(x, mod, mod, wq_t, keys)

    ne = PEER_N_EXPERTS // eb
    once = dict(pipeline_mode=pl.Buffered(1))
    pt = pl.pallas_call(
        _peer_dense_kernel,
        grid=(t // tt, ne),
        in_specs=[pl.BlockSpec((d, tt), lambda i, j: (0, i)),
                  pl.BlockSpec(rows_shape[:3] + (tt,), lambda i, j: (0, 0, 0, i), **once),
                  pl.BlockSpec(rows_shape[:3] + (tt,), lambda i, j: (0, 0, 0, i), **once),
                  pl.BlockSpec(tile_shape[:2] + (tt,), lambda i, j: (0, 0, i), **once),
                  pl.BlockSpec(tile_shape[:2] + (tt,), lambda i, j: (0, 0, i), **once),
                  pl.BlockSpec((eb, d), lambda i, j: (j, 0)),
                  pl.BlockSpec((None, d, eb), lambda i, j: (j, 0, 0))],
        out_specs=pl.BlockSpec((d, tt), lambda i, j: (0, i)),
        out_shape=jax.ShapeDtypeStruct((d, t), jnp.float32),
        scratch_shapes=[pltpu.VMEM((eb, tt), jnp.float32), pltpu.VMEM((eb, tt), jnp.bfloat16)],
        compiler_params=pltpu.CompilerParams(dimension_semantics=("arbitrary", "arbitrary"),
                                             vmem_limit_bytes=VMEM_LIMIT),
        name="peer_dense",
    )(ut, th, e1, p2, e2, u_tab, vt_tab)

    tl = PEER_ROUTE_TILE
    return pl.pallas_call(
        _residual_ln_kernel,
        grid=(t // tl,),
        in_specs=[pl.BlockSpec((tl, d), lambda i: (i, 0)), row(2, tl), pl.BlockSpec((d, tl), lambda i: (0, i)),
                  pl.BlockSpec((1, d), lambda i: (0, 0)), pl.BlockSpec((1, d), lambda i: (0, 0))],
        out_specs=pl.BlockSpec((tl, d), lambda i: (i, 0)),
        out_shape=jax.ShapeDtypeStruct((t, d), jnp.float32),
        compiler_params=pltpu.CompilerParams(dimension_semantics=("arbitrary",), vmem_limit_bytes=VMEM_LIMIT),
        name="peer_residual_ln",
    )(x, mod, pt, ln_g, ln_b)


def _tile_cond(i, tt):
    ctx_tiles = BATCH * SEQ // tt
    return jnp.where(i < ctx_tiles, 0, 1 + (i - ctx_tiles) // (DEC_SEQ // tt))


def kernel(x_prompt, x_sample, c, cache_mla_ckv, cache_mla_krope, cache_swa_k, cache_swa_v, state_gla,
           c_ctx, w_ada, b_ada, w_in, mla_q_norm, w_uq, mla_kv_norm, w_ukv,
           w_gla_a_fwd, b_gla_a_fwd, w_gla_a_bwd, b_gla_a_bwd, gla_norm, swa_sink,
           w_branch, w_out, ln1_g, ln1_b, ln2_g, ln2_b, w_peer_q, peer_keys, peer_u, peer_v):
    bf16 = jnp.bfloat16
    n_ctx, n_lat = BATCH * SEQ, DEC_BATCH * DEC_SEQ
    n_tok = n_ctx + n_lat
    conds = jnp.concatenate([c_ctx[None, :], c], axis=0)
    x = jnp.concatenate([x_prompt.reshape(n_ctx, D_MODEL), x_sample.reshape(n_lat, D_MODEL)], axis=0)
    src = dict(zip(IN_NAMES, (0,) + IN_OFFSETS))
    seg_args = (n_ctx // GLA_SEG, SEQ // GLA_SEG, DEC_SEQ // GLA_SEG)
    lane_block = lambda name, width: HS_OFFSET[name] // width
    gla_cols = (lane_block('gla_q', HK), lane_block('gla_k', HK), lane_block('gla_v', HV),
                HS_OFFSET['gla_af'] // LANES, lane_block('gla_g', HV))
    z_row = lambda name: HS_OFFSET[name] - gla_cols[3] * LANES
    ctx_states = []
    for l in range(DEPTH):
        mod = (jax.nn.silu(conds) @ w_ada[l] + b_ada[l]).reshape(1 + DEC_BATCH, 6, 1, D_MODEL)
        w_parts = jnp.concatenate([w_in[l][:, src[n]:src[n] + _widths[n]] for n in HS_ORDER], axis=1)
        w_parts = jnp.pad(w_parts, ((0, 0), (0, PARTS_WIDTH - GATES_OFFSET))).astype(bf16)
        hs = in_proj(x, mod, _tile_cond, w_parts, tt=512)
        blk = lambda name: HS_OFFSET[name] // BRANCH_W
        mla_w = (mla_q_norm[l][None, :], mla_kv_norm[l][None, :], regroup_uq(w_uq[l]).astype(bf16),
                 regroup_ukv(w_ukv[l]).astype(bf16))
        o_a, o_b, o_d, ckv = ctx_branches(
            hs, BATCH, (blk('mla_q'), blk('fnet'), blk('swa_q'), blk('swa_k'), blk('mla_kv')), swa_sink[l], *mla_w)
        o_a, o_b, o_d = lat_branches(
            hs, n_ctx, DEC_BATCH, (blk('mla_q'), blk('swa_q'), blk('fnet'), blk('swa_k'), blk('mla_kv')), swa_sink[l],
            *mla_w, cache_mla_ckv[:, l], cache_mla_krope[:, l], cache_swa_k[:, l], cache_swa_v[:, l], (o_a, o_b, o_d))
        ctx_part = lambda name, lo, hi: hs[:n_ctx, HS_OFFSET[name] + lo:HS_OFFSET[name] + hi]
        kv_heads = lambda name: (ctx_part(name, 0, SWA_KV_HEADS * SWA_HEAD_DIM)
                                 .reshape(BATCH, SEQ, SWA_KV_HEADS, SWA_HEAD_DIM).transpose(0, 2, 1, 3))
        st = (ckv.reshape(BATCH, SEQ, MLA_KV_LORA),
              ctx_part('mla_kv', MLA_KV_LORA, MLA_KV_LORA + MLA_ROPE).reshape(BATCH, SEQ, MLA_ROPE),
              kv_heads('swa_k'), kv_heads('swa_v'))
        w2 = lambda w, name: (jnp.zeros((LANES, HK), jnp.float32)
                              .at[z_row(name):z_row(name) + GLA_GATE_RANK].set(w).astype(bf16))
        zero = jnp.zeros((BATCH, GLA_HEADS, GLA_DK, GLA_DV), jnp.float32)
        o_c, s_f, s_b = gla_branch(
            hs.reshape(n_tok // GLA_SEG, GLA_SEG, PARTS_WIDTH), gla_cols,
            w2(w_gla_a_fwd[l], 'gla_af'), b_gla_a_fwd[l][None, :], w2(w_gla_a_bwd[l], 'gla_ab'), b_gla_a_bwd[l][None, :],
            jnp.tile(gla_norm[l], GLA_HEADS)[None, :],
            jnp.concatenate([zero, state_gla[:, l, 0]], axis=0), jnp.concatenate([zero, state_gla[:, l, 1]], axis=0),
            seg_args)
        ctx_states.append(st + (jnp.stack([s_f[:BATCH], s_b[:BATCH]], axis=1),))
        branches = (o_a, o_b, o_c.reshape(n_tok, BRANCH_W), o_d)
        x1 = merge(x, mod, _tile_cond, branches, w_in[l][:, GATES_OFFSET:].astype(bf16), w_branch[l].astype(bf16),
                   w_out[l].astype(bf16), ln1_g[l][None, :], ln1_b[l][None, :], tt=256)
        x = peer_layer(
            x1, mod, _tile_cond,
            w_peer_q[l].T.astype(bf16),
            peer_keys[l].reshape(2 * PEER_HEADS, PEER_N_KEYS, PEER_HALF).astype(bf16),
            peer_u[l].astype(bf16),
            peer_v[l].reshape(-1, PEER_EXPERT_BLOCK, D_MODEL).transpose(0, 2, 1).astype(bf16),
            ln2_g[l][None, :], ln2_b[l][None, :],
            tt_route=PEER_ROUTE_TILE, tt=PEER_TOKEN_TILE, eb=PEER_EXPERT_BLOCK)

    h = x[:n_ctx].reshape(BATCH, SEQ, D_MODEL)
    z = x[n_ctx:].reshape(DEC_BATCH, DEC_SEQ, D_MODEL)
    new_mla_ckv = jnp.stack([st[0] for st in ctx_states], axis=1)
    new_mla_krope = jnp.stack([st[1] for st in ctx_states], axis=1)
    new_swa_k = jnp.stack([st[2] for st in ctx_states], axis=1)
    new_swa_v = jnp.stack([st[3] for st in ctx_states], axis=1)
    new_gla_state = jnp.stack([st[4] for st in ctx_states], axis=1)
    return (h, z, new_mla_ckv, new_mla_krope, new_swa_k, new_swa_v, new_gla_state)
```

```python
import functools

import jax
import jax.numpy as jnp
from jax import lax
import numpy as np
from jax.experimental import pallas as pl
from jax.experimental.pallas import tpu as pltpu

D_MODEL = 1024
BATCH = 32
SEQ = 256
DEPTH = 2
DEC_BATCH = 2
DEC_SEQ = 2048
PAST_LEN = 512

GRID_W = 64
N_BRANCH = 4
BRANCH_W = 256
MLA_HEADS = 4
MLA_Q_LORA = 256
MLA_KV_LORA = 128
MLA_NOPE = 64
MLA_ROPE = 32
MLA_V = 64
MLA_SCALE = (MLA_NOPE + MLA_ROPE) ** -0.5
FNET_GROUPS = 4
FNET_CH = BRANCH_W // FNET_GROUPS
GLA_HEADS = 4
GLA_DK = 32
GLA_DV = 64
GLA_GATE_RANK = 16
GLA_TAU = 16.0
GLA_CHUNK = 64
SWA_HEADS = 4
SWA_KV_HEADS = 2
SWA_GROUP = SWA_HEADS // SWA_KV_HEADS
SWA_HEAD_DIM = 64
SWA_WINDOW = 128
SWA_SCALE = SWA_HEAD_DIM ** -0.5
ATTN_BLOCK = 128
PEER_HEADS = 8
PEER_N_KEYS = 128
PEER_N_EXPERTS = PEER_N_KEYS * PEER_N_KEYS
PEER_KEY_DIM = 256
PEER_HALF = PEER_KEY_DIM // 2
PEER_TOPK = 16
PEER_ROUTE_TILE = 512
PEER_TOKEN_TILE = 1024
PEER_EXPERT_BLOCK = 8 * PEER_N_KEYS
DENSE_CHUNK = 256
GLA_SEG = 128
GLA_STEPS = 8
HK = GLA_HEADS * GLA_DK
HV = GLA_HEADS * GLA_DV
LAT_QB = SWA_WINDOW

ROPE_THETA = 10000.0
NORM_EPS = 1e-6
DEEPNORM_ALPHA = (2.0 * DEPTH) ** 0.25
DEEPNORM_BETA = (8.0 * DEPTH) ** -0.25

IN_SPLITS = (
    ('mla_q', MLA_Q_LORA),
    ('mla_kv', MLA_KV_LORA + MLA_ROPE),
    ('fnet', BRANCH_W),
    ('gla_q', GLA_HEADS * GLA_DK),
    ('gla_k', GLA_HEADS * GLA_DK),
    ('gla_v', GLA_HEADS * GLA_DV),
    ('gla_g', BRANCH_W),
    ('gla_af', GLA_GATE_RANK),
    ('gla_ab', GLA_GATE_RANK),
    ('swa_q', SWA_HEADS * SWA_HEAD_DIM),
    ('swa_k', SWA_KV_HEADS * SWA_HEAD_DIM),
    ('swa_v', SWA_KV_HEADS * SWA_HEAD_DIM),
    ('gates', N_BRANCH * D_MODEL),
)
IN_NAMES = tuple(n for n, _ in IN_SPLITS)
IN_OFFSETS = tuple(int(o) for o in np.cumsum([w for _, w in IN_SPLITS])[:-1])
IN_WIDTH = int(sum(w for _, w in IN_SPLITS))
GATES_OFFSET = IN_OFFSETS[-1]
HS_ORDER = ('mla_q', 'fnet', 'gla_q', 'gla_k', 'gla_v', 'gla_g', 'swa_q', 'swa_k', 'swa_v', 'mla_kv', 'gla_af', 'gla_ab')
_widths = dict(IN_SPLITS)
HS_OFFSET = {n: int(o) for n, o in zip(HS_ORDER, np.cumsum([0] + [_widths[n] for n in HS_ORDER])[:-1])}
PARTS_WIDTH = -(-GATES_OFFSET // 128) * 128

LANES = 128
SUBLANES = 8
NEG_INF = float('-inf')
VMEM_LIMIT = 56 * 1024 * 1024


def _ln(x):
    mu = jnp.mean(x, -1, keepdims=True)
    xc = x - mu
    var = jnp.mean(xc * xc, -1, keepdims=True)
    return xc * lax.rsqrt(var + NORM_EPS)


def _in_proj_kernel(x_ref, sh_ref, sc_ref, w_ref, o_ref):
    u = _ln(x_ref[...]) * (1.0 + sc_ref[...]) + sh_ref[...]
    o_ref[...] = jnp.dot(u.astype(jnp.bfloat16), w_ref[...], preferred_element_type=jnp.float32)


def in_proj(x, mod, tile_cond, w, *, tt):
    t, d = x.shape
    n = w.shape[1]
    row = lambda k: pl.BlockSpec((None, None, 1, d), lambda i: (tile_cond(i, tt), k, 0, 0))
    return pl.pallas_call(
        _in_proj_kernel,
        grid=(t // tt,),
        in_specs=[pl.BlockSpec((tt, d), lambda i: (i, 0)), row(0), row(1), pl.BlockSpec((d, n), lambda i: (0, 0))],
        out_specs=pl.BlockSpec((tt, n), lambda i: (i, 0)),
        out_shape=jax.ShapeDtypeStruct((t, n), jnp.float32),
        compiler_params=pltpu.CompilerParams(dimension_semantics=("arbitrary",), vmem_limit_bytes=VMEM_LIMIT),
        name="in_proj",
    )(x, mod, mod, w)


def _merge_kernel(x_ref, sh_ref, sc_ref, g_ref, ba_ref, bb_ref, bc_ref, bd_ref, wg_ref, wb_ref, wo_ref,
                  lng_ref, lnb_ref, y_ref):
    x = x_ref[...]
    u = (_ln(x) * (1.0 + sc_ref[...]) + sh_ref[...]).astype(jnp.bfloat16)
    acc = jnp.zeros(x.shape, jnp.float32)
    for b, br_ref in enumerate((ba_ref, bb_ref, bc_ref, bd_ref)):
        gate = jnp.dot(u, wg_ref[:, b * D_MODEL:(b + 1) * D_MODEL], preferred_element_type=jnp.float32)
        proj = jnp.dot(br_ref[...].astype(jnp.bfloat16), wb_ref[b], preferred_element_type=jnp.float32)
        acc = acc + jax.nn.sigmoid(gate) * proj
    mix = jnp.dot(acc.astype(jnp.bfloat16), wo_ref[...], preferred_element_type=jnp.float32)
    y_ref[...] = _ln(DEEPNORM_ALPHA * x + g_ref[...] * mix) * lng_ref[...] + lnb_ref[...]


def merge(x, mod, tile_cond, branches, w_gates, w_branch, w_out, ln_g, ln_b, *, tt):
    t, d = x.shape
    row = lambda k: pl.BlockSpec((None, None, 1, d), lambda i: (tile_cond(i, tt), k, 0, 0))
    full = lambda a: pl.BlockSpec(a.shape, lambda i: (0,) * a.ndim)
    return pl.pallas_call(
        _merge_kernel,
        grid=(t // tt,),
        in_specs=[pl.BlockSpec((tt, d), lambda i: (i, 0)), row(0), row(1), row(2),
                  *[pl.BlockSpec((tt, BRANCH_W), lambda i: (i, 0)) for _ in range(N_BRANCH)],
                  full(w_gates), full(w_branch), full(w_out), full(ln_g), full(ln_b)],
        out_specs=pl.BlockSpec((tt, d), lambda i: (i, 0)),
        out_shape=jax.ShapeDtypeStruct((t, d), jnp.float32),
        compiler_params=pltpu.CompilerParams(dimension_semantics=("arbitrary",), vmem_limit_bytes=VMEM_LIMIT),
        name="merge",
    )(x, mod, mod, mod, *branches, w_gates, w_branch, w_out, ln_g, ln_b)


def _gla_local_kernel(q_ref, k_ref, v_ref, z_ref, w2_ref, b2_ref, o_ref, qd_ref, st_ref, dl_ref,
                      s_scr, d_scr, stage, ta, tk, tq, *, reverse):
    i = pl.program_id(0)
    nseg = q_ref.shape[0]

    @pl.when(i == 0)
    def _():
        s_scr[...] = jnp.zeros_like(s_scr)
        d_scr[...] = jnp.ones_like(d_scr)
        stage[...] = jnp.zeros_like(stage)

    def to_lanes(x):
        stage[0:nseg, :] = x
        return stage[...].T

    def from_lanes(xt):
        return xt.T[0:nseg, :]

    steps = range(GLA_STEPS - 1, -1, -1) if reverse else range(GLA_STEPS)
    for j in steps:
        z = jnp.dot(z_ref[:, j, :].astype(jnp.bfloat16), w2_ref[...], preferred_element_type=jnp.float32) + b2_ref[...]
        ta[...] = to_lanes(jnp.exp(jax.nn.log_sigmoid(z) / GLA_TAU))
        tq[...] = to_lanes(q_ref[:, j, :] * (GLA_DK ** -0.5))
        tk[...] = to_lanes(k_ref[:, j, :])
        d = d_scr[...] * ta[...]
        d_scr[...] = d
        qd_ref[:, j, :] = from_lanes(tq[...] * d)
        vt = [to_lanes(v_ref[:, j, LANES * c:LANES * (c + 1)]) for c in range(HV // LANES)]
        outs = []
        for h in range(GLA_HEADS):
            off = (h * GLA_DV) % LANES
            vh = vt[(h * GLA_DV) // LANES][off:off + GLA_DV]

            def rows(g, acc, h=h, vh=vh):
                r0 = pl.multiple_of(h * GLA_DK + g * SUBLANES, SUBLANES)
                a8 = ta[pl.ds(r0, SUBLANES), :]
                k8 = tk[pl.ds(r0, SUBLANES), :]
                q8 = tq[pl.ds(r0, SUBLANES), :]
                for r in range(SUBLANES):
                    s_new = a8[r:r + 1] * s_scr[r0 + r] + k8[r:r + 1] * vh
                    s_scr[r0 + r] = s_new
                    acc = acc + q8[r:r + 1] * s_new
                return acc

            outs.append(lax.fori_loop(0, GLA_DK // SUBLANES, rows, jnp.zeros((GLA_DV, LANES), jnp.float32)))
        for c in range(HV // LANES):
            per = LANES // GLA_DV
            o_ref[:, j, LANES * c:LANES * (c + 1)] = from_lanes(jnp.concatenate(outs[per * c:per * (c + 1)], axis=0))

    @pl.when(i == pl.num_programs(0) - 1)
    def _():
        st_ref[...] = s_scr[...]
        dl_ref[...] = d_scr[...].T


def gla_local(hs3, w2pad, b2, cols, *, reverse):
    nseg, seg, _ = hs3.shape
    nb = seg // GLA_STEPS
    tmap = (lambda i: nb - 1 - i) if reverse else (lambda i: i)
    blk = lambda width, cb: pl.BlockSpec((nseg, GLA_STEPS, width), lambda i: (0, tmap(i), cb))
    t3 = lambda width: jax.ShapeDtypeStruct((nseg, seg, width), jnp.float32)
    return pl.pallas_call(
        functools.partial(_gla_local_kernel, reverse=reverse),
        grid=(nb,),
        in_specs=[blk(HK, cols[0]), blk(HK, cols[1]), blk(HV, cols[2]), blk(LANES, cols[3]),
                  pl.BlockSpec((LANES, HK), lambda i: (0, 0)), pl.BlockSpec((1, HK), lambda i: (0, 0))],
        out_specs=[pl.BlockSpec((nseg, GLA_STEPS, HV), lambda i: (0, tmap(i), 0)),
                   pl.BlockSpec((nseg, GLA_STEPS, HK), lambda i: (0, tmap(i), 0)),
                   pl.BlockSpec((HK, GLA_DV, LANES), lambda i: (0, 0, 0)),
                   pl.BlockSpec((LANES, HK), lambda i: (0, 0))],
        out_shape=[t3(HV), t3(HK), jax.ShapeDtypeStruct((HK, GLA_DV, LANES), jnp.float32),
                   jax.ShapeDtypeStruct((LANES, HK), jnp.float32)],
        scratch_shapes=[pltpu.VMEM((HK, GLA_DV, LANES), jnp.float32), pltpu.VMEM((HK, LANES), jnp.float32),
                        pltpu.VMEM((LANES, LANES), jnp.float32), pltpu.VMEM((HK, LANES), jnp.float32),
                        pltpu.VMEM((HK, LANES), jnp.float32), pltpu.VMEM((HK, LANES), jnp.float32)],
        compiler_params=pltpu.CompilerParams(dimension_semantics=("arbitrary",), vmem_limit_bytes=VMEM_LIMIT),
        name="gla_local_bwd" if reverse else "gla_local_fwd",
    )(hs3, hs3, hs3, hs3, w2pad, b2)


def _seg_seq(c, n_ctx_seg, ctx_per, lat_per):
    return jnp.where(c < n_ctx_seg, c // ctx_per, n_ctx_seg // ctx_per + (c - n_ctx_seg) // lat_per)


def _seg_pos(c, n_ctx_seg, ctx_per, lat_per):
    return (jnp.where(c < n_ctx_seg, c % ctx_per, (c - n_ctx_seg) % lat_per),
            jnp.where(c < n_ctx_seg, ctx_per, lat_per))


def _gla_chain_kernel(*refs, reverse, combine, seg_args):
    if combine:
        ol_ref, qd_ref, sl_ref, dl_ref, s0_ref, of_ref, gate_ref, gn_ref, o_ref, fin_ref, st = refs
    else:
        ol_ref, qd_ref, sl_ref, dl_ref, s0_ref, o_ref, fin_ref, st = refs
    i = pl.program_id(0)
    c = pl.num_programs(0) - 1 - i if reverse else i
    pos, per = _seg_pos(c, *seg_args)
    first = (pos == per - 1) if reverse else (pos == 0)

    @pl.when(first)
    def _():
        st[...] = s0_ref[...]

    s_in = st[...]
    o = ol_ref[...] + lax.dot_general(qd_ref[...].astype(jnp.bfloat16), s_in.astype(jnp.bfloat16),
                                      (((1,), (1,)), ((), ())), preferred_element_type=jnp.float32)
    row_head = lax.broadcasted_iota(jnp.int32, (HV, HK), 0) // GLA_DV
    col_head = lax.broadcasted_iota(jnp.int32, (HV, HK), 1) // GLA_DK
    s_loc = jnp.where(row_head == col_head, jnp.concatenate([sl_ref[...]] * GLA_HEADS, axis=0), 0.0)
    s_new = dl_ref[...] * s_in + s_loc
    st[...] = s_new
    fin_ref[...] = s_new
    if combine:
        x = o + of_ref[...]
        lane_head = lax.broadcasted_iota(jnp.int32, x.shape, 1) // GLA_DV
        x2 = x * x
        scale = jnp.zeros_like(x)
        for h in range(GLA_HEADS):
            ms = jnp.sum(jnp.where(lane_head == h, x2, 0.0), axis=-1, keepdims=True) * (1.0 / GLA_DV)
            scale = jnp.where(lane_head == h, lax.rsqrt(ms + NORM_EPS), scale)
        o = x * scale * gn_ref[...] * jax.nn.silu(gate_ref[...])
    o_ref[...] = o


def gla_chain(o_loc, qd, sl_t, dl, s0_t, seg_args, *, reverse, o_fwd=None, hs3=None, gate_col=None, gnorm=None):
    nseg, seg, _ = o_loc.shape
    n_seq = s0_t.shape[0]
    cmap = (lambda i: nseg - 1 - i) if reverse else (lambda i: i)
    seq = lambda i: _seg_seq(cmap(i), *seg_args)
    combine = o_fwd is not None
    in_specs = [pl.BlockSpec((None, seg, HV), lambda i: (cmap(i), 0, 0)),
                pl.BlockSpec((None, seg, HK), lambda i: (cmap(i), 0, 0)),
                pl.BlockSpec((None, GLA_DV, HK), lambda i: (cmap(i), 0, 0)),
                pl.BlockSpec((None, 1, HK), lambda i: (cmap(i), 0, 0)),
                pl.BlockSpec((None, HV, HK), lambda i: (seq(i), 0, 0))]
    args = [o_loc, qd, sl_t, dl.reshape(dl.shape[0], 1, HK), s0_t]
    if combine:
        in_specs += [pl.BlockSpec((None, seg, HV), lambda i: (cmap(i), 0, 0)),
                     pl.BlockSpec((None, seg, HV), lambda i: (cmap(i), 0, gate_col)),
                     pl.BlockSpec((1, HV), lambda i: (0, 0))]
        args += [o_fwd, hs3, gnorm]
    return pl.pallas_call(
        functools.partial(_gla_chain_kernel, reverse=reverse, combine=combine, seg_args=seg_args),
        grid=(nseg,),
        in_specs=in_specs,
        out_specs=[pl.BlockSpec((None, seg, HV), lambda i: (cmap(i), 0, 0)),
                   pl.BlockSpec((None, HV, HK), lambda i: (seq(i), 0, 0))],
        out_shape=[jax.ShapeDtypeStruct((nseg, seg, HV), jnp.float32),
                   jax.ShapeDtypeStruct((n_seq, HV, HK), jnp.float32)],
        scratch_shapes=[pltpu.VMEM((HV, HK), jnp.float32)],
        compiler_params=pltpu.CompilerParams(dimension_semantics=("arbitrary",), vmem_limit_bytes=VMEM_LIMIT),
        name="gla_chain_bwd" if reverse else "gla_chain_fwd",
    )(*args)


def _state_to_chain(s):
    n = s.shape[0]
    eye = jnp.eye(GLA_HEADS, dtype=s.dtype)
    return jnp.einsum('nhkv,hg->nhvgk', s, eye).reshape(n, HV, HK)


def _state_from_chain(f):
    n = f.shape[0]
    f5 = f.reshape(n, GLA_HEADS, GLA_DV, GLA_HEADS, GLA_DK)
    return jnp.stack([f5[:, h, :, h, :] for h in range(GLA_HEADS)], axis=1).transpose(0, 1, 3, 2)


def gla_branch(hs3, cols, w2f, b2f, w2b, b2b, gnorm, s0_fwd, s0_bwd, seg_args):
    nseg = hs3.shape[0]
    o_f = None
    for reverse, w2, b2, s0 in ((False, w2f, b2f, s0_fwd), (True, w2b, b2b, s0_bwd)):
        o_loc, qd, st, dl = gla_local(hs3, w2, b2, cols[:4], reverse=reverse)
        sl_t = jnp.transpose(st, (2, 1, 0))[:nseg]
        if not reverse:
            o_f, fin_f = gla_chain(o_loc, qd, sl_t, dl[:nseg], _state_to_chain(s0), seg_args, reverse=False)
        else:
            o, fin_b = gla_chain(o_loc, qd, sl_t, dl[:nseg], _state_to_chain(s0), seg_args, reverse=True,
                                 o_fwd=o_f, hs3=hs3, gate_col=cols[4], gnorm=gnorm)
    return o, _state_from_chain(fin_f), _state_from_chain(fin_b)


_NT = (((1,), (1,)), ((), ()))
_F32 = dict(preferred_element_type=jnp.float32)


def _rms(x, g):
    return x * lax.rsqrt(jnp.mean(x * x, -1, keepdims=True) + NORM_EPS) * g


def _lane_mask(width, seg, h):
    lane = lax.broadcasted_iota(jnp.int32, (1, width), 1)
    return (lane // seg == h).astype(jnp.float32)


def _softmax_rows(s, sink=None):
    m = jnp.max(s, axis=-1, keepdims=True)
    if sink is not None:
        m = jnp.maximum(m, sink)
    e = jnp.exp(s - m)
    den = jnp.sum(e, axis=-1, keepdims=True)
    if sink is not None:
        den = den + jnp.exp(sink - m)
    return e / den


def _dft(x, cs, ss, cc_bd, sc_bd, scale):
    hi = dict(preferred_element_type=jnp.float32, precision=lax.Precision.HIGHEST)
    xc = jnp.dot(x, cc_bd, **hi)
    xs = jnp.dot(x, sc_bd, **hi)
    return (jnp.dot(cs, xc, **hi) - jnp.dot(ss, xs, **hi)) * scale


def _ctx_branches_kernel(sink_ref, mq_ref, fn_ref, sq_ref, skv_ref, mkv_ref, qn_ref, kvn_ref, wuq_ref, wukv_ref,
                         krsel_ref, ksel_ref, vsel_ref, cs_ref, ss_ref, ccbd_ref, scbd_ref,
                         oa_ref, ob_ref, od_ref, ckv_ref):
    bf16 = jnp.bfloat16
    qn = _rms(mq_ref[...], qn_ref[...]).astype(bf16)
    q = jnp.dot(qn, wuq_ref[...], **_F32)
    q_nope = q[:, :MLA_HEADS * MLA_NOPE]
    q_rope = q[:, MLA_HEADS * MLA_NOPE:]
    mkv = mkv_ref[...]
    ckv = _rms(mkv[:, :MLA_KV_LORA], kvn_ref[...])
    ckv_ref[...] = ckv
    kv = jnp.dot(ckv.astype(bf16), wukv_ref[...], **_F32)
    k_nope = kv[:, :MLA_HEADS * MLA_NOPE].astype(bf16)
    v = kv[:, MLA_HEADS * MLA_NOPE:].astype(bf16)
    k_rope4 = jnp.dot(mkv.astype(bf16), krsel_ref[...], **_F32).astype(bf16)
    o_a = jnp.zeros((SEQ, MLA_HEADS * MLA_V), jnp.float32)
    for h in range(MLA_HEADS):
        mn = _lane_mask(MLA_HEADS * MLA_NOPE, MLA_NOPE, h)
        mr = _lane_mask(MLA_HEADS * MLA_ROPE, MLA_ROPE, h)
        s = (lax.dot_general((q_nope * mn).astype(bf16), k_nope, _NT, **_F32)
             + lax.dot_general((q_rope * mr).astype(bf16), k_rope4, _NT, **_F32)) * MLA_SCALE
        p = _softmax_rows(s)
        o_a = o_a + jnp.dot(p.astype(bf16), v, **_F32) * _lane_mask(MLA_HEADS * MLA_V, MLA_V, h)
    oa_ref[...] = o_a
    ob_ref[...] = _dft(fn_ref[...], cs_ref[...], ss_ref[...], ccbd_ref[...], scbd_ref[...],
                       float((SEQ * FNET_CH) ** -0.5))
    sq = sq_ref[...]
    skv = skv_ref[...].astype(bf16)
    k4 = jnp.dot(skv, ksel_ref[...], **_F32).astype(bf16)
    v4 = jnp.dot(skv, vsel_ref[...], **_F32).astype(bf16)
    o_d = jnp.zeros((SEQ, SWA_HEADS * SWA_HEAD_DIM), jnp.float32)
    for h in range(SWA_HEADS):
        mh = _lane_mask(SWA_HEADS * SWA_HEAD_DIM, SWA_HEAD_DIM, h)
        s = lax.dot_general((sq * mh).astype(bf16), k4, _NT, **_F32) * SWA_SCALE
        p = _softmax_rows(s, sink_ref[h])
        o_d = o_d + jnp.dot(p.astype(bf16), v4, **_F32) * mh
    od_ref[...] = o_d


def dft_mats(n):
    k = np.arange(n)
    ang = 2.0 * np.pi * np.outer(k, k) / n
    return np.cos(ang).astype(np.float32), np.sin(ang).astype(np.float32)


def block_diag(m, reps):
    n = m.shape[0]
    out = np.zeros((n * reps, n * reps), m.dtype)
    for r in range(reps):
        out[r * n:(r + 1) * n, r * n:(r + 1) * n] = m
    return out


def selection(rows, cols, pairs):
    m = np.zeros((rows, cols), np.float32)
    for r, c in pairs:
        m[r, c] = 1.0
    return m


def ctx_branches(hs, n_seq, cols, sink, qn, kvn, wuq_p, wukv_p):
    bf16 = jnp.bfloat16
    cs, ss = dft_mats(SEQ)
    cc, sc = dft_mats(FNET_CH)
    krsel = selection(BRANCH_W, MLA_HEADS * MLA_ROPE,
                      [(MLA_KV_LORA + r, MLA_ROPE * h + r) for h in range(MLA_HEADS) for r in range(MLA_ROPE)])
    kv_w = SWA_KV_HEADS * SWA_HEAD_DIM
    ksel = selection(BRANCH_W, BRANCH_W, [((h // SWA_GROUP) * SWA_HEAD_DIM + r, h * SWA_HEAD_DIM + r)
                                          for h in range(SWA_HEADS) for r in range(SWA_HEAD_DIM)])
    vsel = selection(BRANCH_W, BRANCH_W, [(kv_w + (h // SWA_GROUP) * SWA_HEAD_DIM + r, h * SWA_HEAD_DIM + r)
                                          for h in range(SWA_HEADS) for r in range(SWA_HEAD_DIM)])
    consts = [jnp.asarray(krsel, bf16), jnp.asarray(ksel, bf16), jnp.asarray(vsel, bf16),
              jnp.asarray(cs), jnp.asarray(ss), jnp.asarray(block_diag(cc, FNET_GROUPS)),
              jnp.asarray(block_diag(sc, FNET_GROUPS))]
    col = lambda cb: pl.BlockSpec((SEQ, BRANCH_W), lambda b: (b, cb))
    full = lambda a: pl.BlockSpec(a.shape, lambda b: (0,) * a.ndim)
    weights = [qn, kvn, wuq_p, wukv_p] + consts
    out = lambda rows, w: jax.ShapeDtypeStruct((rows, w), jnp.float32)
    return pl.pallas_call(
        _ctx_branches_kernel,
        grid=(n_seq,),
        in_specs=[pl.BlockSpec(memory_space=pltpu.SMEM)] + [col(cb) for cb in cols] + [full(a) for a in weights],
        out_specs=[pl.BlockSpec((SEQ, BRANCH_W), lambda b: (b, 0))] * 3 + [pl.BlockSpec((SEQ, MLA_KV_LORA), lambda b: (b, 0))],
        out_shape=[out(hs.shape[0], BRANCH_W)] * 3 + [out(n_seq * SEQ, MLA_KV_LORA)],
        compiler_params=pltpu.CompilerParams(dimension_semantics=("arbitrary",), vmem_limit_bytes=VMEM_LIMIT),
        name="ctx_branches",
    )(sink, *[hs] * len(cols), *weights)


def regroup_uq(w_uq):
    w = w_uq.reshape(MLA_Q_LORA, MLA_HEADS, MLA_NOPE + MLA_ROPE)
    return jnp.concatenate([w[:, :, :MLA_NOPE].reshape(MLA_Q_LORA, -1), w[:, :, MLA_NOPE:].reshape(MLA_Q_LORA, -1)], axis=1)


def regroup_ukv(w_ukv):
    w = w_ukv.reshape(MLA_KV_LORA, MLA_HEADS, MLA_NOPE + MLA_V)
    return jnp.concatenate([w[:, :, :MLA_NOPE].reshape(MLA_KV_LORA, -1), w[:, :, MLA_NOPE:].reshape(MLA_KV_LORA, -1)], axis=1)


def rope_tables(n, dim, reps):
    half = dim // 2
    t = np.arange(n)
    freqs = (ROPE_THETA ** (-np.arange(0, half, 2, dtype=np.float32) / half)).astype(np.float32)
    ang = [(t // GRID_W).astype(np.float32)[:, None] * freqs[None, :], (t % GRID_W).astype(np.float32)[:, None] * freqs[None, :]]
    cos = np.concatenate([np.cos(a) for a in ang for _ in range(2)], axis=1)
    sin = np.concatenate([s * np.sin(a) for a in ang for s in (-1.0, 1.0)], axis=1)
    return (jnp.asarray(np.tile(cos, (1, reps)), jnp.float32), jnp.asarray(np.tile(sin, (1, reps)), jnp.float32))


def _rope(x, cos, sin_signed, quarter):
    w = x.shape[-1]
    lane = lax.broadcasted_iota(jnp.int32, x.shape, 1)
    swapped = jnp.where(lane % (2 * quarter) < quarter, pltpu.roll(x, w - quarter, 1), pltpu.roll(x, quarter, 1))
    return x * cos + swapped * sin_signed


def _lat_branches_kernel(sink_ref, mq_ref, sq_ref, fn_ref, skv_ref, mkv_ref, cckv_ref, ckr_ref, csk_ref, csv_ref,
                         qn_ref, kvn_ref, wuq_ref, wukv_ref, krsel_ref, ksel_ref, vsel_ref, ckrsel_ref, csel_ref,
                         cq32_ref, sq32_ref, ck32_ref, sk32_ref, cq64_ref, sq64_ref, ck64_ref, sk64_ref,
                         cs_ref, ss_ref, ccbd_ref, scbd_ref, oa_in, ob_in, od_in,
                         oa_ref, ob_ref, od_ref,
                         kn, vv, kr4, sk4, sv4, ckn, cvv, ckr4, csk4, csv4, xc, xs):
    bf16 = jnp.bfloat16
    j = pl.program_id(1)
    hi = dict(preferred_element_type=jnp.float32, precision=lax.Precision.HIGHEST)

    @pl.when(j == 0)
    def _():
        mkv = mkv_ref[...]
        ckv = _rms(mkv[:, :MLA_KV_LORA], kvn_ref[...])
        kv = jnp.dot(ckv.astype(bf16), wukv_ref[...], **_F32)
        kn[...] = kv[:, :MLA_HEADS * MLA_NOPE].astype(bf16)
        vv[...] = kv[:, MLA_HEADS * MLA_NOPE:].astype(bf16)
        kr = jnp.dot(mkv, krsel_ref[...], **hi)
        kr4[...] = _rope(kr, ck32_ref[...], sk32_ref[...], MLA_ROPE // 4).astype(bf16)
        skv = skv_ref[...]
        k4 = jnp.dot(skv, ksel_ref[...], **hi)
        sk4[...] = _rope(k4, ck64_ref[...], sk64_ref[...], SWA_HEAD_DIM // 4).astype(bf16)
        sv4[...] = jnp.dot(skv.astype(bf16), vsel_ref[...].astype(bf16), **_F32).astype(bf16)
        ckv_c = jnp.dot(cckv_ref[...].astype(bf16), wukv_ref[...], **_F32)
        ckn[...] = ckv_c[:, :MLA_HEADS * MLA_NOPE].astype(bf16)
        cvv[...] = ckv_c[:, MLA_HEADS * MLA_NOPE:].astype(bf16)
        ckr4[...] = jnp.dot(ckr_ref[...].astype(bf16), ckrsel_ref[...], **_F32).astype(bf16)
        for src, dst in ((csk_ref, csk4), (csv_ref, csv4)):
            acc = jnp.zeros(dst.shape, jnp.float32)
            for g in range(SWA_KV_HEADS):
                acc = acc + jnp.dot(src[g].astype(bf16), csel_ref[g], **_F32)
            dst[...] = acc.astype(bf16)
        x = fn_ref[...]
        xc[...] = jnp.dot(x, ccbd_ref[...], **hi)
        xs[...] = jnp.dot(x, scbd_ref[...], **hi)

    qn = _rms(mq_ref[...], qn_ref[...]).astype(bf16)
    q = jnp.dot(qn, wuq_ref[...], **_F32)
    q_nope = q[:, :MLA_HEADS * MLA_NOPE]
    q_rope = _rope(q[:, MLA_HEADS * MLA_NOPE:], cq32_ref[...], sq32_ref[...], MLA_ROPE // 4)
    o_a = jnp.zeros((LAT_QB, MLA_HEADS * MLA_V), jnp.float32)
    for h in range(MLA_HEADS):
        qh_n = (q_nope * _lane_mask(MLA_HEADS * MLA_NOPE, MLA_NOPE, h)).astype(bf16)
        qh_r = (q_rope * _lane_mask(MLA_HEADS * MLA_ROPE, MLA_ROPE, h)).astype(bf16)
        s_c = (lax.dot_general(qh_n, ckn[...], _NT, **_F32) + lax.dot_general(qh_r, ckr4[...], _NT, **_F32)) * MLA_SCALE
        s_l = (lax.dot_general(qh_n, kn[...], _NT, **_F32) + lax.dot_general(qh_r, kr4[...], _NT, **_F32)) * MLA_SCALE
        m = jnp.maximum(jnp.max(s_c, axis=-1, keepdims=True), jnp.max(s_l, axis=-1, keepdims=True))
        e_c, e_l = jnp.exp(s_c - m), jnp.exp(s_l - m)
        inv = 1.0 / (jnp.sum(e_c, axis=-1, keepdims=True) + jnp.sum(e_l, axis=-1, keepdims=True))
        o = jnp.dot((e_c * inv).astype(bf16), cvv[...], **_F32) + jnp.dot((e_l * inv).astype(bf16), vv[...], **_F32)
        o_a = o_a + o * _lane_mask(MLA_HEADS * MLA_V, MLA_V, h)
    oa_ref[...] = o_a
    ob_ref[...] = (jnp.dot(cs_ref[...], xc[...], **hi) - jnp.dot(ss_ref[...], xs[...], **hi)) * float((DEC_SEQ * FNET_CH) ** -0.5)
    sq = _rope(sq_ref[...], cq64_ref[...], sq64_ref[...], SWA_HEAD_DIM // 4)
    band = 3 * SWA_WINDOW
    start = pl.multiple_of(jnp.clip((j - 1) * LAT_QB, 0, DEC_SEQ - band), LAT_QB)
    kb = sk4[pl.ds(start, band), :]
    vb = sv4[pl.ds(start, band), :]
    qpos = j * LAT_QB + lax.broadcasted_iota(jnp.int32, (LAT_QB, band), 0)
    kpos = start + lax.broadcasted_iota(jnp.int32, (LAT_QB, band), 1)
    near = jnp.abs(kpos - qpos) <= SWA_WINDOW
    o_d = jnp.zeros((LAT_QB, SWA_HEADS * SWA_HEAD_DIM), jnp.float32)
    for h in range(SWA_HEADS):
        mh = _lane_mask(SWA_HEADS * SWA_HEAD_DIM, SWA_HEAD_DIM, h)
        qh = (sq * mh).astype(bf16)
        s_b = jnp.where(near, lax.dot_general(qh, kb, _NT, **_F32) * SWA_SCALE, NEG_INF)
        s_c = lax.dot_general(qh, csk4[...], _NT, **_F32) * SWA_SCALE
        sink = sink_ref[h]
        m = jnp.maximum(jnp.maximum(jnp.max(s_b, axis=-1, keepdims=True), jnp.max(s_c, axis=-1, keepdims=True)), sink)
        e_b, e_c = jnp.exp(s_b - m), jnp.exp(s_c - m)
        inv = 1.0 / (jnp.sum(e_b, axis=-1, keepdims=True) + jnp.sum(e_c, axis=-1, keepdims=True) + jnp.exp(sink - m))
        o = jnp.dot((e_b * inv).astype(bf16), vb, **_F32) + jnp.dot((e_c * inv).astype(bf16), csv4[...], **_F32)
        o_d = o_d + o * mh
    od_ref[...] = o_d


def lat_branches(hs, row0, n_seq, cols, sink, qn, kvn, wuq_p, wukv_p, c_ckv, c_krope, c_swa_k, c_swa_v, branch_bufs):
    bf16 = jnp.bfloat16
    f32 = jnp.float32
    nq = DEC_SEQ // LAT_QB
    kv_w = SWA_KV_HEADS * SWA_HEAD_DIM
    krsel = selection(BRANCH_W, MLA_HEADS * MLA_ROPE,
                      [(MLA_KV_LORA + r, MLA_ROPE * h + r) for h in range(MLA_HEADS) for r in range(MLA_ROPE)])
    ksel = selection(BRANCH_W, BRANCH_W, [((h // SWA_GROUP) * SWA_HEAD_DIM + r, h * SWA_HEAD_DIM + r)
                                          for h in range(SWA_HEADS) for r in range(SWA_HEAD_DIM)])
    vsel = selection(BRANCH_W, BRANCH_W, [(kv_w + (h // SWA_GROUP) * SWA_HEAD_DIM + r, h * SWA_HEAD_DIM + r)
                                          for h in range(SWA_HEADS) for r in range(SWA_HEAD_DIM)])
    ckrsel = selection(MLA_ROPE, MLA_HEADS * MLA_ROPE, [(r, MLA_ROPE * h + r) for h in range(MLA_HEADS) for r in range(MLA_ROPE)])
    csel = np.stack([selection(SWA_HEAD_DIM, BRANCH_W, [(r, h * SWA_HEAD_DIM + r) for h in range(SWA_HEADS)
                                                        if h // SWA_GROUP == g for r in range(SWA_HEAD_DIM)])
                     for g in range(SWA_KV_HEADS)])
    c32 = rope_tables(DEC_SEQ, MLA_ROPE, MLA_HEADS)
    c64q = rope_tables(DEC_SEQ, SWA_HEAD_DIM, SWA_HEADS)
    t = jnp.arange(DEC_SEQ, dtype=jnp.int32)
    ang = ((t[:, None] * t[None, :]) % DEC_SEQ).astype(f32) * f32(2.0 * np.pi / DEC_SEQ)
    cs, ss = jnp.cos(ang), jnp.sin(ang)
    cc, sc = dft_mats(FNET_CH)
    rb = row0 // DEC_SEQ
    qb0 = row0 // LAT_QB
    qblk = lambda cb: pl.BlockSpec((LAT_QB, BRANCH_W), lambda b, j: (qb0 + b * nq + j, cb))
    sblk = lambda cb: pl.BlockSpec((DEC_SEQ, BRANCH_W), lambda b, j: (rb + b, cb), pipeline_mode=pl.Buffered(1))
    per_b = lambda a: pl.BlockSpec((None,) + a.shape[1:], lambda b, j: (b,) + (0,) * (a.ndim - 1))
    full = lambda a: pl.BlockSpec(a.shape, lambda b, j: (0,) * a.ndim)
    qtab = lambda a: pl.BlockSpec((LAT_QB, a.shape[1]), lambda b, j: (j, 0))
    consts = [qn, kvn, wuq_p, wukv_p, jnp.asarray(krsel), jnp.asarray(ksel), jnp.asarray(vsel),
              jnp.asarray(ckrsel, bf16), jnp.asarray(csel, bf16)]
    sc_bf = lambda r, w: pltpu.VMEM((r, w), bf16)
    args = [sink, hs, hs, hs, hs, hs, c_ckv, c_krope, c_swa_k, c_swa_v, *consts,
            c32[0], c32[1], c32[0], c32[1], c64q[0], c64q[1], c64q[0], c64q[1], cs, ss,
            jnp.asarray(block_diag(cc, FNET_GROUPS)), jnp.asarray(block_diag(sc, FNET_GROUPS))]
    return pl.pallas_call(
        _lat_branches_kernel,
        grid=(n_seq, nq),
        in_specs=[pl.BlockSpec(memory_space=pltpu.SMEM), qblk(cols[0]), qblk(cols[1]), sblk(cols[2]), sblk(cols[3]),
                  sblk(cols[4]), per_b(c_ckv), per_b(c_krope), per_b(c_swa_k), per_b(c_swa_v)]
                 + [full(a) for a in consts]
                 + [qtab(c32[0]), qtab(c32[1]), full(c32[0]), full(c32[1]),
                    qtab(c64q[0]), qtab(c64q[1]), full(c64q[0]), full(c64q[1]),
                    pl.BlockSpec((LAT_QB, DEC_SEQ), lambda b, j: (j, 0)), pl.BlockSpec((LAT_QB, DEC_SEQ), lambda b, j: (j, 0)),
                    full(jnp.zeros((BRANCH_W, BRANCH_W))), full(jnp.zeros((BRANCH_W, BRANCH_W)))]
                 + [pl.BlockSpec(memory_space=pl.ANY)] * len(branch_bufs),
        input_output_aliases={len(args) + k: k for k in range(len(branch_bufs))},
        out_specs=[pl.BlockSpec((LAT_QB, BRANCH_W), lambda b, j: (qb0 + b * nq + j, 0))] * len(branch_bufs),
        out_shape=[jax.ShapeDtypeStruct(a.shape, a.dtype) for a in branch_bufs],
        scratch_shapes=[sc_bf(DEC_SEQ, 256), sc_bf(DEC_SEQ, 256), sc_bf(DEC_SEQ, 128), sc_bf(DEC_SEQ, 256), sc_bf(DEC_SEQ, 256),
                        sc_bf(PAST_LEN, 256), sc_bf(PAST_LEN, 256), sc_bf(PAST_LEN, 128), sc_bf(PAST_LEN, 256), sc_bf(PAST_LEN, 256),
                        pltpu.VMEM((DEC_SEQ, 256), f32), pltpu.VMEM((DEC_SEQ, 256), f32)],
        compiler_params=pltpu.CompilerParams(dimension_semantics=("arbitrary", "arbitrary"), vmem_limit_bytes=VMEM_LIMIT),
        name="lat_branches",
    )(*args, *branch_bufs)


def _oddeven_merge_sort_pairs(n):
    pairs = []
    p = 1
    while p < n:
        k = p
        while k >= 1:
            for j in range(k % p, n - k, 2 * k):
                for i in range(min(k, n - j - k)):
                    if (i + j) // (p * 2) == (i + j + k) // (p * 2):
                        pairs.append((i + j, i + j + k))
            k //= 2
        p *= 2
    return pairs


def _bitonic_merge_pairs(n):
    pairs = []
    k = n // 2
    while k >= 1:
        pairs += [(i, i + k) for i in range(n) if (i & k) == 0]
        k //= 2
    return pairs


def _compare_exchange(v, pairs):
    for i, j in pairs:
        v[i], v[j] = jnp.maximum(v[i], v[j]), jnp.minimum(v[i], v[j])


def _merge_top(v, shifts):
    nv = len(v)
    dropped = None
    for r in shifts:
        other = [pltpu.roll(v[nv - 1 - i], SUBLANES - r, 0) for i in range(nv)]
        lo = [jnp.minimum(v[i], other[i]) for i in range(nv)]
        v = [jnp.maximum(v[i], other[i]) for i in range(nv)]
        _compare_exchange(v, _bitonic_merge_pairs(nv))
        while len(lo) > 1:
            lo = [jnp.maximum(lo[2 * i], lo[2 * i + 1]) for i in range(len(lo) // 2)]
        d = lo[0]
        if dropped is not None:
            d = jnp.maximum(d, jnp.maximum(dropped, pltpu.roll(dropped, SUBLANES - r, 0)))
        dropped = d
    return v, dropped


def _sorted_top(p):
    nv = PEER_N_KEYS // SUBLANES
    v = [p[SUBLANES * i:SUBLANES * (i + 1)] for i in range(nv)]
    _compare_exchange(v, _oddeven_merge_sort_pairs(nv))
    return _merge_top(v, (4, 2, 1))


def _rank16_17(c):
    v = list(c)
    _compare_exchange(v, _oddeven_merge_sort_pairs(len(v)))
    v = v + [pltpu.roll(t, SUBLANES - 4, 0) for t in reversed(v)]
    _compare_exchange(v, _bitonic_merge_pairs(len(v)))
    v, dropped = _merge_top(v, (2, 1))
    return v[PEER_TOPK - 1][0:1], dropped[0:1]


_INNER_PAIRS = [(i, j) for i in range(1, 8) for j in range(1, 8) if (i + 1) * (j + 1) <= PEER_TOPK + 1]


def _peer_route_kernel(x_ref, sh_ref, sc_ref, wq_ref, keys_ref, ut_ref, th_ref, e1_ref, e2_ref,
                       sc_scr, a_scr, b_scr, a1_scr, ai_scr, bi_scr):
    tt = x_ref.shape[0]
    u = _ln(x_ref[...]) * (1.0 + sc_ref[...]) + sh_ref[...]
    ut = u.T.astype(jnp.bfloat16)
    ut_ref[...] = ut
    qt = jnp.dot(wq_ref[...], ut, preferred_element_type=jnp.float32)
    for hp in range(2 * PEER_HEADS):
        q_hp = qt[hp * PEER_HALF:(hp + 1) * PEER_HALF, :].astype(jnp.bfloat16)
        sc_scr[hp] = jnp.dot(keys_ref[hp], q_hp, preferred_element_type=jnp.float32)

    for scr in (a_scr, b_scr, a1_scr, ai_scr, bi_scr):
        scr[...] = jnp.full(scr.shape, NEG_INF, jnp.float32)

    def per_head(h, carry):
        for tc in range(tt // LANES):
            ls = slice(tc * LANES, (tc + 1) * LANES)
            s1 = sc_scr[2 * h, :, ls]
            s2 = sc_scr[2 * h + 1, :, ls]
            p1 = s1 - jnp.max(s1, axis=0, keepdims=True)
            p2 = s2 - jnp.max(s2, axis=0, keepdims=True)
            a, a16 = _sorted_top(p1)
            b, b16 = _sorted_top(p2)
            for i in range(PEER_TOPK + 1):
                ai = a[i][0:1] if i < PEER_TOPK else a16[0:1]
                bi = b[i][0:1] if i < PEER_TOPK else b16[0:1]
                a_scr[i:i + 1, :] = ai
                b_scr[i:i + 1, :] = bi
                if i >= 1:
                    a1_scr[i - 1:i, :] = ai
                for r, (pi, pj) in enumerate(_INNER_PAIRS):
                    if pi == i:
                        ai_scr[r:r + 1, :] = ai
                    if pj == i:
                        bi_scr[r:r + 1, :] = bi
            bb = b_scr[...]
            cand_tiles = [a_scr[0:1, :] + bb[SUBLANES * k:SUBLANES * (k + 1)] for k in range(3)]
            cand_tiles += [a1_scr[SUBLANES * k:SUBLANES * (k + 1), :] + bb[0:1] for k in range(2)]
            cand_tiles += [ai_scr[SUBLANES * k:SUBLANES * (k + 1), :] + bi_scr[SUBLANES * k:SUBLANES * (k + 1), :]
                           for k in range(3)]
            c16, c17 = _rank16_17(cand_tiles)
            tau = 0.5 * (c16 + c17)
            cand = jnp.concatenate(cand_tiles, axis=0)
            z = jnp.sum(jnp.where(cand >= tau, jnp.exp(cand), 0.0), axis=0, keepdims=True)
            th_ref[h, :, :, ls] = jnp.exp(tau - p1).reshape(PEER_N_KEYS // SUBLANES, SUBLANES, LANES)
            e1_ref[h, :, :, ls] = (0.5 * jnp.exp(p1) / z).reshape(PEER_N_KEYS // SUBLANES, SUBLANES, LANES)
            e2_ref[h, :, ls] = jnp.exp(p2)
        return carry

    lax.fori_loop(0, PEER_HEADS, per_head, 0)


def _peer_dense_kernel(ut_ref, th_ref, e1_ref, e2_ref, u_ref, vt_ref, o_ref, act_ref, gt_ref):
    j = pl.program_id(1)
    eb, tt = act_ref.shape
    n_i1 = eb // PEER_N_KEYS
    assert n_i1 == SUBLANES
    nch = tt // DENSE_CHUNK

    @pl.when(j == 0)
    def _():
        o_ref[...] = jnp.zeros_like(o_ref)

    def mm1(c):
        cs = slice(c * DENSE_CHUNK, (c + 1) * DENSE_CHUNK)
        act_ref[:, cs] = jnp.dot(u_ref[...], ut_ref[:, cs], preferred_element_type=jnp.float32)

    def mm2(c):
        cs = slice(c * DENSE_CHUNK, (c + 1) * DENSE_CHUNK)
        o_ref[:, cs] += jnp.dot(vt_ref[...], gt_ref[:, cs], preferred_element_type=jnp.float32)

    def weights(c):
        for half in range(DENSE_CHUNK // LANES):
            ls = slice(c * DENSE_CHUNK + half * LANES, c * DENSE_CHUNK + (half + 1) * LANES)
            for il in range(n_i1):
                w = None
                for h in range(PEER_HEADS):
                    th_rows = th_ref[h, j, :, ls]
                    e1_rows = e1_ref[h, j, :, ls]
                    e2 = e2_ref[h, :, ls]
                    wh = jnp.where(e2 >= th_rows[il:il + 1], e2 * e1_rows[il:il + 1], 0.0)
                    w = wh if w is None else w + wh
                rs = slice(il * PEER_N_KEYS, (il + 1) * PEER_N_KEYS)
                a = act_ref[rs, ls]
                gt_ref[rs, ls] = (w * (a * (1.0 + lax.erf(a * np.float32(np.sqrt(0.5)))))).astype(jnp.bfloat16)

    mm1(0)
    for c in range(nch):
        if c + 1 < nch:
            mm1(c + 1)
        weights(c)
        if c >= 1:
            mm2(c - 1)
    mm2(nch - 1)


def _residual_ln_kernel(x_ref, g_ref, pt_ref, lng_ref, lnb_ref, y_ref):
    r = DEEPNORM_ALPHA * x_ref[...] + g_ref[...] * pt_ref[...].T
    y_ref[...] = _ln(r) * lng_ref[...] + lnb_ref[...]


def peer_layer(x, mod, tile_cond, wq_t, keys, u_tab, vt_tab, ln_g, ln_b, *, tt_route, tt, eb):
    t, d = x.shape
    nh = PEER_HEADS
    row = lambda k, tsz: pl.BlockSpec((None, None, 1, d), lambda i, *_: (tile_cond(i, tsz), 3 + k, 0, 0))
    rows_shape = (nh, PEER_N_KEYS // SUBLANES, SUBLANES, t)
    tile_shape = (nh, PEER_N_KEYS, t)
    ut, th, e1, e2 = pl.pallas_call(
        _peer_route_kernel,
        grid=(t // tt_route,),
        in_specs=[pl.BlockSpec((tt_route, d), lambda i: (i, 0)), row(0, tt_route), row(1, tt_route),
                  pl.BlockSpec(wq_t.shape, lambda i: (0, 0)),
                  pl.BlockSpec(keys.shape, lambda i: (0, 0, 0))],
        out_specs=[pl.BlockSpec((d, tt_route), lambda i: (0, i)),
                   pl.BlockSpec(rows_shape[:3] + (tt_route,), lambda i: (0, 0, 0, i)),
                   pl.BlockSpec(rows_shape[:3] + (tt_route,), lambda i: (0, 0, 0, i)),
                   pl.BlockSpec(tile_shape[:2] + (tt_route,), lambda i: (0, 0, i))],
        out_shape=[jax.ShapeDtypeStruct((d, t), jnp.bfloat16),
                   jax.ShapeDtypeStruct(rows_shape, jnp.float32), jax.ShapeDtypeStruct(rows_shape, jnp.float32),
                   jax.ShapeDtypeStruct(tile_shape, jnp.float32)],
        scratch_shapes=[pltpu.VMEM((2 * nh, PEER_N_KEYS, tt_route), jnp.float32),
                        pltpu.VMEM((3 * SUBLANES, LANES), jnp.float32), pltpu.VMEM((3 * SUBLANES, LANES), jnp.float32),
                        pltpu.VMEM((2 * SUBLANES, LANES), jnp.float32), pltpu.VMEM((3 * SUBLANES, LANES), jnp.float32),
                        pltpu.VMEM((3 * SUBLANES, LANES), jnp.float32)],
        compiler_params=pltpu.CompilerParams(dimension_semantics=("arbitrary",), vmem_limit_bytes=VMEM_LIMIT),
        name="peer_route",
    )(x, mod, mod, wq_t, keys)

    ne = PEER_N_EXPERTS // eb
    once = dict(pipeline_mode=pl.Buffered(1))
    pt = pl.pallas_call(
        _peer_dense_kernel,
        grid=(t // tt, ne),
        in_specs=[pl.BlockSpec((d, tt), lambda i, j: (0, i)),
                  pl.BlockSpec(rows_shape[:3] + (tt,), lambda i, j: (0, 0, 0, i), **once),
                  pl.BlockSpec(rows_shape[:3] + (tt,), lambda i, j: (0, 0, 0, i), **once),
                  pl.BlockSpec(tile_shape[:2] + (tt,), lambda i, j: (0, 0, i), **once),
                  pl.BlockSpec((eb, d), lambda i, j: (j, 0)),
                  pl.BlockSpec((None, d, eb), lambda i, j: (j, 0, 0))],
        out_specs=pl.BlockSpec((d, tt), lambda i, j: (0, i)),
        out_shape=jax.ShapeDtypeStruct((d, t), jnp.float32),
        scratch_shapes=[pltpu.VMEM((eb, tt), jnp.float32), pltpu.VMEM((eb, tt), jnp.bfloat16)],
        compiler_params=pltpu.CompilerParams(dimension_semantics=("arbitrary", "arbitrary"),
                                             vmem_limit_bytes=VMEM_LIMIT),
        name="peer_dense",
    )(ut, th, e1, e2, u_tab, vt_tab)

    tl = PEER_ROUTE_TILE
    return pl.pallas_call(
        _residual_ln_kernel,
        grid=(t // tl,),
        in_specs=[pl.BlockSpec((tl, d), lambda i: (i, 0)), row(2, tl), pl.BlockSpec((d, tl), lambda i: (0, i)),
                  pl.BlockSpec((1, d), lambda i: (0, 0)), pl.BlockSpec((1, d), lambda i: (0, 0))],
        out_specs=pl.BlockSpec((tl, d), lambda i: (i, 0)),
        out_shape=jax.ShapeDtypeStruct((t, d), jnp.float32),
        compiler_params=pltpu.CompilerParams(dimension_semantics=("arbitrary",), vmem_limit_bytes=VMEM_LIMIT),
        name="peer_residual_ln",
    )(x, mod, pt, ln_g, ln_b)


def _tile_cond(i, tt):
    ctx_tiles = BATCH * SEQ // tt
    return jnp.where(i < ctx_tiles, 0, 1 + (i - ctx_tiles) // (DEC_SEQ // tt))


def kernel(x_prompt, x_sample, c, cache_mla_ckv, cache_mla_krope, cache_swa_k, cache_swa_v, state_gla,
           c_ctx, w_ada, b_ada, w_in, mla_q_norm, w_uq, mla_kv_norm, w_ukv,
           w_gla_a_fwd, b_gla_a_fwd, w_gla_a_bwd, b_gla_a_bwd, gla_norm, swa_sink,
           w_branch, w_out, ln1_g, ln1_b, ln2_g, ln2_b, w_peer_q, peer_keys, peer_u, peer_v):
    bf16 = jnp.bfloat16
    n_ctx, n_lat = BATCH * SEQ, DEC_BATCH * DEC_SEQ
    n_tok = n_ctx + n_lat
    conds = jnp.concatenate([c_ctx[None, :], c], axis=0)
    x = jnp.concatenate([x_prompt.reshape(n_ctx, D_MODEL), x_sample.reshape(n_lat, D_MODEL)], axis=0)
    src = dict(zip(IN_NAMES, (0,) + IN_OFFSETS))
    seg_args = (n_ctx // GLA_SEG, SEQ // GLA_SEG, DEC_SEQ // GLA_SEG)
    lane_block = lambda name, width: HS_OFFSET[name] // width
    gla_cols = (lane_block('gla_q', HK), lane_block('gla_k', HK), lane_block('gla_v', HV),
                HS_OFFSET['gla_af'] // LANES, lane_block('gla_g', HV))
    z_row = lambda name: HS_OFFSET[name] - gla_cols[3] * LANES
    ctx_states = []
    for l in range(DEPTH):
        mod = (jax.nn.silu(conds) @ w_ada[l] + b_ada[l]).reshape(1 + DEC_BATCH, 6, 1, D_MODEL)
        w_parts = jnp.concatenate([w_in[l][:, src[n]:src[n] + _widths[n]] for n in HS_ORDER], axis=1)
        w_parts = jnp.pad(w_parts, ((0, 0), (0, PARTS_WIDTH - GATES_OFFSET))).astype(bf16)
        hs = in_proj(x, mod, _tile_cond, w_parts, tt=512)
        blk = lambda name: HS_OFFSET[name] // BRANCH_W
        mla_w = (mla_q_norm[l][None, :], mla_kv_norm[l][None, :], regroup_uq(w_uq[l]).astype(bf16),
                 regroup_ukv(w_ukv[l]).astype(bf16))
        o_a, o_b, o_d, ckv = ctx_branches(
            hs, BATCH, (blk('mla_q'), blk('fnet'), blk('swa_q'), blk('swa_k'), blk('mla_kv')), swa_sink[l], *mla_w)
        o_a, o_b, o_d = lat_branches(
            hs, n_ctx, DEC_BATCH, (blk('mla_q'), blk('swa_q'), blk('fnet'), blk('swa_k'), blk('mla_kv')), swa_sink[l],
            *mla_w, cache_mla_ckv[:, l], cache_mla_krope[:, l], cache_swa_k[:, l], cache_swa_v[:, l], (o_a, o_b, o_d))
        ctx_part = lambda name, lo, hi: hs[:n_ctx, HS_OFFSET[name] + lo:HS_OFFSET[name] + hi]
        kv_heads = lambda name: (ctx_part(name, 0, SWA_KV_HEADS * SWA_HEAD_DIM)
                                 .reshape(BATCH, SEQ, SWA_KV_HEADS, SWA_HEAD_DIM).transpose(0, 2, 1, 3))
        st = (ckv.reshape(BATCH, SEQ, MLA_KV_LORA),
              ctx_part('mla_kv', MLA_KV_LORA, MLA_KV_LORA + MLA_ROPE).reshape(BATCH, SEQ, MLA_ROPE),
              kv_heads('swa_k'), kv_heads('swa_v'))
        w2 = lambda w, name: (jnp.zeros((LANES, HK), jnp.float32)
                              .at[z_row(name):z_row(name) + GLA_GATE_RANK].set(w).astype(bf16))
        zero = jnp.zeros((BATCH, GLA_HEADS, GLA_DK, GLA_DV), jnp.float32)
        o_c, s_f, s_b = gla_branch(
            hs.reshape(n_tok // GLA_SEG, GLA_SEG, PARTS_WIDTH), gla_cols,
            w2(w_gla_a_fwd[l], 'gla_af'), b_gla_a_fwd[l][None, :], w2(w_gla_a_bwd[l], 'gla_ab'), b_gla_a_bwd[l][None, :],
            jnp.tile(gla_norm[l], GLA_HEADS)[None, :],
            jnp.concatenate([zero, state_gla[:, l, 0]], axis=0), jnp.concatenate([zero, state_gla[:, l, 1]], axis=0),
            seg_args)
        ctx_states.append(st + (jnp.stack([s_f[:BATCH], s_b[:BATCH]], axis=1),))
        branches = (o_a, o_b, o_c.reshape(n_tok, BRANCH_W), o_d)
        x1 = merge(x, mod, _tile_cond, branches, w_in[l][:, GATES_OFFSET:].astype(bf16), w_branch[l].astype(bf16),
                   w_out[l].astype(bf16), ln1_g[l][None, :], ln1_b[l][None, :], tt=256)
        x = peer_layer(
            x1, mod, _tile_cond,
            w_peer_q[l].T.astype(bf16),
            peer_keys[l].reshape(2 * PEER_HEADS, PEER_N_KEYS, PEER_HALF).astype(bf16),
            peer_u[l].astype(bf16),
            peer_v[l].reshape(-1, PEER_EXPERT_BLOCK, D_MODEL).transpose(0, 2, 1).astype(bf16),
            ln2_g[l][None, :], ln2_b[l][None, :],
            tt_route=PEER_ROUTE_TILE, tt=PEER_TOKEN_TILE, eb=PEER_EXPERT_BLOCK)

    h = x[:n_ctx].reshape(BATCH, SEQ, D_MODEL)
    z = x[n_ctx:].reshape(DEC_BATCH, DEC_SEQ, D_MODEL)
    new_mla_ckv = jnp.stack([st[0] for st in ctx_states], axis=1)
    new_mla_krope = jnp.stack([st[1] for st in ctx_states], axis=1)
    new_swa_k = jnp.stack([st[2] for st in ctx_states], axis=1)
    new_swa_v = jnp.stack([st[3] for st in ctx_states], axis=1)
    new_gla_state = jnp.stack([st[4] for st in ctx_states], axis=1)
    return (h, z, new_mla_ckv, new_mla_krope, new_swa_k, new_swa_v, new_gla_state)
```

```python
import functools

import jax
import jax.numpy as jnp
from jax import lax
import numpy as np
from jax.experimental import pallas as pl
from jax.experimental.pallas import tpu as pltpu

D_MODEL = 1024
BATCH = 32
SEQ = 256
DEPTH = 2
DEC_BATCH = 2
DEC_SEQ = 2048
PAST_LEN = 512

GRID_W = 64
N_BRANCH = 4
BRANCH_W = 256
MLA_HEADS = 4
MLA_Q_LORA = 256
MLA_KV_LORA = 128
MLA_NOPE = 64
MLA_ROPE = 32
MLA_V = 64
MLA_SCALE = (MLA_NOPE + MLA_ROPE) ** -0.5
FNET_GROUPS = 4
FNET_CH = BRANCH_W // FNET_GROUPS
GLA_HEADS = 4
GLA_DK = 32
GLA_DV = 64
GLA_GATE_RANK = 16
GLA_TAU = 16.0
GLA_CHUNK = 64
SWA_HEADS = 4
SWA_KV_HEADS = 2
SWA_GROUP = SWA_HEADS // SWA_KV_HEADS
SWA_HEAD_DIM = 64
SWA_WINDOW = 128
SWA_SCALE = SWA_HEAD_DIM ** -0.5
ATTN_BLOCK = 128
PEER_HEADS = 8
PEER_N_KEYS = 128
PEER_N_EXPERTS = PEER_N_KEYS * PEER_N_KEYS
PEER_KEY_DIM = 256
PEER_HALF = PEER_KEY_DIM // 2
PEER_TOPK = 16
PEER_ROUTE_TILE = 512
PEER_TOKEN_TILE = 1024
PEER_EXPERT_BLOCK = 8 * PEER_N_KEYS
DENSE_CHUNK = 256
GLA_SEG = 128
GLA_STEPS = 8
GLA_CHAIN_GROUP = 2
HK = GLA_HEADS * GLA_DK
HV = GLA_HEADS * GLA_DV
LAT_QB = 2 * SWA_WINDOW

ROPE_THETA = 10000.0
NORM_EPS = 1e-6
DEEPNORM_ALPHA = (2.0 * DEPTH) ** 0.25
DEEPNORM_BETA = (8.0 * DEPTH) ** -0.25

IN_SPLITS = (
    ('mla_q', MLA_Q_LORA),
    ('mla_kv', MLA_KV_LORA + MLA_ROPE),
    ('fnet', BRANCH_W),
    ('gla_q', GLA_HEADS * GLA_DK),
    ('gla_k', GLA_HEADS * GLA_DK),
    ('gla_v', GLA_HEADS * GLA_DV),
    ('gla_g', BRANCH_W),
    ('gla_af', GLA_GATE_RANK),
    ('gla_ab', GLA_GATE_RANK),
    ('swa_q', SWA_HEADS * SWA_HEAD_DIM),
    ('swa_k', SWA_KV_HEADS * SWA_HEAD_DIM),
    ('swa_v', SWA_KV_HEADS * SWA_HEAD_DIM),
    ('gates', N_BRANCH * D_MODEL),
)
IN_NAMES = tuple(n for n, _ in IN_SPLITS)
IN_OFFSETS = tuple(int(o) for o in np.cumsum([w for _, w in IN_SPLITS])[:-1])
IN_WIDTH = int(sum(w for _, w in IN_SPLITS))
GATES_OFFSET = IN_OFFSETS[-1]
HS_ORDER = ('mla_q', 'fnet', 'gla_q', 'gla_k', 'gla_v', 'gla_g', 'swa_q', 'swa_k', 'swa_v', 'mla_kv', 'gla_af', 'gla_ab')
_widths = dict(IN_SPLITS)
HS_OFFSET = {n: int(o) for n, o in zip(HS_ORDER, np.cumsum([0] + [_widths[n] for n in HS_ORDER])[:-1])}
PARTS_WIDTH = -(-GATES_OFFSET // 128) * 128

LANES = 128
SUBLANES = 8
NEG_INF = float('-inf')
VMEM_LIMIT = 56 * 1024 * 1024


def _ln(x):
    mu = jnp.mean(x, -1, keepdims=True)
    xc = x - mu
    var = jnp.mean(xc * xc, -1, keepdims=True)
    return xc * lax.rsqrt(var + NORM_EPS)


def _in_proj_kernel(x_ref, sh_ref, sc_ref, w_ref, o_ref):
    u = _ln(x_ref[...]) * (1.0 + sc_ref[...]) + sh_ref[...]
    o_ref[...] = jnp.dot(u.astype(jnp.bfloat16), w_ref[...], preferred_element_type=jnp.float32)


def in_proj(x, mod, tile_cond, w, *, tt):
    t, d = x.shape
    n = w.shape[1]
    row = lambda k: pl.BlockSpec((None, None, 1, d), lambda i: (tile_cond(i, tt), k, 0, 0))
    return pl.pallas_call(
        _in_proj_kernel,
        grid=(t // tt,),
        in_specs=[pl.BlockSpec((tt, d), lambda i: (i, 0)), row(0), row(1), pl.BlockSpec((d, n), lambda i: (0, 0))],
        out_specs=pl.BlockSpec((tt, n), lambda i: (i, 0)),
        out_shape=jax.ShapeDtypeStruct((t, n), jnp.float32),
        compiler_params=pltpu.CompilerParams(dimension_semantics=("arbitrary",), vmem_limit_bytes=VMEM_LIMIT),
        name="in_proj",
    )(x, mod, mod, w)


def _merge_kernel(x_ref, sh_ref, sc_ref, g_ref, ba_ref, bb_ref, bc_ref, bd_ref, wg_ref, wb_ref, wo_ref,
                  lng_ref, lnb_ref, y_ref):
    x = x_ref[...]
    u = (_ln(x) * (1.0 + sc_ref[...]) + sh_ref[...]).astype(jnp.bfloat16)
    acc = jnp.zeros(x.shape, jnp.float32)
    for b, br_ref in enumerate((ba_ref, bb_ref, bc_ref, bd_ref)):
        gate = jnp.dot(u, wg_ref[:, b * D_MODEL:(b + 1) * D_MODEL], preferred_element_type=jnp.float32)
        proj = jnp.dot(br_ref[...].astype(jnp.bfloat16), wb_ref[b], preferred_element_type=jnp.float32)
        acc = acc + jax.nn.sigmoid(gate) * proj
    mix = jnp.dot(acc.astype(jnp.bfloat16), wo_ref[...], preferred_element_type=jnp.float32)
    y_ref[...] = _ln(DEEPNORM_ALPHA * x + g_ref[...] * mix) * lng_ref[...] + lnb_ref[...]


def merge(x, mod, tile_cond, branches, w_gates, w_branch, w_out, ln_g, ln_b, *, tt):
    t, d = x.shape
    row = lambda k: pl.BlockSpec((None, None, 1, d), lambda i: (tile_cond(i, tt), k, 0, 0))
    full = lambda a: pl.BlockSpec(a.shape, lambda i: (0,) * a.ndim)
    return pl.pallas_call(
        _merge_kernel,
        grid=(t // tt,),
        in_specs=[pl.BlockSpec((tt, d), lambda i: (i, 0)), row(0), row(1), row(2),
                  *[pl.BlockSpec((tt, BRANCH_W), lambda i: (i, 0)) for _ in range(N_BRANCH)],
                  full(w_gates), full(w_branch), full(w_out), full(ln_g), full(ln_b)],
        out_specs=pl.BlockSpec((tt, d), lambda i: (i, 0)),
        out_shape=jax.ShapeDtypeStruct((t, d), jnp.float32),
        compiler_params=pltpu.CompilerParams(dimension_semantics=("arbitrary",), vmem_limit_bytes=VMEM_LIMIT),
        name="merge",
    )(x, mod, mod, mod, *branches, w_gates, w_branch, w_out, ln_g, ln_b)


def _gla_local_kernel(q_ref, k_ref, v_ref, z_ref, w2_ref, b2_ref, o_ref, qd_ref, st_ref, dl_ref,
                      s_scr, d_scr, stage, ta, tk, tq, *, reverse):
    i = pl.program_id(0)
    nseg = q_ref.shape[0]

    @pl.when(i == 0)
    def _():
        s_scr[...] = jnp.zeros_like(s_scr)
        d_scr[...] = jnp.ones_like(d_scr)
        stage[...] = jnp.zeros_like(stage)

    def to_lanes(x):
        stage[0:nseg, :] = x
        return stage[...].T

    def from_lanes(xt):
        return xt.T[0:nseg, :]

    steps = range(GLA_STEPS - 1, -1, -1) if reverse else range(GLA_STEPS)
    for j in steps:
        z = jnp.dot(z_ref[:, j, :].astype(jnp.bfloat16), w2_ref[...], preferred_element_type=jnp.float32) + b2_ref[...]
        ta[...] = to_lanes(jnp.exp(jax.nn.log_sigmoid(z) / GLA_TAU))
        tq[...] = to_lanes(q_ref[:, j, :] * (GLA_DK ** -0.5))
        tk[...] = to_lanes(k_ref[:, j, :])
        d = d_scr[...] * ta[...]
        d_scr[...] = d
        qd_ref[:, j, :] = from_lanes(tq[...] * d)
        vt = [to_lanes(v_ref[:, j, LANES * c:LANES * (c + 1)]) for c in range(HV // LANES)]
        outs = []
        for h in range(GLA_HEADS):
            off = (h * GLA_DV) % LANES
            vh = vt[(h * GLA_DV) // LANES][off:off + GLA_DV]

            def rows(g, acc, h=h, vh=vh):
                r0 = pl.multiple_of(h * GLA_DK + g * SUBLANES, SUBLANES)
                a8 = ta[pl.ds(r0, SUBLANES), :]
                k8 = tk[pl.ds(r0, SUBLANES), :]
                q8 = tq[pl.ds(r0, SUBLANES), :]
                for r in range(SUBLANES):
                    s_new = a8[r:r + 1] * s_scr[r0 + r] + k8[r:r + 1] * vh
                    s_scr[r0 + r] = s_new
                    acc = acc + q8[r:r + 1] * s_new
                return acc

            outs.append(lax.fori_loop(0, GLA_DK // SUBLANES, rows, jnp.zeros((GLA_DV, LANES), jnp.float32)))
        for c in range(HV // LANES):
            per = LANES // GLA_DV
            o_ref[:, j, LANES * c:LANES * (c + 1)] = from_lanes(jnp.concatenate(outs[per * c:per * (c + 1)], axis=0))

    @pl.when(i == pl.num_programs(0) - 1)
    def _():
        st_ref[...] = s_scr[...]
        dl_ref[...] = d_scr[...].T


def gla_local(hs3, w2pad, b2, cols, *, reverse):
    nseg, seg, _ = hs3.shape
    nb = seg // GLA_STEPS
    tmap = (lambda i: nb - 1 - i) if reverse else (lambda i: i)
    blk = lambda width, cb: pl.BlockSpec((nseg, GLA_STEPS, width), lambda i: (0, tmap(i), cb))
    t3 = lambda width: jax.ShapeDtypeStruct((nseg, seg, width), jnp.float32)
    return pl.pallas_call(
        functools.partial(_gla_local_kernel, reverse=reverse),
        grid=(nb,),
        in_specs=[blk(HK, cols[0]), blk(HK, cols[1]), blk(HV, cols[2]), blk(LANES, cols[3]),
                  pl.BlockSpec((LANES, HK), lambda i: (0, 0)), pl.BlockSpec((1, HK), lambda i: (0, 0))],
        out_specs=[pl.BlockSpec((nseg, GLA_STEPS, HV), lambda i: (0, tmap(i), 0)),
                   pl.BlockSpec((nseg, GLA_STEPS, HK), lambda i: (0, tmap(i), 0)),
                   pl.BlockSpec((HK, GLA_DV, LANES), lambda i: (0, 0, 0)),
                   pl.BlockSpec((LANES, HK), lambda i: (0, 0))],
        out_shape=[t3(HV), t3(HK), jax.ShapeDtypeStruct((HK, GLA_DV, LANES), jnp.float32),
                   jax.ShapeDtypeStruct((LANES, HK), jnp.float32)],
        scratch_shapes=[pltpu.VMEM((HK, GLA_DV, LANES), jnp.float32), pltpu.VMEM((HK, LANES), jnp.float32),
                        pltpu.VMEM((LANES, LANES), jnp.float32), pltpu.VMEM((HK, LANES), jnp.float32),
                        pltpu.VMEM((HK, LANES), jnp.float32), pltpu.VMEM((HK, LANES), jnp.float32)],
        compiler_params=pltpu.CompilerParams(dimension_semantics=("arbitrary",), vmem_limit_bytes=VMEM_LIMIT),
        name="gla_local_bwd" if reverse else "gla_local_fwd",
    )(hs3, hs3, hs3, hs3, w2pad, b2)


def _seg_seq(c, n_ctx_seg, ctx_per, lat_per):
    return jnp.where(c < n_ctx_seg, c // ctx_per, n_ctx_seg // ctx_per + (c - n_ctx_seg) // lat_per)


def _seg_pos(c, n_ctx_seg, ctx_per, lat_per):
    return (jnp.where(c < n_ctx_seg, c % ctx_per, (c - n_ctx_seg) % lat_per),
            jnp.where(c < n_ctx_seg, ctx_per, lat_per))


def _gla_chain_kernel(*refs, reverse, combine, seg_args):
    if combine:
        ol_ref, qd_ref, sl_ref, dl_ref, s0_ref, of_ref, gate_ref, gn_ref, o_ref, fin_ref, st = refs
    else:
        ol_ref, qd_ref, sl_ref, dl_ref, s0_ref, o_ref, fin_ref, st = refs
    i = pl.program_id(0)
    grp = pl.num_programs(0) - 1 - i if reverse else i
    row_head = lax.broadcasted_iota(jnp.int32, (HV, HK), 0) // GLA_DV
    col_head = lax.broadcasted_iota(jnp.int32, (HV, HK), 1) // GLA_DK
    for g in (range(GLA_CHAIN_GROUP - 1, -1, -1) if reverse else range(GLA_CHAIN_GROUP)):
        pos, per = _seg_pos(grp * GLA_CHAIN_GROUP + g, *seg_args)
        first = (pos == per - 1) if reverse else (pos == 0)

        @pl.when(first)
        def _():
            st[...] = s0_ref[...]

        s_in = st[...]
        o = ol_ref[g] + lax.dot_general(qd_ref[g].astype(jnp.bfloat16), s_in.astype(jnp.bfloat16),
                                        (((1,), (1,)), ((), ())), preferred_element_type=jnp.float32)
        s_loc = jnp.where(row_head == col_head, jnp.concatenate([sl_ref[g]] * GLA_HEADS, axis=0), 0.0)
        s_new = dl_ref[g] * s_in + s_loc
        st[...] = s_new
        fin_ref[...] = s_new
        if combine:
            x = o + of_ref[g]
            lane_head = lax.broadcasted_iota(jnp.int32, x.shape, 1) // GLA_DV
            x2 = x * x
            scale = jnp.zeros_like(x)
            for h in range(GLA_HEADS):
                ms = jnp.sum(jnp.where(lane_head == h, x2, 0.0), axis=-1, keepdims=True) * (1.0 / GLA_DV)
                scale = jnp.where(lane_head == h, lax.rsqrt(ms + NORM_EPS), scale)
            o = x * scale * gn_ref[...] * jax.nn.silu(gate_ref[g])
        o_ref[g] = o


def gla_chain(o_loc, qd, sl_t, dl, s0_t, seg_args, *, reverse, o_fwd=None, hs3=None, gate_col=None, gnorm=None):
    nseg, seg, _ = o_loc.shape
    n_seq = s0_t.shape[0]
    gg = GLA_CHAIN_GROUP
    assert all(n % gg == 0 for n in seg_args)
    ngrp = nseg // gg
    cmap = (lambda i: ngrp - 1 - i) if reverse else (lambda i: i)
    seq = lambda i: _seg_seq(cmap(i) * gg, *seg_args)
    combine = o_fwd is not None
    in_specs = [pl.BlockSpec((gg, seg, HV), lambda i: (cmap(i), 0, 0)),
                pl.BlockSpec((gg, seg, HK), lambda i: (cmap(i), 0, 0)),
                pl.BlockSpec((gg, GLA_DV, HK), lambda i: (cmap(i), 0, 0)),
                pl.BlockSpec((gg, 1, HK), lambda i: (cmap(i), 0, 0)),
                pl.BlockSpec((None, HV, HK), lambda i: (seq(i), 0, 0))]
    args = [o_loc, qd, sl_t, dl.reshape(dl.shape[0], 1, HK), s0_t]
    if combine:
        in_specs += [pl.BlockSpec((gg, seg, HV), lambda i: (cmap(i), 0, 0)),
                     pl.BlockSpec((gg, seg, HV), lambda i: (cmap(i), 0, gate_col)),
                     pl.BlockSpec((1, HV), lambda i: (0, 0))]
        args += [o_fwd, hs3, gnorm]
    return pl.pallas_call(
        functools.partial(_gla_chain_kernel, reverse=reverse, combine=combine, seg_args=seg_args),
        grid=(ngrp,),
        in_specs=in_specs,
        out_specs=[pl.BlockSpec((gg, seg, HV), lambda i: (cmap(i), 0, 0)),
                   pl.BlockSpec((None, HV, HK), lambda i: (seq(i), 0, 0))],
        out_shape=[jax.ShapeDtypeStruct((nseg, seg, HV), jnp.float32),
                   jax.ShapeDtypeStruct((n_seq, HV, HK), jnp.float32)],
        scratch_shapes=[pltpu.VMEM((HV, HK), jnp.float32)],
        compiler_params=pltpu.CompilerParams(dimension_semantics=("arbitrary",), vmem_limit_bytes=VMEM_LIMIT),
        name="gla_chain_bwd" if reverse else "gla_chain_fwd",
    )(*args)


def _state_to_chain(s):
    n = s.shape[0]
    eye = jnp.eye(GLA_HEADS, dtype=s.dtype)
    return jnp.einsum('nhkv,hg->nhvgk', s, eye).reshape(n, HV, HK)


def _state_from_chain(f):
    n = f.shape[0]
    f5 = f.reshape(n, GLA_HEADS, GLA_DV, GLA_HEADS, GLA_DK)
    return jnp.stack([f5[:, h, :, h, :] for h in range(GLA_HEADS)], axis=1).transpose(0, 1, 3, 2)


def gla_branch(hs3, cols, w2f, b2f, w2b, b2b, gnorm, s0_fwd, s0_bwd, seg_args):
    nseg = hs3.shape[0]
    o_f = None
    for reverse, w2, b2, s0 in ((False, w2f, b2f, s0_fwd), (True, w2b, b2b, s0_bwd)):
        o_loc, qd, st, dl = gla_local(hs3, w2, b2, cols[:4], reverse=reverse)
        sl_t = jnp.transpose(st, (2, 1, 0))[:nseg]
        if not reverse:
            o_f, fin_f = gla_chain(o_loc, qd, sl_t, dl[:nseg], _state_to_chain(s0), seg_args, reverse=False)
        else:
            o, fin_b = gla_chain(o_loc, qd, sl_t, dl[:nseg], _state_to_chain(s0), seg_args, reverse=True,
                                 o_fwd=o_f, hs3=hs3, gate_col=cols[4], gnorm=gnorm)
    return o, _state_from_chain(fin_f), _state_from_chain(fin_b)


_NT = (((1,), (1,)), ((), ()))
_F32 = dict(preferred_element_type=jnp.float32)


def _rms(x, g):
    return x * lax.rsqrt(jnp.mean(x * x, -1, keepdims=True) + NORM_EPS) * g


def _lane_mask(width, seg, h):
    lane = lax.broadcasted_iota(jnp.int32, (1, width), 1)
    return (lane // seg == h).astype(jnp.float32)


def _softmax_rows(s, sink=None):
    m = jnp.max(s, axis=-1, keepdims=True)
    if sink is not None:
        m = jnp.maximum(m, sink)
    e = jnp.exp(s - m)
    den = jnp.sum(e, axis=-1, keepdims=True)
    if sink is not None:
        den = den + jnp.exp(sink - m)
    return e / den


def _dft(x, cs, ss, cc_bd, sc_bd, scale):
    hi = dict(preferred_element_type=jnp.float32, precision=lax.Precision.HIGHEST)
    xc = jnp.dot(x, cc_bd, **hi)
    xs = jnp.dot(x, sc_bd, **hi)
    return (jnp.dot(cs, xc, **hi) - jnp.dot(ss, xs, **hi)) * scale


def _ctx_branches_kernel(sink_ref, mq_ref, fn_ref, sq_ref, skv_ref, mkv_ref, qn_ref, kvn_ref, wuq_ref, wukv_ref,
                         krsel_ref, ksel_ref, vsel_ref, cs_ref, ss_ref, ccbd_ref, scbd_ref,
                         oa_ref, ob_ref, od_ref, ckv_ref):
    bf16 = jnp.bfloat16
    qn = _rms(mq_ref[...], qn_ref[...]).astype(bf16)
    q = jnp.dot(qn, wuq_ref[...], **_F32)
    q_nope = q[:, :MLA_HEADS * MLA_NOPE]
    q_rope = q[:, MLA_HEADS * MLA_NOPE:]
    mkv = mkv_ref[...]
    ckv = _rms(mkv[:, :MLA_KV_LORA], kvn_ref[...])
    ckv_ref[...] = ckv
    kv = jnp.dot(ckv.astype(bf16), wukv_ref[...], **_F32)
    k_nope = kv[:, :MLA_HEADS * MLA_NOPE].astype(bf16)
    v = kv[:, MLA_HEADS * MLA_NOPE:].astype(bf16)
    k_rope4 = jnp.dot(mkv.astype(bf16), krsel_ref[...], **_F32).astype(bf16)
    o_a = jnp.zeros((SEQ, MLA_HEADS * MLA_V), jnp.float32)
    for h in range(MLA_HEADS):
        mn = _lane_mask(MLA_HEADS * MLA_NOPE, MLA_NOPE, h)
        mr = _lane_mask(MLA_HEADS * MLA_ROPE, MLA_ROPE, h)
        s = (lax.dot_general((q_nope * mn).astype(bf16), k_nope, _NT, **_F32)
             + lax.dot_general((q_rope * mr).astype(bf16), k_rope4, _NT, **_F32)) * MLA_SCALE
        p = _softmax_rows(s)
        o_a = o_a + jnp.dot(p.astype(bf16), v, **_F32) * _lane_mask(MLA_HEADS * MLA_V, MLA_V, h)
    oa_ref[...] = o_a
    ob_ref[...] = _dft(fn_ref[...], cs_ref[...], ss_ref[...], ccbd_ref[...], scbd_ref[...],
                       float((SEQ * FNET_CH) ** -0.5))
    sq = sq_ref[...]
    skv = skv_ref[...].astype(bf16)
    k4 = jnp.dot(skv, ksel_ref[...], **_F32).astype(bf16)
    v4 = jnp.dot(skv, vsel_ref[...], **_F32).astype(bf16)
    o_d = jnp.zeros((SEQ, SWA_HEADS * SWA_HEAD_DIM), jnp.float32)
    for h in range(SWA_HEADS):
        mh = _lane_mask(SWA_HEADS * SWA_HEAD_DIM, SWA_HEAD_DIM, h)
        s = lax.dot_general((sq * mh).astype(bf16), k4, _NT, **_F32) * SWA_SCALE
        p = _softmax_rows(s, sink_ref[h])
        o_d = o_d + jnp.dot(p.astype(bf16), v4, **_F32) * mh
    od_ref[...] = o_d


def dft_mats(n):
    k = np.arange(n)
    ang = 2.0 * np.pi * np.outer(k, k) / n
    return np.cos(ang).astype(np.float32), np.sin(ang).astype(np.float32)


def block_diag(m, reps):
    n = m.shape[0]
    out = np.zeros((n * reps, n * reps), m.dtype)
    for r in range(reps):
        out[r * n:(r + 1) * n, r * n:(r + 1) * n] = m
    return out


def selection(rows, cols, pairs):
    m = np.zeros((rows, cols), np.float32)
    for r, c in pairs:
        m[r, c] = 1.0
    return m


def ctx_branches(hs, n_seq, cols, sink, qn, kvn, wuq_p, wukv_p):
    bf16 = jnp.bfloat16
    cs, ss = dft_mats(SEQ)
    cc, sc = dft_mats(FNET_CH)
    krsel = selection(BRANCH_W, MLA_HEADS * MLA_ROPE,
                      [(MLA_KV_LORA + r, MLA_ROPE * h + r) for h in range(MLA_HEADS) for r in range(MLA_ROPE)])
    kv_w = SWA_KV_HEADS * SWA_HEAD_DIM
    ksel = selection(BRANCH_W, BRANCH_W, [((h // SWA_GROUP) * SWA_HEAD_DIM + r, h * SWA_HEAD_DIM + r)
                                          for h in range(SWA_HEADS) for r in range(SWA_HEAD_DIM)])
    vsel = selection(BRANCH_W, BRANCH_W, [(kv_w + (h // SWA_GROUP) * SWA_HEAD_DIM + r, h * SWA_HEAD_DIM + r)
                                          for h in range(SWA_HEADS) for r in range(SWA_HEAD_DIM)])
    consts = [jnp.asarray(krsel, bf16), jnp.asarray(ksel, bf16), jnp.asarray(vsel, bf16),
              jnp.asarray(cs), jnp.asarray(ss), jnp.asarray(block_diag(cc, FNET_GROUPS)),
              jnp.asarray(block_diag(sc, FNET_GROUPS))]
    col = lambda cb: pl.BlockSpec((SEQ, BRANCH_W), lambda b: (b, cb))
    full = lambda a: pl.BlockSpec(a.shape, lambda b: (0,) * a.ndim)
    weights = [qn, kvn, wuq_p, wukv_p] + consts
    out = lambda rows, w: jax.ShapeDtypeStruct((rows, w), jnp.float32)
    return pl.pallas_call(
        _ctx_branches_kernel,
        grid=(n_seq,),
        in_specs=[pl.BlockSpec(memory_space=pltpu.SMEM)] + [col(cb) for cb in cols] + [full(a) for a in weights],
        out_specs=[pl.BlockSpec((SEQ, BRANCH_W), lambda b: (b, 0))] * 3 + [pl.BlockSpec((SEQ, MLA_KV_LORA), lambda b: (b, 0))],
        out_shape=[out(hs.shape[0], BRANCH_W)] * 3 + [out(n_seq * SEQ, MLA_KV_LORA)],
        compiler_params=pltpu.CompilerParams(dimension_semantics=("arbitrary",), vmem_limit_bytes=VMEM_LIMIT),
        name="ctx_branches",
    )(sink, *[hs] * len(cols), *weights)


def regroup_uq(w_uq):
    w = w_uq.reshape(MLA_Q_LORA, MLA_HEADS, MLA_NOPE + MLA_ROPE)
    return jnp.concatenate([w[:, :, :MLA_NOPE].reshape(MLA_Q_LORA, -1), w[:, :, MLA_NOPE:].reshape(MLA_Q_LORA, -1)], axis=1)


def regroup_ukv(w_ukv):
    w = w_ukv.reshape(MLA_KV_LORA, MLA_HEADS, MLA_NOPE + MLA_V)
    return jnp.concatenate([w[:, :, :MLA_NOPE].reshape(MLA_KV_LORA, -1), w[:, :, MLA_NOPE:].reshape(MLA_KV_LORA, -1)], axis=1)


def rope_tables(n, dim, reps):
    half = dim // 2
    t = np.arange(n)
    freqs = (ROPE_THETA ** (-np.arange(0, half, 2, dtype=np.float32) / half)).astype(np.float32)
    ang = [(t // GRID_W).astype(np.float32)[:, None] * freqs[None, :], (t % GRID_W).astype(np.float32)[:, None] * freqs[None, :]]
    cos = np.concatenate([np.cos(a) for a in ang for _ in range(2)], axis=1)
    sin = np.concatenate([s * np.sin(a) for a in ang for s in (-1.0, 1.0)], axis=1)
    return (jnp.asarray(np.tile(cos, (1, reps)), jnp.float32), jnp.asarray(np.tile(sin, (1, reps)), jnp.float32))


def _rope(x, cos, sin_signed, quarter):
    w = x.shape[-1]
    lane = lax.broadcasted_iota(jnp.int32, x.shape, 1)
    swapped = jnp.where(lane % (2 * quarter) < quarter, pltpu.roll(x, w - quarter, 1), pltpu.roll(x, quarter, 1))
    return x * cos + swapped * sin_signed


def _lat_branches_kernel(sink_ref, mq_ref, sq_ref, fn_ref, skv_ref, mkv_ref, cckv_ref, ckr_ref, csk_ref, csv_ref,
                         qn_ref, kvn_ref, wuq_ref, wukv_ref, krsel_ref, ksel_ref, vsel_ref, ckrsel_ref, csel_ref,
                         cq32_ref, sq32_ref, ck32_ref, sk32_ref, cq64_ref, sq64_ref, ck64_ref, sk64_ref,
                         cs_ref, ss_ref, ccbd_ref, scbd_ref, oa_in, ob_in, od_in,
                         oa_ref, ob_ref, od_ref,
                         kn, vv, kr4, sk4, sv4, ckn, cvv, ckr4, csk4, csv4, xc, xs):
    bf16 = jnp.bfloat16
    j = pl.program_id(1)
    hi = dict(preferred_element_type=jnp.float32, precision=lax.Precision.HIGHEST)

    @pl.when(j == 0)
    def _():
        mkv = mkv_ref[...]
        ckv = _rms(mkv[:, :MLA_KV_LORA], kvn_ref[...])
        kv = jnp.dot(ckv.astype(bf16), wukv_ref[...], **_F32)
        kn[...] = kv[:, :MLA_HEADS * MLA_NOPE].astype(bf16)
        vv[...] = kv[:, MLA_HEADS * MLA_NOPE:].astype(bf16)
        kr = jnp.dot(mkv, krsel_ref[...], **hi)
        kr4[...] = _rope(kr, ck32_ref[...], sk32_ref[...], MLA_ROPE // 4).astype(bf16)
        skv = skv_ref[...]
        k4 = jnp.dot(skv, ksel_ref[...], **hi)
        sk4[...] = _rope(k4, ck64_ref[...], sk64_ref[...], SWA_HEAD_DIM // 4).astype(bf16)
        sv4[...] = jnp.dot(skv.astype(bf16), vsel_ref[...].astype(bf16), **_F32).astype(bf16)
        ckv_c = jnp.dot(cckv_ref[...].astype(bf16), wukv_ref[...], **_F32)
        ckn[...] = ckv_c[:, :MLA_HEADS * MLA_NOPE].astype(bf16)
        cvv[...] = ckv_c[:, MLA_HEADS * MLA_NOPE:].astype(bf16)
        ckr4[...] = jnp.dot(ckr_ref[...].astype(bf16), ckrsel_ref[...], **_F32).astype(bf16)
        for src, dst in ((csk_ref, csk4), (csv_ref, csv4)):
            acc = jnp.zeros(dst.shape, jnp.float32)
            for g in range(SWA_KV_HEADS):
                acc = acc + jnp.dot(src[g].astype(bf16), csel_ref[g], **_F32)
            dst[...] = acc.astype(bf16)
        x = fn_ref[...]
        xc[...] = jnp.dot(x, ccbd_ref[...], **hi)
        xs[...] = jnp.dot(x, scbd_ref[...], **hi)

    qn = _rms(mq_ref[...], qn_ref[...]).astype(bf16)
    q = jnp.dot(qn, wuq_ref[...], **_F32)
    q_nope = q[:, :MLA_HEADS * MLA_NOPE]
    q_rope = _rope(q[:, MLA_HEADS * MLA_NOPE:], cq32_ref[...], sq32_ref[...], MLA_ROPE // 4)
    o_a = jnp.zeros((LAT_QB, MLA_HEADS * MLA_V), jnp.float32)
    for h in range(MLA_HEADS):
        qh_n = (q_nope * _lane_mask(MLA_HEADS * MLA_NOPE, MLA_NOPE, h)).astype(bf16)
        qh_r = (q_rope * _lane_mask(MLA_HEADS * MLA_ROPE, MLA_ROPE, h)).astype(bf16)
        s_c = (lax.dot_general(qh_n, ckn[...], _NT, **_F32) + lax.dot_general(qh_r, ckr4[...], _NT, **_F32)) * MLA_SCALE
        s_l = (lax.dot_general(qh_n, kn[...], _NT, **_F32) + lax.dot_general(qh_r, kr4[...], _NT, **_F32)) * MLA_SCALE
        m = jnp.maximum(jnp.max(s_c, axis=-1, keepdims=True), jnp.max(s_l, axis=-1, keepdims=True))
        e_c, e_l = jnp.exp(s_c - m), jnp.exp(s_l - m)
        inv = 1.0 / (jnp.sum(e_c, axis=-1, keepdims=True) + jnp.sum(e_l, axis=-1, keepdims=True))
        o = jnp.dot((e_c * inv).astype(bf16), cvv[...], **_F32) + jnp.dot((e_l * inv).astype(bf16), vv[...], **_F32)
        o_a = o_a + o * _lane_mask(MLA_HEADS * MLA_V, MLA_V, h)
    oa_ref[...] = o_a
    ob_ref[...] = (jnp.dot(cs_ref[...], xc[...], **hi) - jnp.dot(ss_ref[...], xs[...], **hi)) * float((DEC_SEQ * FNET_CH) ** -0.5)
    sq = _rope(sq_ref[...], cq64_ref[...], sq64_ref[...], SWA_HEAD_DIM // 4)
    band = LAT_QB + 2 * SWA_WINDOW
    start = pl.multiple_of(jnp.clip(j * LAT_QB - SWA_WINDOW, 0, DEC_SEQ - band), SWA_WINDOW)
    kb = sk4[pl.ds(start, band), :]
    vb = sv4[pl.ds(start, band), :]
    qpos = j * LAT_QB + lax.broadcasted_iota(jnp.int32, (LAT_QB, band), 0)
    kpos = start + lax.broadcasted_iota(jnp.int32, (LAT_QB, band), 1)
    near = jnp.abs(kpos - qpos) <= SWA_WINDOW
    o_d = jnp.zeros((LAT_QB, SWA_HEADS * SWA_HEAD_DIM), jnp.float32)
    for h in range(SWA_HEADS):
        mh = _lane_mask(SWA_HEADS * SWA_HEAD_DIM, SWA_HEAD_DIM, h)
        qh = (sq * mh).astype(bf16)
        s_b = jnp.where(near, lax.dot_general(qh, kb, _NT, **_F32) * SWA_SCALE, NEG_INF)
        s_c = lax.dot_general(qh, csk4[...], _NT, **_F32) * SWA_SCALE
        sink = sink_ref[h]
        m = jnp.maximum(jnp.maximum(jnp.max(s_b, axis=-1, keepdims=True), jnp.max(s_c, axis=-1, keepdims=True)), sink)
        e_b, e_c = jnp.exp(s_b - m), jnp.exp(s_c - m)
        inv = 1.0 / (jnp.sum(e_b, axis=-1, keepdims=True) + jnp.sum(e_c, axis=-1, keepdims=True) + jnp.exp(sink - m))
        o = jnp.dot((e_b * inv).astype(bf16), vb, **_F32) + jnp.dot((e_c * inv).astype(bf16), csv4[...], **_F32)
        o_d = o_d + o * mh
    od_ref[...] = o_d


def lat_branches(hs, row0, n_seq, cols, sink, qn, kvn, wuq_p, wukv_p, c_ckv, c_krope, c_swa_k, c_swa_v, branch_bufs):
    bf16 = jnp.bfloat16
    f32 = jnp.float32
    nq = DEC_SEQ // LAT_QB
    kv_w = SWA_KV_HEADS * SWA_HEAD_DIM
    krsel = selection(BRANCH_W, MLA_HEADS * MLA_ROPE,
                      [(MLA_KV_LORA + r, MLA_ROPE * h + r) for h in range(MLA_HEADS) for r in range(MLA_ROPE)])
    ksel = selection(BRANCH_W, BRANCH_W, [((h // SWA_GROUP) * SWA_HEAD_DIM + r, h * SWA_HEAD_DIM + r)
                                          for h in range(SWA_HEADS) for r in range(SWA_HEAD_DIM)])
    vsel = selection(BRANCH_W, BRANCH_W, [(kv_w + (h // SWA_GROUP) * SWA_HEAD_DIM + r, h * SWA_HEAD_DIM + r)
                                          for h in range(SWA_HEADS) for r in range(SWA_HEAD_DIM)])
    ckrsel = selection(MLA_ROPE, MLA_HEADS * MLA_ROPE, [(r, MLA_ROPE * h + r) for h in range(MLA_HEADS) for r in range(MLA_ROPE)])
    csel = np.stack([selection(SWA_HEAD_DIM, BRANCH_W, [(r, h * SWA_HEAD_DIM + r) for h in range(SWA_HEADS)
                                                        if h // SWA_GROUP == g for r in range(SWA_HEAD_DIM)])
                     for g in range(SWA_KV_HEADS)])
    c32 = rope_tables(DEC_SEQ, MLA_ROPE, MLA_HEADS)
    c64q = rope_tables(DEC_SEQ, SWA_HEAD_DIM, SWA_HEADS)
    t = jnp.arange(DEC_SEQ, dtype=jnp.int32)
    ang = ((t[:, None] * t[None, :]) % DEC_SEQ).astype(f32) * f32(2.0 * np.pi / DEC_SEQ)
    cs, ss = jnp.cos(ang), jnp.sin(ang)
    cc, sc = dft_mats(FNET_CH)
    rb = row0 // DEC_SEQ
    qb0 = row0 // LAT_QB
    qblk = lambda cb: pl.BlockSpec((LAT_QB, BRANCH_W), lambda b, j: (qb0 + b * nq + j, cb))
    sblk = lambda cb: pl.BlockSpec((DEC_SEQ, BRANCH_W), lambda b, j: (rb + b, cb), pipeline_mode=pl.Buffered(1))
    once = dict(pipeline_mode=pl.Buffered(1))
    per_b = lambda a: pl.BlockSpec((None,) + a.shape[1:], lambda b, j: (b,) + (0,) * (a.ndim - 1), **once)
    full = lambda a: pl.BlockSpec(a.shape, lambda b, j: (0,) * a.ndim, **once)
    qtab = lambda a: pl.BlockSpec((LAT_QB, a.shape[1]), lambda b, j: (j, 0))
    consts = [qn, kvn, wuq_p, wukv_p, jnp.asarray(krsel), jnp.asarray(ksel), jnp.asarray(vsel),
              jnp.asarray(ckrsel, bf16), jnp.asarray(csel, bf16)]
    sc_bf = lambda r, w: pltpu.VMEM((r, w), bf16)
    args = [sink, hs, hs, hs, hs, hs, c_ckv, c_krope, c_swa_k, c_swa_v, *consts,
            c32[0], c32[1], c32[0], c32[1], c64q[0], c64q[1], c64q[0], c64q[1], cs, ss,
            jnp.asarray(block_diag(cc, FNET_GROUPS)), jnp.asarray(block_diag(sc, FNET_GROUPS))]
    return pl.pallas_call(
        _lat_branches_kernel,
        grid=(n_seq, nq),
        in_specs=[pl.BlockSpec(memory_space=pltpu.SMEM), qblk(cols[0]), qblk(cols[1]), sblk(cols[2]), sblk(cols[3]),
                  sblk(cols[4]), per_b(c_ckv), per_b(c_krope), per_b(c_swa_k), per_b(c_swa_v)]
                 + [full(a) for a in consts]
                 + [qtab(c32[0]), qtab(c32[1]), full(c32[0]), full(c32[1]),
                    qtab(c64q[0]), qtab(c64q[1]), full(c64q[0]), full(c64q[1]),
                    pl.BlockSpec((LAT_QB, DEC_SEQ), lambda b, j: (j, 0)), pl.BlockSpec((LAT_QB, DEC_SEQ), lambda b, j: (j, 0)),
                    full(jnp.zeros((BRANCH_W, BRANCH_W))), full(jnp.zeros((BRANCH_W, BRANCH_W)))]
                 + [pl.BlockSpec(memory_space=pl.ANY)] * len(branch_bufs),
        input_output_aliases={len(args) + k: k for k in range(len(branch_bufs))},
        out_specs=[pl.BlockSpec((LAT_QB, BRANCH_W), lambda b, j: (qb0 + b * nq + j, 0))] * len(branch_bufs),
        out_shape=[jax.ShapeDtypeStruct(a.shape, a.dtype) for a in branch_bufs],
        scratch_shapes=[sc_bf(DEC_SEQ, 256), sc_bf(DEC_SEQ, 256), sc_bf(DEC_SEQ, 128), sc_bf(DEC_SEQ, 256), sc_bf(DEC_SEQ, 256),
                        sc_bf(PAST_LEN, 256), sc_bf(PAST_LEN, 256), sc_bf(PAST_LEN, 128), sc_bf(PAST_LEN, 256), sc_bf(PAST_LEN, 256),
                        pltpu.VMEM((DEC_SEQ, 256), f32), pltpu.VMEM((DEC_SEQ, 256), f32)],
        compiler_params=pltpu.CompilerParams(dimension_semantics=("arbitrary", "arbitrary"), vmem_limit_bytes=VMEM_LIMIT),
        name="lat_branches",
    )(*args, *branch_bufs)


def _oddeven_merge_sort_pairs(n):
    pairs = []
    p = 1
    while p < n:
        k = p
        while k >= 1:
            for j in range(k % p, n - k, 2 * k):
                for i in range(min(k, n - j - k)):
                    if (i + j) // (p * 2) == (i + j + k) // (p * 2):
                        pairs.append((i + j, i + j + k))
            k //= 2
        p *= 2
    return pairs


def _bitonic_merge_pairs(n):
    pairs = []
    k = n // 2
    while k >= 1:
        pairs += [(i, i + k) for i in range(n) if (i & k) == 0]
        k //= 2
    return pairs


def _compare_exchange(v, pairs):
    for i, j in pairs:
        v[i], v[j] = jnp.maximum(v[i], v[j]), jnp.minimum(v[i], v[j])


def _merge_top(v, shifts):
    nv = len(v)
    dropped = None
    for r in shifts:
        other = [pltpu.roll(v[nv - 1 - i], SUBLANES - r, 0) for i in range(nv)]
        lo = [jnp.minimum(v[i], other[i]) for i in range(nv)]
        v = [jnp.maximum(v[i], other[i]) for i in range(nv)]
        _compare_exchange(v, _bitonic_merge_pairs(nv))
        while len(lo) > 1:
            lo = [jnp.maximum(lo[2 * i], lo[2 * i + 1]) for i in range(len(lo) // 2)]
        d = lo[0]
        if dropped is not None:
            d = jnp.maximum(d, jnp.maximum(dropped, pltpu.roll(dropped, SUBLANES - r, 0)))
        dropped = d
    return v, dropped


def _sorted_top(p):
    nv = PEER_N_KEYS // SUBLANES
    v = [p[SUBLANES * i:SUBLANES * (i + 1)] for i in range(nv)]
    _compare_exchange(v, _oddeven_merge_sort_pairs(nv))
    return _merge_top(v, (4, 2, 1))


def _rank16_17(c):
    v = list(c)
    _compare_exchange(v, _oddeven_merge_sort_pairs(len(v)))
    v = v + [pltpu.roll(t, SUBLANES - 4, 0) for t in reversed(v)]
    _compare_exchange(v, _bitonic_merge_pairs(len(v)))
    v, dropped = _merge_top(v, (2, 1))
    return v[PEER_TOPK - 1][0:1], dropped[0:1]


_INNER_PAIRS = [(i, j) for i in range(1, 8) for j in range(1, 8) if (i + 1) * (j + 1) <= PEER_TOPK + 1]


def _peer_route_kernel(x_ref, sh_ref, sc_ref, wq_ref, keys_ref, ut_ref, th_ref, e1_ref, e2_ref,
                       sc_scr, a_scr, b_scr, a1_scr, ai_scr, bi_scr):
    tt = x_ref.shape[0]
    u = _ln(x_ref[...]) * (1.0 + sc_ref[...]) + sh_ref[...]
    ut = u.T.astype(jnp.bfloat16)
    ut_ref[...] = ut
    qt = jnp.dot(wq_ref[...], ut, preferred_element_type=jnp.float32)
    for hp in range(2 * PEER_HEADS):
        q_hp = qt[hp * PEER_HALF:(hp + 1) * PEER_HALF, :].astype(jnp.bfloat16)
        sc_scr[hp] = jnp.dot(keys_ref[hp], q_hp, preferred_element_type=jnp.float32)

    for scr in (a_scr, b_scr, a1_scr, ai_scr, bi_scr):
        scr[...] = jnp.full(scr.shape, NEG_INF, jnp.float32)

    def per_head(h, carry):
        for tc in range(tt // LANES):
            ls = slice(tc * LANES, (tc + 1) * LANES)
            s1 = sc_scr[2 * h, :, ls]
            s2 = sc_scr[2 * h + 1, :, ls]
            p1 = s1 - jnp.max(s1, axis=0, keepdims=True)
            p2 = s2 - jnp.max(s2, axis=0, keepdims=True)
            a, a16 = _sorted_top(p1)
            b, b16 = _sorted_top(p2)
            for i in range(PEER_TOPK + 1):
                ai = a[i][0:1] if i < PEER_TOPK else a16[0:1]
                bi = b[i][0:1] if i < PEER_TOPK else b16[0:1]
                a_scr[i:i + 1, :] = ai
                b_scr[i:i + 1, :] = bi
                if i >= 1:
                    a1_scr[i - 1:i, :] = ai
                for r, (pi, pj) in enumerate(_INNER_PAIRS):
                    if pi == i:
                        ai_scr[r:r + 1, :] = ai
                    if pj == i:
                        bi_scr[r:r + 1, :] = bi
            bb = b_scr[...]
            cand_tiles = [a_scr[0:1, :] + bb[SUBLANES * k:SUBLANES * (k + 1)] for k in range(3)]
            cand_tiles += [a1_scr[SUBLANES * k:SUBLANES * (k + 1), :] + bb[0:1] for k in range(2)]
            cand_tiles += [ai_scr[SUBLANES * k:SUBLANES * (k + 1), :] + bi_scr[SUBLANES * k:SUBLANES * (k + 1), :]
                           for k in range(3)]
            c16, c17 = _rank16_17(cand_tiles)
            tau = 0.5 * (c16 + c17)
            cand = jnp.concatenate(cand_tiles, axis=0)
            z = jnp.sum(jnp.where(cand >= tau, jnp.exp(cand), 0.0), axis=0, keepdims=True)
            th_ref[h, :, :, ls] = jnp.exp(tau - p1).reshape(PEER_N_KEYS // SUBLANES, SUBLANES, LANES)
            e1_ref[h, :, :, ls] = (0.5 * jnp.exp(p1) / z).reshape(PEER_N_KEYS // SUBLANES, SUBLANES, LANES)
            e2_ref[h, :, ls] = jnp.exp(p2)
        return carry

    lax.fori_loop(0, PEER_HEADS, per_head, 0)


def _peer_dense_kernel(ut_ref, th_ref, e1_ref, e2_ref, u_ref, vt_ref, o_ref, act_ref, gt_ref):
    j = pl.program_id(1)
    eb, tt = act_ref.shape
    n_i1 = eb // PEER_N_KEYS
    assert n_i1 == SUBLANES
    nch = tt // DENSE_CHUNK

    @pl.when(j == 0)
    def _():
        o_ref[...] = jnp.zeros_like(o_ref)

    def mm1(c):
        cs = slice(c * DENSE_CHUNK, (c + 1) * DENSE_CHUNK)
        act_ref[:, cs] = jnp.dot(u_ref[...], ut_ref[:, cs], preferred_element_type=jnp.float32)

    def mm2(c):
        cs = slice(c * DENSE_CHUNK, (c + 1) * DENSE_CHUNK)
        o_ref[:, cs] += jnp.dot(vt_ref[...], gt_ref[:, cs], preferred_element_type=jnp.float32)

    def weights(c):
        for half in range(DENSE_CHUNK // LANES):
            ls = slice(c * DENSE_CHUNK + half * LANES, c * DENSE_CHUNK + (half + 1) * LANES)
            for il in range(n_i1):
                w = None
                for h in range(PEER_HEADS):
                    th_rows = th_ref[h, j, :, ls]
                    e1_rows = e1_ref[h, j, :, ls]
                    e2 = e2_ref[h, :, ls]
                    wh = jnp.where(e2 >= th_rows[il:il + 1], e2 * e1_rows[il:il + 1], 0.0)
                    w = wh if w is None else w + wh
                rs = slice(il * PEER_N_KEYS, (il + 1) * PEER_N_KEYS)
                a = act_ref[rs, ls]
                gt_ref[rs, ls] = (w * (a * (1.0 + lax.erf(a * np.float32(np.sqrt(0.5)))))).astype(jnp.bfloat16)

    mm1(0)
    for c in range(nch):
        if c + 1 < nch:
            mm1(c + 1)
        weights(c)
        if c >= 1:
            mm2(c - 1)
    mm2(nch - 1)


def _residual_ln_kernel(x_ref, g_ref, pt_ref, lng_ref, lnb_ref, y_ref):
    r = DEEPNORM_ALPHA * x_ref[...] + g_ref[...] * pt_ref[...].T
    y_ref[...] = _ln(r) * lng_ref[...] + lnb_ref[...]


def peer_layer(x, mod, tile_cond, wq_t, keys, u_tab, vt_tab, ln_g, ln_b, *, tt_route, tt, eb):
    t, d = x.shape
    nh = PEER_HEADS
    row = lambda k, tsz: pl.BlockSpec((None, None, 1, d), lambda i, *_: (tile_cond(i, tsz), 3 + k, 0, 0))
    rows_shape = (nh, PEER_N_KEYS // SUBLANES, SUBLANES, t)
    tile_shape = (nh, PEER_N_KEYS, t)
    ut, th, e1, e2 = pl.pallas_call(
        _peer_route_kernel,
        grid=(t // tt_route,),
        in_specs=[pl.BlockSpec((tt_route, d), lambda i: (i, 0)), row(0, tt_route), row(1, tt_route),
                  pl.BlockSpec(wq_t.shape, lambda i: (0, 0)),
                  pl.BlockSpec(keys.shape, lambda i: (0, 0, 0))],
        out_specs=[pl.BlockSpec((d, tt_route), lambda i: (0, i)),
                   pl.BlockSpec(rows_shape[:3] + (tt_route,), lambda i: (0, 0, 0, i)),
                   pl.BlockSpec(rows_shape[:3] + (tt_route,), lambda i: (0, 0, 0, i)),
                   pl.BlockSpec(tile_shape[:2] + (tt_route,), lambda i: (0, 0, i))],
        out_shape=[jax.ShapeDtypeStruct((d, t), jnp.bfloat16),
                   jax.ShapeDtypeStruct(rows_shape, jnp.float32), jax.ShapeDtypeStruct(rows_shape, jnp.float32),
                   jax.ShapeDtypeStruct(tile_shape, jnp.float32)],
        scratch_shapes=[pltpu.VMEM((2 * nh, PEER_N_KEYS, tt_route), jnp.float32),
                        pltpu.VMEM((3 * SUBLANES, LANES), jnp.float32), pltpu.VMEM((3 * SUBLANES, LANES), jnp.float32),
                        pltpu.VMEM((2 * SUBLANES, LANES), jnp.float32), pltpu.VMEM((3 * SUBLANES, LANES), jnp.float32),
                        pltpu.VMEM((3 * SUBLANES, LANES), jnp.float32)],
        compiler_params=pltpu.CompilerParams(dimension_semantics=("arbitrary",), vmem_limit_bytes=VMEM_LIMIT),
        name="peer_route",
    )(x, mod, mod, wq_t, keys)

    ne = PEER_N_EXPERTS // eb
    once = dict(pipeline_mode=pl.Buffered(1))
    pt = pl.pallas_call(
        _peer_dense_kernel,
        grid=(t // tt, ne),
        in_specs=[pl.BlockSpec((d, tt), lambda i, j: (0, i)),
                  pl.BlockSpec(rows_shape[:3] + (tt,), lambda i, j: (0, 0, 0, i), **once),
                  pl.BlockSpec(rows_shape[:3] + (tt,), lambda i, j: (0, 0, 0, i), **once),
                  pl.BlockSpec(tile_shape[:2] + (tt,), lambda i, j: (0, 0, i), **once),
                  pl.BlockSpec((eb, d), lambda i, j: (j, 0)),
                  pl.BlockSpec((None, d, eb), lambda i, j: (j, 0, 0))],
        out_specs=pl.BlockSpec((d, tt), lambda i, j: (0, i)),
        out_shape=jax.ShapeDtypeStruct((d, t), jnp.float32),
        scratch_shapes=[pltpu.VMEM((eb, tt), jnp.float32), pltpu.VMEM((eb, tt), jnp.bfloat16)],
        compiler_params=pltpu.CompilerParams(dimension_semantics=("arbitrary", "arbitrary"),
                                             vmem_limit_bytes=VMEM_LIMIT),
        name="peer_dense",
    )(ut, th, e1, e2, u_tab, vt_tab)

    tl = PEER_ROUTE_TILE
    return pl.pallas_call(
        _residual_ln_kernel,
        grid=(t // tl,),
        in_specs=[pl.BlockSpec((tl, d), lambda i: (i, 0)), row(2, tl), pl.BlockSpec((d, tl), lambda i: (0, i)),
                  pl.BlockSpec((1, d), lambda i: (0, 0)), pl.BlockSpec((1, d), lambda i: (0, 0))],
        out_specs=pl.BlockSpec((tl, d), lambda i: (i, 0)),
        out_shape=jax.ShapeDtypeStruct((t, d), jnp.float32),
        compiler_params=pltpu.CompilerParams(dimension_semantics=("arbitrary",), vmem_limit_bytes=VMEM_LIMIT),
        name="peer_residual_ln",
    )(x, mod, pt, ln_g, ln_b)


def _tile_cond(i, tt):
    ctx_tiles = BATCH * SEQ // tt
    return jnp.where(i < ctx_tiles, 0, 1 + (i - ctx_tiles) // (DEC_SEQ // tt))


def kernel(x_prompt, x_sample, c, cache_mla_ckv, cache_mla_krope, cache_swa_k, cache_swa_v, state_gla,
           c_ctx, w_ada, b_ada, w_in, mla_q_norm, w_uq, mla_kv_norm, w_ukv,
           w_gla_a_fwd, b_gla_a_fwd, w_gla_a_bwd, b_gla_a_bwd, gla_norm, swa_sink,
           w_branch, w_out, ln1_g, ln1_b, ln2_g, ln2_b, w_peer_q, peer_keys, peer_u, peer_v):
    bf16 = jnp.bfloat16
    n_ctx, n_lat = BATCH * SEQ, DEC_BATCH * DEC_SEQ
    n_tok = n_ctx + n_lat
    conds = jnp.concatenate([c_ctx[None, :], c], axis=0)
    x = jnp.concatenate([x_prompt.reshape(n_ctx, D_MODEL), x_sample.reshape(n_lat, D_MODEL)], axis=0)
    src = dict(zip(IN_NAMES, (0,) + IN_OFFSETS))
    seg_args = (n_ctx // GLA_SEG, SEQ // GLA_SEG, DEC_SEQ // GLA_SEG)
    lane_block = lambda name, width: HS_OFFSET[name] // width
    gla_cols = (lane_block('gla_q', HK), lane_block('gla_k', HK), lane_block('gla_v', HV),
                HS_OFFSET['gla_af'] // LANES, lane_block('gla_g', HV))
    z_row = lambda name: HS_OFFSET[name] - gla_cols[3] * LANES
    ctx_states = []
    for l in range(DEPTH):
        mod = (jax.nn.silu(conds) @ w_ada[l] + b_ada[l]).reshape(1 + DEC_BATCH, 6, 1, D_MODEL)
        w_parts = jnp.concatenate([w_in[l][:, src[n]:src[n] + _widths[n]] for n in HS_ORDER], axis=1)
        w_parts = jnp.pad(w_parts, ((0, 0), (0, PARTS_WIDTH - GATES_OFFSET))).astype(bf16)
        hs = in_proj(x, mod, _tile_cond, w_parts, tt=512)
        blk = lambda name: HS_OFFSET[name] // BRANCH_W
        mla_w = (mla_q_norm[l][None, :], mla_kv_norm[l][None, :], regroup_uq(w_uq[l]).astype(bf16),
                 regroup_ukv(w_ukv[l]).astype(bf16))
        o_a, o_b, o_d, ckv = ctx_branches(
            hs, BATCH, (blk('mla_q'), blk('fnet'), blk('swa_q'), blk('swa_k'), blk('mla_kv')), swa_sink[l], *mla_w)
        o_a, o_b, o_d = lat_branches(
            hs, n_ctx, DEC_BATCH, (blk('mla_q'), blk('swa_q'), blk('fnet'), blk('swa_k'), blk('mla_kv')), swa_sink[l],
            *mla_w, cache_mla_ckv[:, l], cache_mla_krope[:, l], cache_swa_k[:, l], cache_swa_v[:, l], (o_a, o_b, o_d))
        ctx_part = lambda name, lo, hi: hs[:n_ctx, HS_OFFSET[name] + lo:HS_OFFSET[name] + hi]
        kv_heads = lambda name: (ctx_part(name, 0, SWA_KV_HEADS * SWA_HEAD_DIM)
                                 .reshape(BATCH, SEQ, SWA_KV_HEADS, SWA_HEAD_DIM).transpose(0, 2, 1, 3))
        st = (ckv.reshape(BATCH, SEQ, MLA_KV_LORA),
              ctx_part('mla_kv', MLA_KV_LORA, MLA_KV_LORA + MLA_ROPE).reshape(BATCH, SEQ, MLA_ROPE),
              kv_heads('swa_k'), kv_heads('swa_v'))
        w2 = lambda w, name: (jnp.zeros((LANES, HK), jnp.float32)
                              .at[z_row(name):z_row(name) + GLA_GATE_RANK].set(w).astype(bf16))
        zero = jnp.zeros((BATCH, GLA_HEADS, GLA_DK, GLA_DV), jnp.float32)
        o_c, s_f, s_b = gla_branch(
            hs.reshape(n_tok // GLA_SEG, GLA_SEG, PARTS_WIDTH), gla_cols,
            w2(w_gla_a_fwd[l], 'gla_af'), b_gla_a_fwd[l][None, :], w2(w_gla_a_bwd[l], 'gla_ab'), b_gla_a_bwd[l][None, :],
            jnp.tile(gla_norm[l], GLA_HEADS)[None, :],
            jnp.concatenate([zero, state_gla[:, l, 0]], axis=0), jnp.concatenate([zero, state_gla[:, l, 1]], axis=0),
            seg_args)
        ctx_states.append(st + (jnp.stack([s_f[:BATCH], s_b[:BATCH]], axis=1),))
        branches = (o_a, o_b, o_c.reshape(n_tok, BRANCH_W), o_d)
        x1 = merge(x, mod, _tile_cond, branches, w_in[l][:, GATES_OFFSET:].astype(bf16), w_branch[l].astype(bf16),
                   w_out[l].astype(bf16), ln1_g[l][None, :], ln1_b[l][None, :], tt=256)
        x = peer_layer(
            x1, mod, _tile_cond,
            w_peer_q[l].T.astype(bf16),
            peer_keys[l].reshape(2 * PEER_HEADS, PEER_N_KEYS, PEER_HALF).astype(bf16),
            peer_u[l].astype(bf16),
            peer_v[l].reshape(-1, PEER_EXPERT_BLOCK, D_MODEL).transpose(0, 2, 1).astype(bf16),
            ln2_g[l][None, :], ln2_b[l][None, :],
            tt_route=PEER_ROUTE_TILE, tt=PEER_TOKEN_TILE, eb=PEER_EXPERT_BLOCK)

    h = x[:n_ctx].reshape(BATCH, SEQ, D_MODEL)
    z = x[n_ctx:].reshape(DEC_BATCH, DEC_SEQ, D_MODEL)
    new_mla_ckv = jnp.stack([st[0] for st in ctx_states], axis=1)
    new_mla_krope = jnp.stack([st[1] for st in ctx_states], axis=1)
    new_swa_k = jnp.stack([st[2] for st in ctx_states], axis=1)
    new_swa_v = jnp.stack([st[3] for st in ctx_states], axis=1)
    new_gla_state = jnp.stack([st[4] for st in ctx_states], axis=1)
    return (h, z, new_mla_ckv, new_mla_krope, new_swa_k, new_swa_v, new_gla_state)
```

```python
import functools

import jax
import jax.numpy as jnp
from jax import lax
import numpy as np
from jax.experimental import pallas as pl
from jax.experimental.pallas import tpu as pltpu

D_MODEL = 1024
BATCH = 32
SEQ = 256
DEPTH = 2
DEC_BATCH = 2
DEC_SEQ = 2048
PAST_LEN = 512

GRID_W = 64
N_BRANCH = 4
BRANCH_W = 256
MLA_HEADS = 4
MLA_Q_LORA = 256
MLA_KV_LORA = 128
MLA_NOPE = 64
MLA_ROPE = 32
MLA_V = 64
MLA_SCALE = (MLA_NOPE + MLA_ROPE) ** -0.5
FNET_GROUPS = 4
FNET_CH = BRANCH_W // FNET_GROUPS
GLA_HEADS = 4
GLA_DK = 32
GLA_DV = 64
GLA_GATE_RANK = 16
GLA_TAU = 16.0
SWA_HEADS = 4
SWA_KV_HEADS = 2
SWA_GROUP = SWA_HEADS // SWA_KV_HEADS
SWA_HEAD_DIM = 64
SWA_WINDOW = 128
SWA_SCALE = SWA_HEAD_DIM ** -0.5
PEER_HEADS = 8
PEER_N_KEYS = 128
PEER_N_EXPERTS = PEER_N_KEYS * PEER_N_KEYS
PEER_KEY_DIM = 256
PEER_HALF = PEER_KEY_DIM // 2
PEER_TOPK = 16
IN_PROJ_TILE = 1024
MERGE_TILE = 256
PEER_ROUTE_TILE = 512
PEER_TOKEN_TILE = 1024
PEER_EXPERT_BLOCK = 8 * PEER_N_KEYS
DENSE_CHUNK = 256
GLA_SEG = 128
GLA_STEPS = 8
GLA_CHAIN_GROUP = 2
HK = GLA_HEADS * GLA_DK
HV = GLA_HEADS * GLA_DV
LAT_QB = 2 * SWA_WINDOW

ROPE_THETA = 10000.0
NORM_EPS = 1e-6
DEEPNORM_ALPHA = (2.0 * DEPTH) ** 0.25

IN_SPLITS = (
    ('mla_q', MLA_Q_LORA),
    ('mla_kv', MLA_KV_LORA + MLA_ROPE),
    ('fnet', BRANCH_W),
    ('gla_q', GLA_HEADS * GLA_DK),
    ('gla_k', GLA_HEADS * GLA_DK),
    ('gla_v', GLA_HEADS * GLA_DV),
    ('gla_g', BRANCH_W),
    ('gla_af', GLA_GATE_RANK),
    ('gla_ab', GLA_GATE_RANK),
    ('swa_q', SWA_HEADS * SWA_HEAD_DIM),
    ('swa_k', SWA_KV_HEADS * SWA_HEAD_DIM),
    ('swa_v', SWA_KV_HEADS * SWA_HEAD_DIM),
    ('gates', N_BRANCH * D_MODEL),
)
IN_NAMES = tuple(n for n, _ in IN_SPLITS)
IN_OFFSETS = tuple(int(o) for o in np.cumsum([w for _, w in IN_SPLITS])[:-1])
GATES_OFFSET = IN_OFFSETS[-1]
HS_ORDER = ('mla_q', 'fnet', 'gla_q', 'gla_k', 'gla_v', 'gla_g', 'swa_q', 'swa_k', 'swa_v', 'mla_kv', 'gla_af', 'gla_ab')
_widths = dict(IN_SPLITS)
HS_OFFSET = {n: int(o) for n, o in zip(HS_ORDER, np.cumsum([0] + [_widths[n] for n in HS_ORDER])[:-1])}
PARTS_WIDTH = -(-GATES_OFFSET // 128) * 128

LANES = 128
SUBLANES = 8
NEG_INF = float('-inf')
VMEM_LIMIT = 56 * 1024 * 1024


def _ln(x):
    mu = jnp.mean(x, -1, keepdims=True)
    xc = x - mu
    var = jnp.mean(xc * xc, -1, keepdims=True)
    return xc * lax.rsqrt(var + NORM_EPS)


def _in_proj_kernel(x_ref, sh_ref, sc_ref, w_ref, o_ref):
    u = _ln(x_ref[...]) * (1.0 + sc_ref[...]) + sh_ref[...]
    o_ref[...] = jnp.dot(u.astype(jnp.bfloat16), w_ref[...], preferred_element_type=jnp.float32)


def in_proj(x, mod, tile_cond, w, *, tt):
    t, d = x.shape
    n = w.shape[1]
    row = lambda k: pl.BlockSpec((None, None, 1, d), lambda i: (tile_cond(i, tt), k, 0, 0))
    return pl.pallas_call(
        _in_proj_kernel,
        grid=(t // tt,),
        in_specs=[pl.BlockSpec((tt, d), lambda i: (i, 0)), row(0), row(1), pl.BlockSpec((d, n), lambda i: (0, 0))],
        out_specs=pl.BlockSpec((tt, n), lambda i: (i, 0)),
        out_shape=jax.ShapeDtypeStruct((t, n), jnp.float32),
        compiler_params=pltpu.CompilerParams(dimension_semantics=("arbitrary",), vmem_limit_bytes=VMEM_LIMIT),
        name="in_proj",
    )(x, mod, mod, w)


def _merge_kernel(x_ref, sh_ref, sc_ref, g_ref, ba_ref, bb_ref, bc_ref, bd_ref, wg_ref, wb_ref, wo_ref,
                  lng_ref, lnb_ref, y_ref):
    x = x_ref[...]
    u = (_ln(x) * (1.0 + sc_ref[...]) + sh_ref[...]).astype(jnp.bfloat16)
    acc = jnp.zeros(x.shape, jnp.float32)
    for b, br_ref in enumerate((ba_ref, bb_ref, bc_ref, bd_ref)):
        gate = jnp.dot(u, wg_ref[:, b * D_MODEL:(b + 1) * D_MODEL], preferred_element_type=jnp.float32)
        proj = jnp.dot(br_ref[...].astype(jnp.bfloat16), wb_ref[b], preferred_element_type=jnp.float32)
        acc = acc + jax.nn.sigmoid(gate) * proj
    mix = jnp.dot(acc.astype(jnp.bfloat16), wo_ref[...], preferred_element_type=jnp.float32)
    y_ref[...] = _ln(DEEPNORM_ALPHA * x + g_ref[...] * mix) * lng_ref[...] + lnb_ref[...]


def merge(x, mod, tile_cond, branches, w_gates, w_branch, w_out, ln_g, ln_b, *, tt):
    t, d = x.shape
    row = lambda k: pl.BlockSpec((None, None, 1, d), lambda i: (tile_cond(i, tt), k, 0, 0))
    full = lambda a: pl.BlockSpec(a.shape, lambda i: (0,) * a.ndim)
    return pl.pallas_call(
        _merge_kernel,
        grid=(t // tt,),
        in_specs=[pl.BlockSpec((tt, d), lambda i: (i, 0)), row(0), row(1), row(2),
                  *[pl.BlockSpec((tt, BRANCH_W), lambda i: (i, 0)) for _ in range(N_BRANCH)],
                  full(w_gates), full(w_branch), full(w_out), full(ln_g), full(ln_b)],
        out_specs=pl.BlockSpec((tt, d), lambda i: (i, 0)),
        out_shape=jax.ShapeDtypeStruct((t, d), jnp.float32),
        compiler_params=pltpu.CompilerParams(dimension_semantics=("arbitrary",), vmem_limit_bytes=VMEM_LIMIT),
        name="merge",
    )(x, mod, mod, mod, *branches, w_gates, w_branch, w_out, ln_g, ln_b)


def _gla_local_kernel(q_ref, k_ref, v_ref, z_ref, w2_ref, b2_ref, o_ref, qd_ref, st_ref, dl_ref,
                      s_scr, d_scr, stage, ta, tk, tq, *, reverse):
    i = pl.program_id(0)
    nseg = q_ref.shape[0]

    @pl.when(i == 0)
    def _():
        s_scr[...] = jnp.zeros_like(s_scr)
        d_scr[...] = jnp.ones_like(d_scr)
        stage[...] = jnp.zeros_like(stage)

    def to_lanes(x):
        stage[0:nseg, :] = x
        return stage[...].T

    def from_lanes(xt):
        return xt.T[0:nseg, :]

    steps = range(GLA_STEPS - 1, -1, -1) if reverse else range(GLA_STEPS)
    for j in steps:
        z = jnp.dot(z_ref[:, j, :].astype(jnp.bfloat16), w2_ref[...], preferred_element_type=jnp.float32) + b2_ref[...]
        ta[...] = to_lanes(jnp.exp(jax.nn.log_sigmoid(z) / GLA_TAU))
        tq[...] = to_lanes(q_ref[:, j, :] * (GLA_DK ** -0.5))
        tk[...] = to_lanes(k_ref[:, j, :])
        d = d_scr[...] * ta[...]
        d_scr[...] = d
        qd_ref[:, j, :] = from_lanes(tq[...] * d)
        vt = [to_lanes(v_ref[:, j, LANES * c:LANES * (c + 1)]) for c in range(HV // LANES)]
        outs = []
        for h in range(GLA_HEADS):
            off = (h * GLA_DV) % LANES
            vh = vt[(h * GLA_DV) // LANES][off:off + GLA_DV]

            def rows(g, acc, h=h, vh=vh):
                r0 = pl.multiple_of(h * GLA_DK + g * SUBLANES, SUBLANES)
                a8 = ta[pl.ds(r0, SUBLANES), :]
                k8 = tk[pl.ds(r0, SUBLANES), :]
                q8 = tq[pl.ds(r0, SUBLANES), :]
                for r in range(SUBLANES):
                    s_new = a8[r:r + 1] * s_scr[r0 + r] + k8[r:r + 1] * vh
                    s_scr[r0 + r] = s_new
                    acc = acc + q8[r:r + 1] * s_new
                return acc

            outs.append(lax.fori_loop(0, GLA_DK // SUBLANES, rows, jnp.zeros((GLA_DV, LANES), jnp.float32)))
        for c in range(HV // LANES):
            per = LANES // GLA_DV
            o_ref[:, j, LANES * c:LANES * (c + 1)] = from_lanes(jnp.concatenate(outs[per * c:per * (c + 1)], axis=0))

    @pl.when(i == pl.num_programs(0) - 1)
    def _():
        st_ref[...] = s_scr[...]
        dl_ref[...] = d_scr[...].T


def gla_local(hs3, w2pad, b2, cols, *, reverse):
    nseg, seg, _ = hs3.shape
    nb = seg // GLA_STEPS
    tmap = (lambda i: nb - 1 - i) if reverse else (lambda i: i)
    blk = lambda width, cb: pl.BlockSpec((nseg, GLA_STEPS, width), lambda i: (0, tmap(i), cb))
    t3 = lambda width: jax.ShapeDtypeStruct((nseg, seg, width), jnp.float32)
    return pl.pallas_call(
        functools.partial(_gla_local_kernel, reverse=reverse),
        grid=(nb,),
        in_specs=[blk(HK, cols[0]), blk(HK, cols[1]), blk(HV, cols[2]), blk(LANES, cols[3]),
                  pl.BlockSpec((LANES, HK), lambda i: (0, 0)), pl.BlockSpec((1, HK), lambda i: (0, 0))],
        out_specs=[pl.BlockSpec((nseg, GLA_STEPS, HV), lambda i: (0, tmap(i), 0)),
                   pl.BlockSpec((nseg, GLA_STEPS, HK), lambda i: (0, tmap(i), 0)),
                   pl.BlockSpec((HK, GLA_DV, LANES), lambda i: (0, 0, 0)),
                   pl.BlockSpec((LANES, HK), lambda i: (0, 0))],
        out_shape=[t3(HV), t3(HK), jax.ShapeDtypeStruct((HK, GLA_DV, LANES), jnp.float32),
                   jax.ShapeDtypeStruct((LANES, HK), jnp.float32)],
        scratch_shapes=[pltpu.VMEM((HK, GLA_DV, LANES), jnp.float32), pltpu.VMEM((HK, LANES), jnp.float32),
                        pltpu.VMEM((LANES, LANES), jnp.float32), pltpu.VMEM((HK, LANES), jnp.float32),
                        pltpu.VMEM((HK, LANES), jnp.float32), pltpu.VMEM((HK, LANES), jnp.float32)],
        compiler_params=pltpu.CompilerParams(dimension_semantics=("arbitrary",), vmem_limit_bytes=VMEM_LIMIT),
        name="gla_local_bwd" if reverse else "gla_local_fwd",
    )(hs3, hs3, hs3, hs3, w2pad, b2)


def _seg_seq(c, n_ctx_seg, ctx_per, lat_per):
    return jnp.where(c < n_ctx_seg, c // ctx_per, n_ctx_seg // ctx_per + (c - n_ctx_seg) // lat_per)


def _seg_pos(c, n_ctx_seg, ctx_per, lat_per):
    return (jnp.where(c < n_ctx_seg, c % ctx_per, (c - n_ctx_seg) % lat_per),
            jnp.where(c < n_ctx_seg, ctx_per, lat_per))


def _gla_chain_kernel(*refs, reverse, combine, seg_args):
    if combine:
        ol_ref, qd_ref, sl_ref, dl_ref, s0_ref, of_ref, gate_ref, gn_ref, o_ref, fin_ref, st = refs
    else:
        ol_ref, qd_ref, sl_ref, dl_ref, s0_ref, o_ref, fin_ref, st = refs
    i = pl.program_id(0)
    grp = pl.num_programs(0) - 1 - i if reverse else i
    row_head = lax.broadcasted_iota(jnp.int32, (HV, HK), 0) // GLA_DV
    col_head = lax.broadcasted_iota(jnp.int32, (HV, HK), 1) // GLA_DK
    for g in (range(GLA_CHAIN_GROUP - 1, -1, -1) if reverse else range(GLA_CHAIN_GROUP)):
        pos, per = _seg_pos(grp * GLA_CHAIN_GROUP + g, *seg_args)
        first = (pos == per - 1) if reverse else (pos == 0)

        @pl.when(first)
        def _():
            st[...] = s0_ref[...]

        s_in = st[...]
        o = ol_ref[g] + lax.dot_general(qd_ref[g].astype(jnp.bfloat16), s_in.astype(jnp.bfloat16),
                                        (((1,), (1,)), ((), ())), preferred_element_type=jnp.float32)
        s_loc = jnp.where(row_head == col_head, jnp.concatenate([sl_ref[g]] * GLA_HEADS, axis=0), 0.0)
        s_new = dl_ref[g] * s_in + s_loc
        st[...] = s_new
        fin_ref[...] = s_new
        if combine:
            x = o + of_ref[g]
            lane_head = lax.broadcasted_iota(jnp.int32, x.shape, 1) // GLA_DV
            x2 = x * x
            scale = jnp.zeros_like(x)
            for h in range(GLA_HEADS):
                ms = jnp.sum(jnp.where(lane_head == h, x2, 0.0), axis=-1, keepdims=True) * (1.0 / GLA_DV)
                scale = jnp.where(lane_head == h, lax.rsqrt(ms + NORM_EPS), scale)
            o = x * scale * gn_ref[...] * jax.nn.silu(gate_ref[g])
        o_ref[g] = o


def gla_chain(o_loc, qd, sl_t, dl, s0_t, seg_args, *, reverse, o_fwd=None, hs3=None, gate_col=None, gnorm=None):
    nseg, seg, _ = o_loc.shape
    n_seq = s0_t.shape[0]
    gg = GLA_CHAIN_GROUP
    assert all(n % gg == 0 for n in seg_args)
    ngrp = nseg // gg
    cmap = (lambda i: ngrp - 1 - i) if reverse else (lambda i: i)
    seq = lambda i: _seg_seq(cmap(i) * gg, *seg_args)
    combine = o_fwd is not None
    in_specs = [pl.BlockSpec((gg, seg, HV), lambda i: (cmap(i), 0, 0)),
                pl.BlockSpec((gg, seg, HK), lambda i: (cmap(i), 0, 0)),
                pl.BlockSpec((gg, GLA_DV, HK), lambda i: (cmap(i), 0, 0)),
                pl.BlockSpec((gg, 1, HK), lambda i: (cmap(i), 0, 0)),
                pl.BlockSpec((None, HV, HK), lambda i: (seq(i), 0, 0))]
    args = [o_loc, qd, sl_t, dl.reshape(dl.shape[0], 1, HK), s0_t]
    if combine:
        in_specs += [pl.BlockSpec((gg, seg, HV), lambda i: (cmap(i), 0, 0)),
                     pl.BlockSpec((gg, seg, HV), lambda i: (cmap(i), 0, gate_col)),
                     pl.BlockSpec((1, HV), lambda i: (0, 0))]
        args += [o_fwd, hs3, gnorm]
    return pl.pallas_call(
        functools.partial(_gla_chain_kernel, reverse=reverse, combine=combine, seg_args=seg_args),
        grid=(ngrp,),
        in_specs=in_specs,
        out_specs=[pl.BlockSpec((gg, seg, HV), lambda i: (cmap(i), 0, 0)),
                   pl.BlockSpec((None, HV, HK), lambda i: (seq(i), 0, 0))],
        out_shape=[jax.ShapeDtypeStruct((nseg, seg, HV), jnp.float32),
                   jax.ShapeDtypeStruct((n_seq, HV, HK), jnp.float32)],
        scratch_shapes=[pltpu.VMEM((HV, HK), jnp.float32)],
        compiler_params=pltpu.CompilerParams(dimension_semantics=("arbitrary",), vmem_limit_bytes=VMEM_LIMIT),
        name="gla_chain_bwd" if reverse else "gla_chain_fwd",
    )(*args)


def _state_to_chain(s):
    n = s.shape[0]
    eye = jnp.eye(GLA_HEADS, dtype=s.dtype)
    return jnp.einsum('nhkv,hg->nhvgk', s, eye).reshape(n, HV, HK)


def _state_from_chain(f):
    n = f.shape[0]
    f5 = f.reshape(n, GLA_HEADS, GLA_DV, GLA_HEADS, GLA_DK)
    return jnp.stack([f5[:, h, :, h, :] for h in range(GLA_HEADS)], axis=1).transpose(0, 1, 3, 2)


def gla_branch(hs3, cols, w2f, b2f, w2b, b2b, gnorm, s0_fwd, s0_bwd, seg_args):
    nseg = hs3.shape[0]
    o_f = None
    for reverse, w2, b2, s0 in ((False, w2f, b2f, s0_fwd), (True, w2b, b2b, s0_bwd)):
        o_loc, qd, st, dl = gla_local(hs3, w2, b2, cols[:4], reverse=reverse)
        sl_t = jnp.transpose(st, (2, 1, 0))[:nseg]
        if not reverse:
            o_f, fin_f = gla_chain(o_loc, qd, sl_t, dl[:nseg], _state_to_chain(s0), seg_args, reverse=False)
        else:
            o, fin_b = gla_chain(o_loc, qd, sl_t, dl[:nseg], _state_to_chain(s0), seg_args, reverse=True,
                                 o_fwd=o_f, hs3=hs3, gate_col=cols[4], gnorm=gnorm)
    return o, _state_from_chain(fin_f), _state_from_chain(fin_b)


_NT = (((1,), (1,)), ((), ()))
_F32 = dict(preferred_element_type=jnp.float32)


def _rms(x, g):
    return x * lax.rsqrt(jnp.mean(x * x, -1, keepdims=True) + NORM_EPS) * g


def _lane_mask(width, seg, h):
    lane = lax.broadcasted_iota(jnp.int32, (1, width), 1)
    return (lane // seg == h).astype(jnp.float32)


def _softmax_rows(s, sink=None):
    m = jnp.max(s, axis=-1, keepdims=True)
    if sink is not None:
        m = jnp.maximum(m, sink)
    e = jnp.exp(s - m)
    den = jnp.sum(e, axis=-1, keepdims=True)
    if sink is not None:
        den = den + jnp.exp(sink - m)
    return e / den


def _dft(x, cs, ss, cc_bd, sc_bd, scale):
    hi = dict(preferred_element_type=jnp.float32, precision=lax.Precision.HIGHEST)
    xc = jnp.dot(x, cc_bd, **hi)
    xs = jnp.dot(x, sc_bd, **hi)
    return (jnp.dot(cs, xc, **hi) - jnp.dot(ss, xs, **hi)) * scale


def _ctx_branches_kernel(sink_ref, mq_ref, fn_ref, sq_ref, skv_ref, mkv_ref, qn_ref, kvn_ref, wuq_ref, wukv_ref,
                         krsel_ref, ksel_ref, vsel_ref, cs_ref, ss_ref, ccbd_ref, scbd_ref,
                         oa_ref, ob_ref, od_ref, ckv_ref):
    bf16 = jnp.bfloat16
    qn = _rms(mq_ref[...], qn_ref[...]).astype(bf16)
    q = jnp.dot(qn, wuq_ref[...], **_F32)
    q_nope = q[:, :MLA_HEADS * MLA_NOPE]
    q_rope = q[:, MLA_HEADS * MLA_NOPE:]
    mkv = mkv_ref[...]
    ckv = _rms(mkv[:, :MLA_KV_LORA], kvn_ref[...])
    ckv_ref[...] = ckv
    kv = jnp.dot(ckv.astype(bf16), wukv_ref[...], **_F32)
    k_nope = kv[:, :MLA_HEADS * MLA_NOPE].astype(bf16)
    v = kv[:, MLA_HEADS * MLA_NOPE:].astype(bf16)
    k_rope4 = jnp.dot(mkv.astype(bf16), krsel_ref[...], **_F32).astype(bf16)
    stack = lambda x, seg: jnp.concatenate(
        [x * _lane_mask(x.shape[1], seg, h) for h in range(x.shape[1] // seg)], axis=0).astype(bf16)
    s = (lax.dot_general(stack(q_nope, MLA_NOPE), k_nope, _NT, **_F32)
         + lax.dot_general(stack(q_rope, MLA_ROPE), k_rope4, _NT, **_F32)) * MLA_SCALE
    pv = jnp.dot(_softmax_rows(s).astype(bf16), v, **_F32)
    oa_ref[...] = sum(pv[h * SEQ:(h + 1) * SEQ] * _lane_mask(MLA_HEADS * MLA_V, MLA_V, h) for h in range(MLA_HEADS))
    ob_ref[...] = _dft(fn_ref[...], cs_ref[...], ss_ref[...], ccbd_ref[...], scbd_ref[...],
                       float((SEQ * FNET_CH) ** -0.5))
    sq = sq_ref[...]
    skv = skv_ref[...].astype(bf16)
    k4 = jnp.dot(skv, ksel_ref[...], **_F32).astype(bf16)
    v4 = jnp.dot(skv, vsel_ref[...], **_F32).astype(bf16)
    s = lax.dot_general(stack(sq, SWA_HEAD_DIM), k4, _NT, **_F32) * SWA_SCALE
    row_head = lax.broadcasted_iota(jnp.int32, (SWA_HEADS * SEQ, 1), 0) // SEQ
    sink = sum(jnp.where(row_head == h, sink_ref[h], 0.0) for h in range(SWA_HEADS))
    pv = jnp.dot(_softmax_rows(s, sink).astype(bf16), v4, **_F32)
    od_ref[...] = sum(pv[h * SEQ:(h + 1) * SEQ] * _lane_mask(SWA_HEADS * SWA_HEAD_DIM, SWA_HEAD_DIM, h)
                      for h in range(SWA_HEADS))


def dft_mats(n):
    k = np.arange(n)
    ang = 2.0 * np.pi * np.outer(k, k) / n
    return np.cos(ang).astype(np.float32), np.sin(ang).astype(np.float32)


def block_diag(m, reps):
    n = m.shape[0]
    out = np.zeros((n * reps, n * reps), m.dtype)
    for r in range(reps):
        out[r * n:(r + 1) * n, r * n:(r + 1) * n] = m
    return out


def selection(rows, cols, pairs):
    m = np.zeros((rows, cols), np.float32)
    for r, c in pairs:
        m[r, c] = 1.0
    return m


def ctx_branches(hs, n_seq, cols, sink, qn, kvn, wuq_p, wukv_p):
    bf16 = jnp.bfloat16
    cs, ss = dft_mats(SEQ)
    cc, sc = dft_mats(FNET_CH)
    krsel = selection(BRANCH_W, MLA_HEADS * MLA_ROPE,
                      [(MLA_KV_LORA + r, MLA_ROPE * h + r) for h in range(MLA_HEADS) for r in range(MLA_ROPE)])
    kv_w = SWA_KV_HEADS * SWA_HEAD_DIM
    ksel = selection(BRANCH_W, BRANCH_W, [((h // SWA_GROUP) * SWA_HEAD_DIM + r, h * SWA_HEAD_DIM + r)
                                          for h in range(SWA_HEADS) for r in range(SWA_HEAD_DIM)])
    vsel = selection(BRANCH_W, BRANCH_W, [(kv_w + (h // SWA_GROUP) * SWA_HEAD_DIM + r, h * SWA_HEAD_DIM + r)
                                          for h in range(SWA_HEADS) for r in range(SWA_HEAD_DIM)])
    consts = [jnp.asarray(krsel, bf16), jnp.asarray(ksel, bf16), jnp.asarray(vsel, bf16),
              jnp.asarray(cs), jnp.asarray(ss), jnp.asarray(block_diag(cc, FNET_GROUPS)),
              jnp.asarray(block_diag(sc, FNET_GROUPS))]
    col = lambda cb: pl.BlockSpec((SEQ, BRANCH_W), lambda b: (b, cb))
    full = lambda a: pl.BlockSpec(a.shape, lambda b: (0,) * a.ndim)
    weights = [qn, kvn, wuq_p, wukv_p] + consts
    out = lambda rows, w: jax.ShapeDtypeStruct((rows, w), jnp.float32)
    return pl.pallas_call(
        _ctx_branches_kernel,
        grid=(n_seq,),
        in_specs=[pl.BlockSpec(memory_space=pltpu.SMEM)] + [col(cb) for cb in cols] + [full(a) for a in weights],
        out_specs=[pl.BlockSpec((SEQ, BRANCH_W), lambda b: (b, 0))] * 3 + [pl.BlockSpec((SEQ, MLA_KV_LORA), lambda b: (b, 0))],
        out_shape=[out(hs.shape[0], BRANCH_W)] * 3 + [out(n_seq * SEQ, MLA_KV_LORA)],
        compiler_params=pltpu.CompilerParams(dimension_semantics=("arbitrary",), vmem_limit_bytes=VMEM_LIMIT),
        name="ctx_branches",
    )(sink, *[hs] * len(cols), *weights)


def regroup_uq(w_uq):
    w = w_uq.reshape(MLA_Q_LORA, MLA_HEADS, MLA_NOPE + MLA_ROPE)
    return jnp.concatenate([w[:, :, :MLA_NOPE].reshape(MLA_Q_LORA, -1), w[:, :, MLA_NOPE:].reshape(MLA_Q_LORA, -1)], axis=1)


def regroup_ukv(w_ukv):
    w = w_ukv.reshape(MLA_KV_LORA, MLA_HEADS, MLA_NOPE + MLA_V)
    return jnp.concatenate([w[:, :, :MLA_NOPE].reshape(MLA_KV_LORA, -1), w[:, :, MLA_NOPE:].reshape(MLA_KV_LORA, -1)], axis=1)


def rope_tables(n, dim, reps):
    half = dim // 2
    t = np.arange(n)
    freqs = (ROPE_THETA ** (-np.arange(0, half, 2, dtype=np.float32) / half)).astype(np.float32)
    ang = [(t // GRID_W).astype(np.float32)[:, None] * freqs[None, :], (t % GRID_W).astype(np.float32)[:, None] * freqs[None, :]]
    cos = np.concatenate([np.cos(a) for a in ang for _ in range(2)], axis=1)
    sin = np.concatenate([s * np.sin(a) for a in ang for s in (-1.0, 1.0)], axis=1)
    return (jnp.asarray(np.tile(cos, (1, reps)), jnp.float32), jnp.asarray(np.tile(sin, (1, reps)), jnp.float32))


def _rope(x, cos, sin_signed, quarter):
    w = x.shape[-1]
    lane = lax.broadcasted_iota(jnp.int32, x.shape, 1)
    swapped = jnp.where(lane % (2 * quarter) < quarter, pltpu.roll(x, w - quarter, 1), pltpu.roll(x, quarter, 1))
    return x * cos + swapped * sin_signed


def _lat_branches_kernel(sink_ref, mq_ref, sq_ref, fn_ref, skv_ref, mkv_ref, cckv_ref, ckr_ref, csk_ref, csv_ref,
                         qn_ref, kvn_ref, wuq_ref, wukv_ref, krsel_ref, ksel_ref, vsel_ref, ckrsel_ref, csel_ref,
                         cq32_ref, sq32_ref, ck32_ref, sk32_ref, cq64_ref, sq64_ref, ck64_ref, sk64_ref,
                         cs_ref, ss_ref, ccbd_ref, scbd_ref, oa_in, ob_in, od_in,
                         oa_ref, ob_ref, od_ref,
                         kn, vv, kr4, sk4, sv4, ckn, cvv, ckr4, csk4, csv4, xc, xs):
    bf16 = jnp.bfloat16
    j = pl.program_id(1)
    hi = dict(preferred_element_type=jnp.float32, precision=lax.Precision.HIGHEST)

    @pl.when(j == 0)
    def _():
        mkv = mkv_ref[...]
        ckv = _rms(mkv[:, :MLA_KV_LORA], kvn_ref[...])
        kv = jnp.dot(ckv.astype(bf16), wukv_ref[...], **_F32)
        kn[...] = kv[:, :MLA_HEADS * MLA_NOPE].astype(bf16)
        vv[...] = kv[:, MLA_HEADS * MLA_NOPE:].astype(bf16)
        kr = jnp.dot(mkv, krsel_ref[...], **hi)
        kr4[...] = _rope(kr, ck32_ref[...], sk32_ref[...], MLA_ROPE // 4).astype(bf16)
        skv = skv_ref[...]
        k4 = jnp.dot(skv, ksel_ref[...], **hi)
        sk4[...] = _rope(k4, ck64_ref[...], sk64_ref[...], SWA_HEAD_DIM // 4).astype(bf16)
        sv4[...] = jnp.dot(skv.astype(bf16), vsel_ref[...].astype(bf16), **_F32).astype(bf16)
        ckv_c = jnp.dot(cckv_ref[...].astype(bf16), wukv_ref[...], **_F32)
        ckn[...] = ckv_c[:, :MLA_HEADS * MLA_NOPE].astype(bf16)
        cvv[...] = ckv_c[:, MLA_HEADS * MLA_NOPE:].astype(bf16)
        ckr4[...] = jnp.dot(ckr_ref[...].astype(bf16), ckrsel_ref[...], **_F32).astype(bf16)
        for src, dst in ((csk_ref, csk4), (csv_ref, csv4)):
            acc = jnp.zeros(dst.shape, jnp.float32)
            for g in range(SWA_KV_HEADS):
                acc = acc + jnp.dot(src[g].astype(bf16), csel_ref[g], **_F32)
            dst[...] = acc.astype(bf16)
        x = fn_ref[...]
        xc[...] = jnp.dot(x, ccbd_ref[...], **hi)
        xs[...] = jnp.dot(x, scbd_ref[...], **hi)

    qn = _rms(mq_ref[...], qn_ref[...]).astype(bf16)
    q = jnp.dot(qn, wuq_ref[...], **_F32)
    q_nope = q[:, :MLA_HEADS * MLA_NOPE]
    q_rope = _rope(q[:, MLA_HEADS * MLA_NOPE:], cq32_ref[...], sq32_ref[...], MLA_ROPE // 4)
    o_a = jnp.zeros((LAT_QB, MLA_HEADS * MLA_V), jnp.float32)
    for h in range(MLA_HEADS):
        qh_n = (q_nope * _lane_mask(MLA_HEADS * MLA_NOPE, MLA_NOPE, h)).astype(bf16)
        qh_r = (q_rope * _lane_mask(MLA_HEADS * MLA_ROPE, MLA_ROPE, h)).astype(bf16)
        s_c = (lax.dot_general(qh_n, ckn[...], _NT, **_F32) + lax.dot_general(qh_r, ckr4[...], _NT, **_F32)) * MLA_SCALE
        s_l = (lax.dot_general(qh_n, kn[...], _NT, **_F32) + lax.dot_general(qh_r, kr4[...], _NT, **_F32)) * MLA_SCALE
        m = jnp.maximum(jnp.max(s_c, axis=-1, keepdims=True), jnp.max(s_l, axis=-1, keepdims=True))
        e_c, e_l = jnp.exp(s_c - m), jnp.exp(s_l - m)
        inv = 1.0 / (jnp.sum(e_c, axis=-1, keepdims=True) + jnp.sum(e_l, axis=-1, keepdims=True))
        o = jnp.dot((e_c * inv).astype(bf16), cvv[...], **_F32) + jnp.dot((e_l * inv).astype(bf16), vv[...], **_F32)
        o_a = o_a + o * _lane_mask(MLA_HEADS * MLA_V, MLA_V, h)
    oa_ref[...] = o_a
    ob_ref[...] = (jnp.dot(cs_ref[...], xc[...], **hi) - jnp.dot(ss_ref[...], xs[...], **hi)) * float((DEC_SEQ * FNET_CH) ** -0.5)
    sq = _rope(sq_ref[...], cq64_ref[...], sq64_ref[...], SWA_HEAD_DIM // 4)
    band = LAT_QB + 2 * SWA_WINDOW
    start = pl.multiple_of(jnp.clip(j * LAT_QB - SWA_WINDOW, 0, DEC_SEQ - band), SWA_WINDOW)
    kb = sk4[pl.ds(start, band), :]
    vb = sv4[pl.ds(start, band), :]
    qpos = j * LAT_QB + lax.broadcasted_iota(jnp.int32, (LAT_QB, band), 0)
    kpos = start + lax.broadcasted_iota(jnp.int32, (LAT_QB, band), 1)
    near = jnp.abs(kpos - qpos) <= SWA_WINDOW
    o_d = jnp.zeros((LAT_QB, SWA_HEADS * SWA_HEAD_DIM), jnp.float32)
    for h in range(SWA_HEADS):
        mh = _lane_mask(SWA_HEADS * SWA_HEAD_DIM, SWA_HEAD_DIM, h)
        qh = (sq * mh).astype(bf16)
        s_b = jnp.where(near, lax.dot_general(qh, kb, _NT, **_F32) * SWA_SCALE, NEG_INF)
        s_c = lax.dot_general(qh, csk4[...], _NT, **_F32) * SWA_SCALE
        sink = sink_ref[h]
        m = jnp.maximum(jnp.maximum(jnp.max(s_b, axis=-1, keepdims=True), jnp.max(s_c, axis=-1, keepdims=True)), sink)
        e_b, e_c = jnp.exp(s_b - m), jnp.exp(s_c - m)
        inv = 1.0 / (jnp.sum(e_b, axis=-1, keepdims=True) + jnp.sum(e_c, axis=-1, keepdims=True) + jnp.exp(sink - m))
        o = jnp.dot((e_b * inv).astype(bf16), vb, **_F32) + jnp.dot((e_c * inv).astype(bf16), csv4[...], **_F32)
        o_d = o_d + o * mh
    od_ref[...] = o_d


def lat_branches(hs, row0, n_seq, cols, sink, qn, kvn, wuq_p, wukv_p, c_ckv, c_krope, c_swa_k, c_swa_v, branch_bufs):
    bf16 = jnp.bfloat16
    f32 = jnp.float32
    nq = DEC_SEQ // LAT_QB
    kv_w = SWA_KV_HEADS * SWA_HEAD_DIM
    krsel = selection(BRANCH_W, MLA_HEADS * MLA_ROPE,
                      [(MLA_KV_LORA + r, MLA_ROPE * h + r) for h in range(MLA_HEADS) for r in range(MLA_ROPE)])
    ksel = selection(BRANCH_W, BRANCH_W, [((h // SWA_GROUP) * SWA_HEAD_DIM + r, h * SWA_HEAD_DIM + r)
                                          for h in range(SWA_HEADS) for r in range(SWA_HEAD_DIM)])
    vsel = selection(BRANCH_W, BRANCH_W, [(kv_w + (h // SWA_GROUP) * SWA_HEAD_DIM + r, h * SWA_HEAD_DIM + r)
                                          for h in range(SWA_HEADS) for r in range(SWA_HEAD_DIM)])
    ckrsel = selection(MLA_ROPE, MLA_HEADS * MLA_ROPE, [(r, MLA_ROPE * h + r) for h in range(MLA_HEADS) for r in range(MLA_ROPE)])
    csel = np.stack([selection(SWA_HEAD_DIM, BRANCH_W, [(r, h * SWA_HEAD_DIM + r) for h in range(SWA_HEADS)
                                                        if h // SWA_GROUP == g for r in range(SWA_HEAD_DIM)])
                     for g in range(SWA_KV_HEADS)])
    c32 = rope_tables(DEC_SEQ, MLA_ROPE, MLA_HEADS)
    c64q = rope_tables(DEC_SEQ, SWA_HEAD_DIM, SWA_HEADS)
    t = jnp.arange(DEC_SEQ, dtype=jnp.int32)
    ang = ((t[:, None] * t[None, :]) % DEC_SEQ).astype(f32) * f32(2.0 * np.pi / DEC_SEQ)
    cs, ss = jnp.cos(ang), jnp.sin(ang)
    cc, sc = dft_mats(FNET_CH)
    rb = row0 // DEC_SEQ
    qb0 = row0 // LAT_QB
    qblk = lambda cb: pl.BlockSpec((LAT_QB, BRANCH_W), lambda b, j: (qb0 + b * nq + j, cb))
    sblk = lambda cb: pl.BlockSpec((DEC_SEQ, BRANCH_W), lambda b, j: (rb + b, cb), pipeline_mode=pl.Buffered(1))
    once = dict(pipeline_mode=pl.Buffered(1))
    per_b = lambda a: pl.BlockSpec((None,) + a.shape[1:], lambda b, j: (b,) + (0,) * (a.ndim - 1), **once)
    full = lambda a: pl.BlockSpec(a.shape, lambda b, j: (0,) * a.ndim, **once)
    qtab = lambda a: pl.BlockSpec((LAT_QB, a.shape[1]), lambda b, j: (j, 0))
    consts = [qn, kvn, wuq_p, wukv_p, jnp.asarray(krsel), jnp.asarray(ksel), jnp.asarray(vsel),
              jnp.asarray(ckrsel, bf16), jnp.asarray(csel, bf16)]
    sc_bf = lambda r, w: pltpu.VMEM((r, w), bf16)
    args = [sink, hs, hs, hs, hs, hs, c_ckv, c_krope, c_swa_k, c_swa_v, *consts,
            c32[0], c32[1], c32[0], c32[1], c64q[0], c64q[1], c64q[0], c64q[1], cs, ss,
            jnp.asarray(block_diag(cc, FNET_GROUPS)), jnp.asarray(block_diag(sc, FNET_GROUPS))]
    return pl.pallas_call(
        _lat_branches_kernel,
        grid=(n_seq, nq),
        in_specs=[pl.BlockSpec(memory_space=pltpu.SMEM), qblk(cols[0]), qblk(cols[1]), sblk(cols[2]), sblk(cols[3]),
                  sblk(cols[4]), per_b(c_ckv), per_b(c_krope), per_b(c_swa_k), per_b(c_swa_v)]
                 + [full(a) for a in consts]
                 + [qtab(c32[0]), qtab(c32[1]), full(c32[0]), full(c32[1]),
                    qtab(c64q[0]), qtab(c64q[1]), full(c64q[0]), full(c64q[1]),
                    pl.BlockSpec((LAT_QB, DEC_SEQ), lambda b, j: (j, 0)), pl.BlockSpec((LAT_QB, DEC_SEQ), lambda b, j: (j, 0)),
                    full(jnp.zeros((BRANCH_W, BRANCH_W))), full(jnp.zeros((BRANCH_W, BRANCH_W)))]
                 + [pl.BlockSpec(memory_space=pl.ANY)] * len(branch_bufs),
        input_output_aliases={len(args) + k: k for k in range(len(branch_bufs))},
        out_specs=[pl.BlockSpec((LAT_QB, BRANCH_W), lambda b, j: (qb0 + b * nq + j, 0))] * len(branch_bufs),
        out_shape=[jax.ShapeDtypeStruct(a.shape, a.dtype) for a in branch_bufs],
        scratch_shapes=[sc_bf(DEC_SEQ, 256), sc_bf(DEC_SEQ, 256), sc_bf(DEC_SEQ, 128), sc_bf(DEC_SEQ, 256), sc_bf(DEC_SEQ, 256),
                        sc_bf(PAST_LEN, 256), sc_bf(PAST_LEN, 256), sc_bf(PAST_LEN, 128), sc_bf(PAST_LEN, 256), sc_bf(PAST_LEN, 256),
                        pltpu.VMEM((DEC_SEQ, 256), f32), pltpu.VMEM((DEC_SEQ, 256), f32)],
        compiler_params=pltpu.CompilerParams(dimension_semantics=("arbitrary", "arbitrary"), vmem_limit_bytes=VMEM_LIMIT),
        name="lat_branches",
    )(*args, *branch_bufs)


def _oddeven_merge_sort_pairs(n):
    pairs = []
    p = 1
    while p < n:
        k = p
        while k >= 1:
            for j in range(k % p, n - k, 2 * k):
                for i in range(min(k, n - j - k)):
                    if (i + j) // (p * 2) == (i + j + k) // (p * 2):
                        pairs.append((i + j, i + j + k))
            k //= 2
        p *= 2
    return pairs


def _bitonic_merge_pairs(n):
    pairs = []
    k = n // 2
    while k >= 1:
        pairs += [(i, i + k) for i in range(n) if (i & k) == 0]
        k //= 2
    return pairs


def _compare_exchange(v, pairs):
    for i, j in pairs:
        v[i], v[j] = jnp.maximum(v[i], v[j]), jnp.minimum(v[i], v[j])


def _merge_top(v, shifts):
    nv = len(v)
    dropped = None
    for r in shifts:
        other = [pltpu.roll(v[nv - 1 - i], SUBLANES - r, 0) for i in range(nv)]
        lo = [jnp.minimum(v[i], other[i]) for i in range(nv)]
        v = [jnp.maximum(v[i], other[i]) for i in range(nv)]
        _compare_exchange(v, _bitonic_merge_pairs(nv))
        while len(lo) > 1:
            lo = [jnp.maximum(lo[2 * i], lo[2 * i + 1]) for i in range(len(lo) // 2)]
        d = lo[0]
        if dropped is not None:
            d = jnp.maximum(d, jnp.maximum(dropped, pltpu.roll(dropped, SUBLANES - r, 0)))
        dropped = d
    return v, dropped


def _sorted_top(p):
    nv = PEER_N_KEYS // SUBLANES
    v = [p[SUBLANES * i:SUBLANES * (i + 1)] for i in range(nv)]
    _compare_exchange(v, _oddeven_merge_sort_pairs(nv))
    return _merge_top(v, (4, 2, 1))


def _rank16_17(c):
    v = list(c)
    _compare_exchange(v, _oddeven_merge_sort_pairs(len(v)))
    v = v + [pltpu.roll(t, SUBLANES - 4, 0) for t in reversed(v)]
    _compare_exchange(v, _bitonic_merge_pairs(len(v)))
    v, dropped = _merge_top(v, (2, 1))
    return v[PEER_TOPK - 1][0:1], dropped[0:1]


_INNER_PAIRS = [(i, j) for i in range(1, 8) for j in range(1, 8) if (i + 1) * (j + 1) <= PEER_TOPK + 1]


def _peer_route_kernel(x_ref, sh_ref, sc_ref, wq_ref, keys_ref, ut_ref, th_ref, e1_ref, e2_ref,
                       sc_scr, a_scr, b_scr, a1_scr, ai_scr, bi_scr):
    tt = x_ref.shape[0]
    u = _ln(x_ref[...]) * (1.0 + sc_ref[...]) + sh_ref[...]
    ut = u.T.astype(jnp.bfloat16)
    ut_ref[...] = ut
    qt = jnp.dot(wq_ref[...], ut, preferred_element_type=jnp.float32)
    for hp in range(2 * PEER_HEADS):
        q_hp = qt[hp * PEER_HALF:(hp + 1) * PEER_HALF, :].astype(jnp.bfloat16)
        sc_scr[hp] = jnp.dot(keys_ref[hp], q_hp, preferred_element_type=jnp.float32)

    for scr in (a_scr, b_scr, a1_scr, ai_scr, bi_scr):
        scr[...] = jnp.full(scr.shape, NEG_INF, jnp.float32)

    def per_head(h, carry):
        for tc in range(tt // LANES):
            ls = slice(tc * LANES, (tc + 1) * LANES)
            s1 = sc_scr[2 * h, :, ls]
            s2 = sc_scr[2 * h + 1, :, ls]
            p1 = s1 - jnp.max(s1, axis=0, keepdims=True)
            p2 = s2 - jnp.max(s2, axis=0, keepdims=True)
            a, a16 = _sorted_top(p1)
            b, b16 = _sorted_top(p2)
            for i in range(PEER_TOPK + 1):
                ai = a[i][0:1] if i < PEER_TOPK else a16[0:1]
                bi = b[i][0:1] if i < PEER_TOPK else b16[0:1]
                a_scr[i:i + 1, :] = ai
                b_scr[i:i + 1, :] = bi
                if i >= 1:
                    a1_scr[i - 1:i, :] = ai
                for r, (pi, pj) in enumerate(_INNER_PAIRS):
                    if pi == i:
                        ai_scr[r:r + 1, :] = ai
                    if pj == i:
                        bi_scr[r:r + 1, :] = bi
            bb = b_scr[...]
            cand_tiles = [a_scr[0:1, :] + bb[SUBLANES * k:SUBLANES * (k + 1)] for k in range(3)]
            cand_tiles += [a1_scr[SUBLANES * k:SUBLANES * (k + 1), :] + bb[0:1] for k in range(2)]
            cand_tiles += [ai_scr[SUBLANES * k:SUBLANES * (k + 1), :] + bi_scr[SUBLANES * k:SUBLANES * (k + 1), :]
                           for k in range(3)]
            c16, c17 = _rank16_17(cand_tiles)
            tau = 0.5 * (c16 + c17)
            cand = jnp.concatenate(cand_tiles, axis=0)
            z = jnp.sum(jnp.where(cand >= tau, jnp.exp(cand), 0.0), axis=0, keepdims=True)
            th_ref[h, :, :, ls] = jnp.exp(tau - p1).reshape(PEER_N_KEYS // SUBLANES, SUBLANES, LANES)
            e1_ref[h, :, :, ls] = (0.5 * jnp.exp(p1) / z).reshape(PEER_N_KEYS // SUBLANES, SUBLANES, LANES)
            e2_ref[h, :, ls] = jnp.exp(p2)
        return carry

    lax.fori_loop(0, PEER_HEADS, per_head, 0)


def _peer_dense_kernel(ut_ref, th_ref, e1_ref, e2_ref, u_ref, vt_ref, o_ref, act_ref, gt_ref):
    j = pl.program_id(1)
    eb, tt = act_ref.shape
    n_i1 = eb // PEER_N_KEYS
    assert n_i1 == SUBLANES
    nch = tt // DENSE_CHUNK

    @pl.when(j == 0)
    def _():
        o_ref[...] = jnp.zeros_like(o_ref)

    def mm1(c):
        cs = slice(c * DENSE_CHUNK, (c + 1) * DENSE_CHUNK)
        act_ref[:, cs] = jnp.dot(u_ref[...], ut_ref[:, cs], preferred_element_type=jnp.float32)

    def mm2(c):
        cs = slice(c * DENSE_CHUNK, (c + 1) * DENSE_CHUNK)
        o_ref[:, cs] += jnp.dot(vt_ref[...], gt_ref[:, cs], preferred_element_type=jnp.float32)

    def weights(c):
        for half in range(DENSE_CHUNK // LANES):
            ls = slice(c * DENSE_CHUNK + half * LANES, c * DENSE_CHUNK + (half + 1) * LANES)
            for il in range(n_i1):
                w = None
                for h in range(PEER_HEADS):
                    th_rows = th_ref[h, j, :, ls]
                    e1_rows = e1_ref[h, j, :, ls]
                    e2 = e2_ref[h, :, ls]
                    wh = jnp.where(e2 >= th_rows[il:il + 1], e2 * e1_rows[il:il + 1], 0.0)
                    w = wh if w is None else w + wh
                rs = slice(il * PEER_N_KEYS, (il + 1) * PEER_N_KEYS)
                a = act_ref[rs, ls]
                gt_ref[rs, ls] = (w * (a * (1.0 + lax.erf(a * np.float32(np.sqrt(0.5)))))).astype(jnp.bfloat16)

    mm1(0)
    for c in range(nch):
        if c + 1 < nch:
            mm1(c + 1)
        weights(c)
        if c >= 1:
            mm2(c - 1)
    mm2(nch - 1)


def _residual_ln_kernel(x_ref, g_ref, pt_ref, lng_ref, lnb_ref, y_ref):
    r = DEEPNORM_ALPHA * x_ref[...] + g_ref[...] * pt_ref[...].T
    y_ref[...] = _ln(r) * lng_ref[...] + lnb_ref[...]


def peer_layer(x, mod, tile_cond, wq_t, keys, u_tab, vt_tab, ln_g, ln_b, *, tt_route, tt, eb):
    t, d = x.shape
    nh = PEER_HEADS
    row = lambda k, tsz: pl.BlockSpec((None, None, 1, d), lambda i, *_: (tile_cond(i, tsz), 3 + k, 0, 0))
    rows_shape = (nh, PEER_N_KEYS // SUBLANES, SUBLANES, t)
    tile_shape = (nh, PEER_N_KEYS, t)
    ut, th, e1, e2 = pl.pallas_call(
        _peer_route_kernel,
        grid=(t // tt_route,),
        in_specs=[pl.BlockSpec((tt_route, d), lambda i: (i, 0)), row(0, tt_route), row(1, tt_route),
                  pl.BlockSpec(wq_t.shape, lambda i: (0, 0)),
                  pl.BlockSpec(keys.shape, lambda i: (0, 0, 0))],
        out_specs=[pl.BlockSpec((d, tt_route), lambda i: (0, i)),
                   pl.BlockSpec(rows_shape[:3] + (tt_route,), lambda i: (0, 0, 0, i)),
                   pl.BlockSpec(rows_shape[:3] + (tt_route,), lambda i: (0, 0, 0, i)),
                   pl.BlockSpec(tile_shape[:2] + (tt_route,), lambda i: (0, 0, i))],
        out_shape=[jax.ShapeDtypeStruct((d, t), jnp.bfloat16),
                   jax.ShapeDtypeStruct(rows_shape, jnp.float32), jax.ShapeDtypeStruct(rows_shape, jnp.float32),
                   jax.ShapeDtypeStruct(tile_shape, jnp.float32)],
        scratch_shapes=[pltpu.VMEM((2 * nh, PEER_N_KEYS, tt_route), jnp.float32),
                        pltpu.VMEM((3 * SUBLANES, LANES), jnp.float32), pltpu.VMEM((3 * SUBLANES, LANES), jnp.float32),
                        pltpu.VMEM((2 * SUBLANES, LANES), jnp.float32), pltpu.VMEM((3 * SUBLANES, LANES), jnp.float32),
                        pltpu.VMEM((3 * SUBLANES, LANES), jnp.float32)],
        compiler_params=pltpu.CompilerParams(dimension_semantics=("arbitrary",), vmem_limit_bytes=VMEM_LIMIT),
        name="peer_route",
    )(x, mod, mod, wq_t, keys)

    ne = PEER_N_EXPERTS // eb
    once = dict(pipeline_mode=pl.Buffered(1))
    pt = pl.pallas_call(
        _peer_dense_kernel,
        grid=(t // tt, ne),
        in_specs=[pl.BlockSpec((d, tt), lambda i, j: (0, i)),
                  pl.BlockSpec(rows_shape[:3] + (tt,), lambda i, j: (0, 0, 0, i), **once),
                  pl.BlockSpec(rows_shape[:3] + (tt,), lambda i, j: (0, 0, 0, i), **once),
                  pl.BlockSpec(tile_shape[:2] + (tt,), lambda i, j: (0, 0, i), **once),
                  pl.BlockSpec((eb, d), lambda i, j: (j, 0)),
                  pl.BlockSpec((None, d, eb), lambda i, j: (j, 0, 0))],
        out_specs=pl.BlockSpec((d, tt), lambda i, j: (0, i)),
        out_shape=jax.ShapeDtypeStruct((d, t), jnp.float32),
        scratch_shapes=[pltpu.VMEM((eb, tt), jnp.float32), pltpu.VMEM((eb, tt), jnp.bfloat16)],
        compiler_params=pltpu.CompilerParams(dimension_semantics=("arbitrary", "arbitrary"),
                                             vmem_limit_bytes=VMEM_LIMIT),
        name="peer_dense",
    )(ut, th, e1, e2, u_tab, vt_tab)

    tl = PEER_ROUTE_TILE
    return pl.pallas_call(
        _residual_ln_kernel,
        grid=(t // tl,),
        in_specs=[pl.BlockSpec((tl, d), lambda i: (i, 0)), row(2, tl), pl.BlockSpec((d, tl), lambda i: (0, i)),
                  pl.BlockSpec((1, d), lambda i: (0, 0)), pl.BlockSpec((1, d), lambda i: (0, 0))],
        out_specs=pl.BlockSpec((tl, d), lambda i: (i, 0)),
        out_shape=jax.ShapeDtypeStruct((t, d), jnp.float32),
        compiler_params=pltpu.CompilerParams(dimension_semantics=("arbitrary",), vmem_limit_bytes=VMEM_LIMIT),
        name="peer_residual_ln",
    )(x, mod, pt, ln_g, ln_b)


def _tile_cond(i, tt):
    ctx_tiles = BATCH * SEQ // tt
    return jnp.where(i < ctx_tiles, 0, 1 + (i - ctx_tiles) // (DEC_SEQ // tt))


def kernel(x_prompt, x_sample, c, cache_mla_ckv, cache_mla_krope, cache_swa_k, cache_swa_v, state_gla,
           c_ctx, w_ada, b_ada, w_in, mla_q_norm, w_uq, mla_kv_norm, w_ukv,
           w_gla_a_fwd, b_gla_a_fwd, w_gla_a_bwd, b_gla_a_bwd, gla_norm, swa_sink,
           w_branch, w_out, ln1_g, ln1_b, ln2_g, ln2_b, w_peer_q, peer_keys, peer_u, peer_v):
    bf16 = jnp.bfloat16
    n_ctx, n_lat = BATCH * SEQ, DEC_BATCH * DEC_SEQ
    n_tok = n_ctx + n_lat
    conds = jnp.concatenate([c_ctx[None, :], c], axis=0)
    x = jnp.concatenate([x_prompt.reshape(n_ctx, D_MODEL), x_sample.reshape(n_lat, D_MODEL)], axis=0)
    src = dict(zip(IN_NAMES, (0,) + IN_OFFSETS))
    seg_args = (n_ctx // GLA_SEG, SEQ // GLA_SEG, DEC_SEQ // GLA_SEG)
    lane_block = lambda name, width: HS_OFFSET[name] // width
    gla_cols = (lane_block('gla_q', HK), lane_block('gla_k', HK), lane_block('gla_v', HV),
                HS_OFFSET['gla_af'] // LANES, lane_block('gla_g', HV))
    z_row = lambda name: HS_OFFSET[name] - gla_cols[3] * LANES
    ctx_states = []
    for l in range(DEPTH):
        mod = (jax.nn.silu(conds) @ w_ada[l] + b_ada[l]).reshape(1 + DEC_BATCH, 6, 1, D_MODEL)
        w_parts = jnp.concatenate([w_in[l][:, src[n]:src[n] + _widths[n]] for n in HS_ORDER], axis=1)
        w_parts = jnp.pad(w_parts, ((0, 0), (0, PARTS_WIDTH - GATES_OFFSET))).astype(bf16)
        hs = in_proj(x, mod, _tile_cond, w_parts, tt=IN_PROJ_TILE)
        blk = lambda name: HS_OFFSET[name] // BRANCH_W
        mla_w = (mla_q_norm[l][None, :], mla_kv_norm[l][None, :], regroup_uq(w_uq[l]).astype(bf16),
                 regroup_ukv(w_ukv[l]).astype(bf16))
        o_a, o_b, o_d, ckv = ctx_branches(
            hs, BATCH, (blk('mla_q'), blk('fnet'), blk('swa_q'), blk('swa_k'), blk('mla_kv')), swa_sink[l], *mla_w)
        o_a, o_b, o_d = lat_branches(
            hs, n_ctx, DEC_BATCH, (blk('mla_q'), blk('swa_q'), blk('fnet'), blk('swa_k'), blk('mla_kv')), swa_sink[l],
            *mla_w, cache_mla_ckv[:, l], cache_mla_krope[:, l], cache_swa_k[:, l], cache_swa_v[:, l], (o_a, o_b, o_d))
        ctx_part = lambda name, lo, hi: hs[:n_ctx, HS_OFFSET[name] + lo:HS_OFFSET[name] + hi]
        kv_heads = lambda name: (ctx_part(name, 0, SWA_KV_HEADS * SWA_HEAD_DIM)
                                 .reshape(BATCH, SEQ, SWA_KV_HEADS, SWA_HEAD_DIM).transpose(0, 2, 1, 3))
        st = (ckv.reshape(BATCH, SEQ, MLA_KV_LORA),
              ctx_part('mla_kv', MLA_KV_LORA, MLA_KV_LORA + MLA_ROPE).reshape(BATCH, SEQ, MLA_ROPE),
              kv_heads('swa_k'), kv_heads('swa_v'))
        w2 = lambda w, name: (jnp.zeros((LANES, HK), jnp.float32)
                              .at[z_row(name):z_row(name) + GLA_GATE_RANK].set(w).astype(bf16))
        zero = jnp.zeros((BATCH, GLA_HEADS, GLA_DK, GLA_DV), jnp.float32)
        o_c, s_f, s_b = gla_branch(
            hs.reshape(n_tok // GLA_SEG, GLA_SEG, PARTS_WIDTH), gla_cols,
            w2(w_gla_a_fwd[l], 'gla_af'), b_gla_a_fwd[l][None, :], w2(w_gla_a_bwd[l], 'gla_ab'), b_gla_a_bwd[l][None, :],
            jnp.tile(gla_norm[l], GLA_HEADS)[None, :],
            jnp.concatenate([zero, state_gla[:, l, 0]], axis=0), jnp.concatenate([zero, state_gla[:, l, 1]], axis=0),
            seg_args)
        ctx_states.append(st + (jnp.stack([s_f[:BATCH], s_b[:BATCH]], axis=1),))
        branches = (o_a, o_b, o_c.reshape(n_tok, BRANCH_W), o_d)
        x1 = merge(x, mod, _tile_cond, branches, w_in[l][:, GATES_OFFSET:].astype(bf16), w_branch[l].astype(bf16),
                   w_out[l].astype(bf16), ln1_g[l][None, :], ln1_b[l][None, :], tt=MERGE_TILE)
        x = peer_layer(
            x1, mod, _tile_cond,
            w_peer_q[l].T.astype(bf16),
            peer_keys[l].reshape(2 * PEER_HEADS, PEER_N_KEYS, PEER_HALF).astype(bf16),
            peer_u[l].astype(bf16),
            peer_v[l].reshape(-1, PEER_EXPERT_BLOCK, D_MODEL).transpose(0, 2, 1).astype(bf16),
            ln2_g[l][None, :], ln2_b[l][None, :],
            tt_route=PEER_ROUTE_TILE, tt=PEER_TOKEN_TILE, eb=PEER_EXPERT_BLOCK)

    h = x[:n_ctx].reshape(BATCH, SEQ, D_MODEL)
    z = x[n_ctx:].reshape(DEC_BATCH, DEC_SEQ, D_MODEL)
    new_mla_ckv = jnp.stack([st[0] for st in ctx_states], axis=1)
    new_mla_krope = jnp.stack([st[1] for st in ctx_states], axis=1)
    new_swa_k = jnp.stack([st[2] for st in ctx_states], axis=1)
    new_swa_v = jnp.stack([st[3] for st in ctx_states], axis=1)
    new_gla_state = jnp.stack([st[4] for st in ctx_states], axis=1)
    return (h, z, new_mla_ckv, new_mla_krope, new_swa_k, new_swa_v, new_gla_state)
```
